```python
import math
import jax, jax.numpy as jnp
from jax import lax
import numpy as np

D_MODEL = 2048
BATCH = 2
SEQ = 4096
DEPTH = 1
DEC_BATCH = 8
DEC_SEQ = 4096
PAST_LEN = 128

POOL_WINDOWS = (2, 4, 8, 16)
N_POOL_GROUPS = 4
POOL_GROUP = 256
POOL_WIDTH = N_POOL_GROUPS * POOL_GROUP
SSM_GROUP = 16
N_SSM_GROUPS = 32
SSM_WIDTH = SSM_GROUP * N_SSM_GROUPS
SSM_STATE = 64
IN_COLS = POOL_WIDTH + SSM_WIDTH + 2 * D_MODEL
N_EXPERT_GROUPS = 4
EXPERTS_PER_GROUP = 8
N_EXPERTS = N_EXPERT_GROUPS * EXPERTS_PER_GROUP
EXPERT_TOPK = 2
D_EXPERT = 512
MOE_BLOCK = 128
N_MOD = 6
EPS = 1e-6

kernel_name = 'hybrid_pool_s5_hiermoe_encoder'

F32 = jnp.float32


def rmsnorm(x, g):
    xf = x.astype(F32)
    r = lax.rsqrt(jnp.mean(xf * xf, axis=-1, keepdims=True) + EPS)
    return (xf * r * g.astype(F32)).astype(x.dtype)


def pool_mixer(u, w_pool, pool_scale):
    b, l, _ = u.shape
    ug = u.reshape(b, l, N_POOL_GROUPS, POOL_GROUP)
    cs = jnp.cumsum(ug.astype(F32), axis=1)
    cs = jnp.concatenate([jnp.zeros_like(cs[:, :1]), cs], axis=1)
    t = jnp.arange(l)
    outs = []
    for k, w in enumerate(POOL_WINDOWS):
        lo = jnp.clip(t - w // 2, 0, l - 1)
        hi = jnp.clip(t + (w - 1 - w // 2), 0, l - 1)
        cnt = (hi - lo + 1).astype(F32)
        csk = cs[:, :, k]
        s = csk[:, hi + 1] - csk[:, lo]
        outs.append(s / cnt[None, :, None])
    mean = jnp.stack(outs, axis=2)
    diff = (mean - ug.astype(F32)).astype(u.dtype)
    y = jnp.einsum('blgc,gcd->blgd', diff, w_pool)
    return y.reshape(b, l, POOL_WIDTH) * pool_scale


def _ssm_combine(e1, e2):
    a1r, a1i, x1r, x1i = e1
    a2r, a2i, x2r, x2i = e2
    ar = a1r * a2r - a1i * a2i
    ai = a1r * a2i + a1i * a2r
    xr = a2r * x1r - a2i * x1i + x2r
    xi = a2r * x1i + a2i * x1r + x2i
    return (ar, ai, xr, xi)


def s5_scan(ug, a_re, a_im, log_dt, b_re, b_im, c_re, c_im, reverse):
    b, l, g, p = ug.shape
    a_re = a_re.astype(F32); a_im = a_im.astype(F32)
    dt = jnp.exp(log_dt.astype(F32))[:, None]
    mag = jnp.exp(a_re * dt)
    ab_re = mag * jnp.cos(a_im * dt)
    ab_im = mag * jnp.sin(a_im * dt)
    den = a_re * a_re + a_im * a_im
    q_re = ((ab_re - 1.0) * a_re + ab_im * a_im) / den
    q_im = (ab_im * a_re - (ab_re - 1.0) * a_im) / den
    b_re = b_re.astype(F32); b_im = b_im.astype(F32)
    bb_re = q_re[:, :, None] * b_re - q_im[:, :, None] * b_im
    bb_im = q_re[:, :, None] * b_im + q_im[:, :, None] * b_re
    bu_re = jnp.einsum('blgp,gnp->blgn', ug, bb_re)
    bu_im = jnp.einsum('blgp,gnp->blgn', ug, bb_im)
    n = a_re.shape[-1]
    ar = jnp.broadcast_to(ab_re[None, None], (1, l, g, n))
    ai = jnp.broadcast_to(ab_im[None, None], (1, l, g, n))
    _, _, xr, xi = lax.associative_scan(_ssm_combine, (ar, ai, bu_re, bu_im), reverse=reverse, axis=1)
    return (jnp.einsum('blgn,gpn->blgp', xr, c_re.astype(F32))
            - jnp.einsum('blgn,gpn->blgp', xi, c_im.astype(F32)))


def s5_mixer(u, fwd, bwd, ssm_d, w_glu, b_glu):
    b, l, _ = u.shape
    ug = u.reshape(b, l, N_SSM_GROUPS, SSM_GROUP).astype(F32)
    y = (s5_scan(ug, *fwd, reverse=False) + s5_scan(ug, *bwd, reverse=True)
         + ssm_d.astype(F32).reshape(N_SSM_GROUPS, SSM_GROUP) * ug)
    z = jax.nn.gelu(y.reshape(b, l, SSM_WIDTH)).astype(u.dtype)
    return z * jax.nn.sigmoid(z @ w_glu + b_glu)


def hier_moe(h, w_grp, b_grp, w_router, b_router, w_exp_gate, w_exp_up, w_exp_down):
    b, l, d = h.shape
    t = b * l
    hf = h.reshape(t, d)
    grp_logits = (hf @ w_grp).astype(F32) + b_grp.astype(F32)
    grp_prob = jax.nn.softmax(grp_logits, axis=-1)
    g_sel = jnp.argmax(grp_logits, axis=-1)
    g_p = jnp.take_along_axis(grp_prob, g_sel[:, None], axis=1)[:, 0]
    exp_logits = ((hf @ w_router).astype(F32) + b_router.astype(F32)).reshape(t, N_EXPERT_GROUPS, EXPERTS_PER_GROUP)
    in_grp = jnp.take_along_axis(exp_logits, g_sel[:, None, None], axis=1)[:, 0]
    top_v, top_i = lax.top_k(in_grp, EXPERT_TOPK)
    gate = g_p[:, None] * jax.nn.softmax(top_v, axis=-1)
    eid = (g_sel[:, None] * EXPERTS_PER_GROUP + top_i).reshape(-1).astype(jnp.int32)
    tok = jnp.repeat(jnp.arange(t, dtype=jnp.int32), EXPERT_TOPK)
    gate = gate.reshape(-1)
    a = t * EXPERT_TOPK
    order = jnp.argsort(eid)
    e_s = eid[order]; tok_s = tok[order]; gate_s = gate[order]
    counts = jnp.zeros((N_EXPERTS,), jnp.int32).at[eid].add(1)
    starts = jnp.cumsum(counts) - counts
    padded = (counts + MOE_BLOCK - 1) // MOE_BLOCK * MOE_BLOCK
    pend = jnp.cumsum(padded)
    pstart = pend - padded
    dest = pstart[e_s] + jnp.arange(a, dtype=jnp.int32) - starts[e_s]
    n_slots = -(-a // MOE_BLOCK) * MOE_BLOCK + N_EXPERTS * MOE_BLOCK
    n_blocks = n_slots // MOE_BLOCK
    slot_tok = jnp.full((n_slots,), t, jnp.int32).at[dest].set(tok_s)
    slot_gate = jnp.zeros((n_slots,), h.dtype).at[dest].set(gate_s.astype(h.dtype))
    block_e = jnp.minimum(jnp.searchsorted(pend, jnp.arange(n_blocks, dtype=jnp.int32) * MOE_BLOCK, side='right'),
                          N_EXPERTS - 1).astype(jnp.int32)
    h_pad = jnp.concatenate([hf, jnp.zeros((1, d), hf.dtype)], axis=0)
    xin = h_pad[slot_tok].reshape(n_blocks, MOE_BLOCK, d)

    def expert_block(args):
        xb, e = args
        return (jax.nn.silu(xb @ w_exp_gate[e]) * (xb @ w_exp_up[e])) @ w_exp_down[e]

    yb = lax.map(expert_block, (xin, block_e)).reshape(n_slots, d) * slot_gate[:, None]
    y = jnp.zeros((t + 1, d), yb.dtype).at[slot_tok].add(yb)[:t]
    return y.reshape(b, l, d).astype(h.dtype)


def encoder_layer(x, c, w_ada, b_ada, norm1_g, w_in, w_pool, pool_scale,
                  ssm_a_re_f, ssm_a_im_f, ssm_log_dt_f, ssm_b_re_f, ssm_b_im_f, ssm_c_re_f, ssm_c_im_f,
                  ssm_a_re_b, ssm_a_im_b, ssm_log_dt_b, ssm_b_re_b, ssm_b_im_b, ssm_c_re_b, ssm_c_im_b,
                  ssm_d, w_glu, b_glu, w_proj_a, w_proj_b, w_out, norm2_g,
                  w_grp, b_grp, w_router, b_router, w_exp_gate, w_exp_up, w_exp_down):
    mod = (jax.nn.silu(c) @ w_ada + b_ada)[:, None, :]
    sh1, sc1, gt1, sh2, sc2, gt2 = jnp.split(mod, N_MOD, axis=-1)
    h = rmsnorm(x, norm1_g) * (1.0 + sc1) + sh1
    proj = h @ w_in
    o1 = POOL_WIDTH
    o2 = o1 + SSM_WIDTH
    o3 = o2 + D_MODEL
    u_a = proj[..., :o1]
    u_b = proj[..., o1:o2]
    g_a = proj[..., o2:o3]
    g_b = proj[..., o3:]
    y_a = pool_mixer(u_a, w_pool, pool_scale) @ w_proj_a
    y_b = s5_mixer(u_b,
                   (ssm_a_re_f, ssm_a_im_f, ssm_log_dt_f, ssm_b_re_f, ssm_b_im_f, ssm_c_re_f, ssm_c_im_f),
                   (ssm_a_re_b, ssm_a_im_b, ssm_log_dt_b, ssm_b_re_b, ssm_b_im_b, ssm_c_re_b, ssm_c_im_b),
                   ssm_d, w_glu, b_glu) @ w_proj_b
    merged = jax.nn.sigmoid(g_a) * y_a + jax.nn.sigmoid(g_b) * y_b
    x = x + gt1 * (merged @ w_out)
    h2 = rmsnorm(x, norm2_g) * (1.0 + sc2) + sh2
    x = x + gt2 * hier_moe(h2, w_grp, b_grp, w_router, b_router, w_exp_gate, w_exp_up, w_exp_down)
    return x


def trunk(x, c, layer_params, final_g):
    for i in range(DEPTH):
        x = encoder_layer(x, c, *[p[i] for p in layer_params])
    return rmsnorm(x, final_g)


def setup_inputs(seed: int = 0) -> dict:
    key = jax.random.key(seed)
    ks = iter(jax.random.split(key, 64))

    def nrm(shape, s):
        return jax.random.normal(next(ks), shape, F32) * s

    L = DEPTH
    D = D_MODEL
    G, N, P = N_SSM_GROUPS, SSM_STATE, SSM_GROUP
    a_im_base = math.pi * jnp.arange(N, dtype=F32)

    def log_dt():
        return jax.random.uniform(next(ks), (L, G), F32, math.log(1e-3), math.log(1e-1))

    return {
        'x_prompt': nrm((BATCH, SEQ, D), 1.0),
        'x_sample': nrm((DEC_BATCH, DEC_SEQ, D), 1.0),
        'c_prompt': nrm((BATCH, D), 1.0),
        'c_sample': nrm((DEC_BATCH, D), 1.0),
        'w_ada': nrm((L, D, N_MOD * D), 0.5 * D ** -0.5),
        'b_ada': nrm((L, N_MOD * D), 0.02),
        'norm1_g': 1.0 + nrm((L, D), 0.02),
        'w_in': nrm((L, D, IN_COLS), D ** -0.5),
        'w_pool': nrm((L, N_POOL_GROUPS, POOL_GROUP, POOL_GROUP), POOL_GROUP ** -0.5),
        'pool_scale': 1.0 + nrm((L, POOL_WIDTH), 0.02),
        'ssm_a_re_f': -0.5 + nrm((L, G, N), 0.01),
        'ssm_a_im_f': a_im_base + nrm((L, G, N), 0.01),
        'ssm_log_dt_f': log_dt(),
        'ssm_b_re_f': nrm((L, G, N, P), (2 * P) ** -0.5),
        'ssm_b_im_f': nrm((L, G, N, P), (2 * P) ** -0.5),
        'ssm_c_re_f': nrm((L, G, P, N), 0.25),
        'ssm_c_im_f': nrm((L, G, P, N), 0.25),
        'ssm_a_re_b': -0.5 + nrm((L, G, N), 0.01),
        'ssm_a_im_b': a_im_base + nrm((L, G, N), 0.01),
        'ssm_log_dt_b': log_dt(),
        'ssm_b_re_b': nrm((L, G, N, P), (2 * P) ** -0.5),
        'ssm_b_im_b': nrm((L, G, N, P), (2 * P) ** -0.5),
        'ssm_c_re_b': nrm((L, G, P, N), 0.25),
        'ssm_c_im_b': nrm((L, G, P, N), 0.25),
        'ssm_d': nrm((L, SSM_WIDTH), 1.0),
        'w_glu': nrm((L, SSM_WIDTH, SSM_WIDTH), SSM_WIDTH ** -0.5),
        'b_glu': nrm((L, SSM_WIDTH), 0.02),
        'w_proj_a': nrm((L, POOL_WIDTH, D), POOL_WIDTH ** -0.5),
        'w_proj_b': nrm((L, SSM_WIDTH, D), SSM_WIDTH ** -0.5),
        'w_out': nrm((L, D, D), D ** -0.5),
        'norm2_g': 1.0 + nrm((L, D), 0.02),
        'w_grp': nrm((L, D, N_EXPERT_GROUPS), D ** -0.5),
        'b_grp': nrm((L, N_EXPERT_GROUPS), 0.01),
        'w_router': nrm((L, D, N_EXPERTS), D ** -0.5),
        'b_router': nrm((L, N_EXPERTS), 0.01),
        'w_exp_gate': nrm((L, N_EXPERTS, D, D_EXPERT), D ** -0.5),
        'w_exp_up': nrm((L, N_EXPERTS, D, D_EXPERT), D ** -0.5),
        'w_exp_down': nrm((L, N_EXPERTS, D_EXPERT, D), D_EXPERT ** -0.5),
        'final_g': 1.0 + nrm((D,), 0.02),
    }


def reference(x_prompt, x_sample, c_prompt, c_sample, w_ada, b_ada, norm1_g, w_in, w_pool, pool_scale,
              ssm_a_re_f, ssm_a_im_f, ssm_log_dt_f, ssm_b_re_f, ssm_b_im_f, ssm_c_re_f, ssm_c_im_f,
              ssm_a_re_b, ssm_a_im_b, ssm_log_dt_b, ssm_b_re_b, ssm_b_im_b, ssm_c_re_b, ssm_c_im_b,
              ssm_d, w_glu, b_glu, w_proj_a, w_proj_b, w_out, norm2_g,
              w_grp, b_grp, w_router, b_router, w_exp_gate, w_exp_up, w_exp_down, final_g):
    layer_params = (w_ada, b_ada, norm1_g, w_in, w_pool, pool_scale,
                    ssm_a_re_f, ssm_a_im_f, ssm_log_dt_f, ssm_b_re_f, ssm_b_im_f, ssm_c_re_f, ssm_c_im_f,
                    ssm_a_re_b, ssm_a_im_b, ssm_log_dt_b, ssm_b_re_b, ssm_b_im_b, ssm_c_re_b, ssm_c_im_b,
                    ssm_d, w_glu, b_glu, w_proj_a, w_proj_b, w_out, norm2_g,
                    w_grp, b_grp, w_router, b_router, w_exp_gate, w_exp_up, w_exp_down)
    y_prompt = trunk(x_prompt, c_prompt, layer_params, final_g)
    y_sample = trunk(x_sample, c_sample, layer_params, final_g)
    return (y_prompt, y_sample)
```

```python
import functools
import math

import jax
import jax.numpy as jnp
from jax import lax
from jax.experimental import pallas as pl
from jax.experimental.pallas import tpu as pltpu

F32 = jnp.float32
BF16 = jnp.bfloat16

D_MODEL = 2048
SEQ = 4096
N_PROMPT = 2
N_SAMPLE = 8
N_SEQ = N_PROMPT + N_SAMPLE
N_TOK = N_SEQ * SEQ
EPS = 1e-6
N_MOD = 6

POOL_WINDOWS = (2, 4, 8, 16)
POOL_GROUP = 256
POOL_WIDTH = 1024
POOL_HALO = 16

SSM_GROUPS = 32
SSM_P = 16
SSM_N = 64
SSM_WIDTH = 512
SSM_CHUNK = 16
SSM_NCHUNK = SEQ // SSM_CHUNK
SSM_ROWS_PER_CHUNK = 16
SSM_COLS = SSM_CHUNK * SSM_P

N_GROUPS = 4
EXPERTS_PER_GROUP = 8
N_EXPERTS = 32
TOPK = 2
D_EXPERT = 512
N_ASSIGN = N_TOK * TOPK
ROUTER_COLS = 128

TILE_INPROJ = 512
TILE_MIX = 256
TILE_FINAL = 512
MOE_ROWS = 256
MOE_BLOCKS = N_ASSIGN // MOE_ROWS + N_EXPERTS
MOE_SLOTS = MOE_BLOCKS * MOE_ROWS

VMEM_LIMIT = 60 * 1024 * 1024


def _params(sem, vmem=VMEM_LIMIT):
    return pltpu.CompilerParams(dimension_semantics=sem, vmem_limit_bytes=vmem)


def _resident(shape):
    zeros = (0,) * len(shape)
    return pl.BlockSpec(shape, lambda *_: zeros, pipeline_mode=pl.Buffered(1))


def _ada_norm(x, gain, scale, shift):
    r = lax.rsqrt(jnp.mean(x * x, axis=-1, keepdims=True) + EPS)
    return (x * r * gain) * (1.0 + scale) + shift


def _mod_kernel(c_ref, w_ref, b_ref, o_ref):
    c = c_ref[...]
    s = c * jax.nn.sigmoid(c)
    o_ref[...] = jnp.dot(s.astype(BF16), w_ref[...].astype(BF16), preferred_element_type=F32) + b_ref[...]


def _mod_call(c_pad, w_ada, b_ada):
    n = w_ada.shape[1]
    tn = 1024
    return pl.pallas_call(
        _mod_kernel,
        grid=(n // tn,),
        in_specs=[pl.BlockSpec(c_pad.shape, lambda j: (0, 0)),
                  pl.BlockSpec((D_MODEL, tn), lambda j: (0, j)),
                  pl.BlockSpec((1, tn), lambda j: (0, j))],
        out_specs=pl.BlockSpec((c_pad.shape[0], tn), lambda j: (0, j)),
        out_shape=jax.ShapeDtypeStruct((c_pad.shape[0], n), F32),
        compiler_params=_params(("arbitrary",)),
        name="mod",
    )(c_pad, w_ada, b_ada)


def _x_specs(tile):
    last = SEQ // tile - 1
    xp = pl.BlockSpec((1, tile, D_MODEL),
                      lambda b, i: (jnp.minimum(b, N_PROMPT - 1), jnp.where(b < N_PROMPT, i, last), 0))
    xs = pl.BlockSpec((1, tile, D_MODEL),
                      lambda b, i: (jnp.maximum(b - N_PROMPT, 0), jnp.where(b < N_PROMPT, 0, i), 0))
    return xp, xs


def _load_x(xp_ref, xs_ref):
    return jnp.where(pl.program_id(0) < N_PROMPT, xp_ref[0], xs_ref[0])


def _inproj_kernel(xp_ref, xs_ref, mod_ref, g_ref, w_ref, ua_ref, ub_ref):
    x = _load_x(xp_ref, xs_ref)
    h = _ada_norm(x, g_ref[...], mod_ref[0, 1:2, :], mod_ref[0, 0:1, :])
    p = jnp.dot(h.astype(BF16), w_ref[...], preferred_element_type=F32)
    ua_ref[0] = p[:, :POOL_WIDTH].astype(BF16)
    ub_ref[0] = p[:, POOL_WIDTH:].astype(BF16)


def _inproj_call(x_prompt, x_sample, mod, norm_g, w_u):
    tile = TILE_INPROJ
    xp, xs = _x_specs(tile)
    return pl.pallas_call(
        _inproj_kernel,
        grid=(N_SEQ, SEQ // tile),
        in_specs=[xp, xs,
                  pl.BlockSpec((1, N_MOD, D_MODEL), lambda b, i: (b, 0, 0)),
                  _resident((1, D_MODEL)),
                  _resident(w_u.shape)],
        out_specs=[pl.BlockSpec((1, tile, POOL_WIDTH), lambda b, i: (b, i, 0)),
                   pl.BlockSpec((1, tile, SSM_WIDTH), lambda b, i: (b, i, 0))],
        out_shape=[jax.ShapeDtypeStruct((N_SEQ, SEQ, POOL_WIDTH), BF16),
                   jax.ShapeDtypeStruct((N_SEQ, SEQ, SSM_WIDTH), BF16)],
        compiler_params=_params(("arbitrary", "arbitrary")),
        name="inproj",
    )(x_prompt, x_sample, mod, norm_g, w_u)


def _s5_kernel(u_ref, ws_ref, t_ref, wf_ref, wb_ref, a_ref, y_ref, s_ref, xf_ref, xb_ref):
    rows = SSM_ROWS_PER_CHUNK
    nrow = SSM_NCHUNK * rows
    half = 2 * SSM_N
    u = u_ref[0]
    s_ref[...] = jnp.dot(u, ws_ref[0], preferred_element_type=F32)
    a_re = jnp.broadcast_to(a_ref[0, 0:1, :], (rows, half))
    a_im = jnp.broadcast_to(a_ref[0, 1:2, :], (rows, half))
    is_fwd = lax.broadcasted_iota(jnp.int32, (rows, half), 1) < SSM_N
    zero_block = jnp.zeros((rows, 2 * half), F32)
    xf_ref[pl.ds(0, rows), :] = zero_block
    xb_ref[pl.ds(nrow, rows), :] = zero_block

    def step(i, carry):
        x_re, x_im = carry
        rf = pl.multiple_of(i * rows, rows)
        rb = pl.multiple_of((SSM_NCHUNK - 1 - i) * rows, rows)
        sf = s_ref[pl.ds(rf, rows), :]
        sb = s_ref[pl.ds(rb, rows), :]
        s_re = jnp.where(is_fwd, sf[:, :half], sb[:, :half])
        s_im = jnp.where(is_fwd, sf[:, half:], sb[:, half:])
        n_re = a_re * x_re - a_im * x_im + s_re
        n_im = a_re * x_im + a_im * x_re + s_im
        x_new = jnp.concatenate([n_re, n_im], axis=1)
        xf_ref[pl.ds(rf + rows, rows), :] = x_new
        xb_ref[pl.ds(rb, rows), :] = x_new
        return n_re, n_im

    zero_half = jnp.zeros((rows, half), F32)
    lax.fori_loop(0, SSM_NCHUNK, step, (zero_half, zero_half))
    y = jnp.dot(u, t_ref[0], preferred_element_type=F32)
    y += jnp.dot(xf_ref[pl.ds(0, nrow), :].astype(BF16), wf_ref[0], preferred_element_type=F32)
    y += jnp.dot(xb_ref[pl.ds(rows, nrow), :].astype(BF16), wb_ref[0], preferred_element_type=F32)
    y_ref[0] = y


def _s5_call(u_chunks, ws, t, wf, wb, a16):
    nrow = SSM_NCHUNK * SSM_ROWS_PER_CHUNK
    mat = pl.BlockSpec((1, SSM_COLS, SSM_COLS), lambda g: (g, 0, 0))
    return pl.pallas_call(
        _s5_kernel,
        grid=(SSM_GROUPS,),
        in_specs=[pl.BlockSpec((1, nrow, SSM_COLS), lambda g: (g, 0, 0)), mat, mat, mat, mat,
                  pl.BlockSpec((1, 2, 2 * SSM_N), lambda g: (g, 0, 0))],
        out_specs=pl.BlockSpec((1, nrow, SSM_COLS), lambda g: (g, 0, 0)),
        out_shape=jax.ShapeDtypeStruct((SSM_GROUPS, nrow, SSM_COLS), F32),
        scratch_shapes=[pltpu.VMEM((nrow, SSM_COLS), F32),
                        pltpu.VMEM((nrow + SSM_ROWS_PER_CHUNK, SSM_COLS), F32),
                        pltpu.VMEM((nrow + SSM_ROWS_PER_CHUNK, SSM_COLS), F32)],
        compiler_params=_params(("arbitrary",)),
        name="s5",
    )(u_chunks, ws, t, wf, wb, a16)


def _s5_direction(a_re, a_im, log_dt, b_re, b_im, c_re, c_im):
    dt = jnp.exp(log_dt)[:, None]
    k = jnp.arange(SSM_CHUNK + 1, dtype=F32)[None, :, None]
    mag = jnp.exp(k * (a_re * dt)[:, None, :])
    ang = k * (a_im * dt)[:, None, :]
    pw_re = mag * jnp.cos(ang)
    pw_im = mag * jnp.sin(ang)
    ab_re, ab_im = pw_re[:, 1], pw_im[:, 1]
    den = a_re * a_re + a_im * a_im
    q_re = ((ab_re - 1.0) * a_re + ab_im * a_im) / den
    q_im = (ab_im * a_re - (ab_re - 1.0) * a_im) / den
    bb_re = q_re[:, :, None] * b_re - q_im[:, :, None] * b_im
    bb_im = q_re[:, :, None] * b_im + q_im[:, :, None] * b_re
    cp_re = c_re[:, None] * pw_re[:, :, None, :] - c_im[:, None] * pw_im[:, :, None, :]
    cp_im = c_re[:, None] * pw_im[:, :, None, :] + c_im[:, None] * pw_re[:, :, None, :]
    taps = (jnp.sum(cp_re[:, :SSM_CHUNK, :, :, None] * bb_re[:, None, None], axis=3)
            - jnp.sum(cp_im[:, :SSM_CHUNK, :, :, None] * bb_im[:, None, None], axis=3))
    return pw_re, pw_im, bb_re, bb_im, cp_re, cp_im, taps


def _s5_operators(fwd, bwd, ssm_d):
    g, q, p, n = SSM_GROUPS, SSM_CHUNK, SSM_P, SSM_N
    pf_re, pf_im, bf_re, bf_im, cf_re, cf_im, taps_f = _s5_direction(*fwd)
    pb_re, pb_im, bb_re, bb_im, cb_re, cb_im, taps_b = _s5_direction(*bwd)
    s_idx = jnp.arange(q)
    lag = s_idx[None, :] - s_idx[:, None]
    tf = jnp.where((lag >= 0)[None, :, :, None, None], taps_f[:, jnp.clip(lag, 0, q - 1)], 0.0)
    tb = jnp.where((lag <= 0)[None, :, :, None, None], taps_b[:, jnp.clip(-lag, 0, q - 1)], 0.0)
    skip = (jnp.eye(q, dtype=F32)[None, :, :, None, None] * jnp.eye(p, dtype=F32)[None, None, None]
            * ssm_d.reshape(g, 1, 1, 1, p))
    t_mat = (tf + tb + skip).transpose(0, 1, 4, 2, 3).reshape(g, q * p, q * p)

    def state_in(pw_re, pw_im, b_re, b_im, powers):
        w_re = pw_re[:, powers][:, :, None, :] * b_re.transpose(0, 2, 1)[:, None] \
            - pw_im[:, powers][:, :, None, :] * b_im.transpose(0, 2, 1)[:, None]
        w_im = pw_re[:, powers][:, :, None, :] * b_im.transpose(0, 2, 1)[:, None] \
            + pw_im[:, powers][:, :, None, :] * b_re.transpose(0, 2, 1)[:, None]
        return w_re.reshape(g, q * p, n), w_im.reshape(g, q * p, n)

    wsf_re, wsf_im = state_in(pf_re, pf_im, bf_re, bf_im, q - 1 - s_idx)
    wsb_re, wsb_im = state_in(pb_re, pb_im, bb_re, bb_im, s_idx)
    ws = jnp.concatenate([wsf_re, wsb_re, wsf_im, wsb_im], axis=2)

    def state_out(cp_re, cp_im, powers):
        o_re = cp_re[:, powers].transpose(0, 3, 1, 2).reshape(g, n, q * p)
        o_im = -cp_im[:, powers].transpose(0, 3, 1, 2).reshape(g, n, q * p)
        return o_re, o_im

    of_re, of_im = state_out(cf_re, cf_im, s_idx + 1)
    ob_re, ob_im = state_out(cb_re, cb_im, q - s_idx)
    zero = jnp.zeros_like(of_re)
    wf = jnp.concatenate([of_re, zero, of_im, zero], axis=1)
    wb = jnp.concatenate([zero, ob_re, zero, ob_im], axis=1)
    a16 = jnp.stack([jnp.concatenate([pf_re[:, q], pb_re[:, q]], axis=1),
                     jnp.concatenate([pf_im[:, q], pb_im[:, q]], axis=1)], axis=1)
    return ws.astype(BF16), t_mat.astype(BF16), wf.astype(BF16), wb.astype(BF16), a16


def _gelu_tanh(x):
    return 0.5 * x * (1.0 + jnp.tanh(math.sqrt(2.0 / math.pi) * (x + 0.044715 * (x * x * x))))


def _mix_kernel(xp_ref, xs_ref, mod_ref, g1_ref, g2_ref, ua_ref, ua_prev_ref, ua_next_ref, ys_ref,
                wg_ref, wpool_ref, pscale_ref, wpa_ref, wglu_ref, bglu_ref, wpb_ref, wout_ref,
                wr_ref, br_ref,
                x1_ref, h2_ref, logit_ref, ext_ref, diff_ref, merged_ref):
    tile = TILE_MIX
    i = pl.program_id(1)
    x = _load_x(xp_ref, xs_ref)
    h = _ada_norm(x, g1_ref[...], mod_ref[0, 1:2, :], mod_ref[0, 0:1, :]).astype(BF16)

    first = i == 0
    last = i == pl.num_programs(1) - 1
    ext_ref[pl.ds(0, POOL_HALO), :] = jnp.where(first, 0.0, ua_prev_ref[0].astype(F32))
    ext_ref[pl.ds(POOL_HALO, tile), :] = ua_ref[0].astype(F32)
    ext_ref[pl.ds(POOL_HALO + tile, POOL_HALO), :] = jnp.where(last, 0.0, ua_next_ref[0].astype(F32))
    pos = i * tile + lax.broadcasted_iota(jnp.int32, (tile, 1), 0)
    for k, w in enumerate(POOL_WINDOWS):
        cols = pl.ds(k * POOL_GROUP, POOL_GROUP)
        lo = jnp.maximum(pos - w // 2, 0)
        hi = jnp.minimum(pos + (w - 1 - w // 2), SEQ - 1)
        inv_cnt = 1.0 / (hi - lo + 1).astype(F32)
        acc = ext_ref[pl.ds(POOL_HALO - w // 2, tile), cols]
        for j in range(1, w):
            acc = acc + ext_ref[pl.ds(POOL_HALO - w // 2 + j, tile), cols]
        diff = acc * inv_cnt - ext_ref[pl.ds(POOL_HALO, tile), cols]
        mixed = jnp.dot(diff.astype(BF16), wpool_ref[k], preferred_element_type=F32)
        diff_ref[:, cols] = (mixed * pscale_ref[:, cols]).astype(BF16)

    z = _gelu_tanh(ys_ref[0])
    zg = z * jax.nn.sigmoid(jnp.dot(z.astype(BF16), wglu_ref[...], preferred_element_type=F32) + bglu_ref[...])
    zg = zg.astype(BF16)
    pa = diff_ref[...]

    chunk = 1024
    for j in range(D_MODEL // chunk):
        c0 = j * chunk
        g_a = jnp.dot(h, wg_ref[:, pl.ds(c0, chunk)], preferred_element_type=F32)
        y_a = jnp.dot(pa, wpa_ref[:, pl.ds(c0, chunk)], preferred_element_type=F32)
        m = jax.nn.sigmoid(g_a) * y_a
        g_b = jnp.dot(h, wg_ref[:, pl.ds(D_MODEL + c0, chunk)], preferred_element_type=F32)
        y_b = jnp.dot(zg, wpb_ref[:, pl.ds(c0, chunk)], preferred_element_type=F32)
        m = m + jax.nn.sigmoid(g_b) * y_b
        merged_ref[:, pl.ds(c0, chunk)] = m.astype(BF16)

    x1 = x + mod_ref[0, 2:3, :] * jnp.dot(merged_ref[...], wout_ref[...], preferred_element_type=F32)
    x1_ref[0] = x1
    h2 = _ada_norm(x1, g2_ref[...], mod_ref[0, 4:5, :], mod_ref[0, 3:4, :])
    h2_ref[0] = h2.astype(BF16)
    logit_ref[0] = jnp.dot(h2, wr_ref[...], preferred_element_type=F32,
                           precision=lax.Precision.HIGHEST) + br_ref[...]


def _mix_call(x_prompt, x_sample, mod, norm1_g, norm2_g, u_a, y_s5, w_g, w_pool, pool_scale, w_pa,
              w_glu, b_glu, w_pb, w_out, w_r, b_r):
    tile = TILE_MIX
    xp, xs = _x_specs(tile)
    halo_per_tile = tile // POOL_HALO
    n_halo = SEQ // POOL_HALO
    seq_tile = lambda width: pl.BlockSpec((1, tile, width), lambda b, i: (b, i, 0))
    return pl.pallas_call(
        _mix_kernel,
        grid=(N_SEQ, SEQ // tile),
        in_specs=[xp, xs,
                  pl.BlockSpec((1, N_MOD, D_MODEL), lambda b, i: (b, 0, 0)),
                  _resident((1, D_MODEL)), _resident((1, D_MODEL)),
                  seq_tile(POOL_WIDTH),
                  pl.BlockSpec((1, POOL_HALO, POOL_WIDTH),
                               lambda b, i: (b, jnp.maximum(i * halo_per_tile - 1, 0), 0)),
                  pl.BlockSpec((1, POOL_HALO, POOL_WIDTH),
                               lambda b, i: (b, jnp.minimum((i + 1) * halo_per_tile, n_halo - 1), 0)),
                  seq_tile(SSM_WIDTH),
                  _resident(w_g.shape), _resident(w_pool.shape), _resident(pool_scale.shape),
                  _resident(w_pa.shape), _resident(w_glu.shape), _resident(b_glu.shape),
                  _resident(w_pb.shape), _resident(w_out.shape), _resident(w_r.shape), _resident(b_r.shape)],
        out_specs=[seq_tile(D_MODEL), seq_tile(D_MODEL), seq_tile(ROUTER_COLS)],
        out_shape=[jax.ShapeDtypeStruct((N_SEQ, SEQ, D_MODEL), F32),
                   jax.ShapeDtypeStruct((N_SEQ, SEQ, D_MODEL), BF16),
                   jax.ShapeDtypeStruct((N_SEQ, SEQ, ROUTER_COLS), F32)],
        scratch_shapes=[pltpu.VMEM((tile + 2 * POOL_HALO, POOL_WIDTH), F32),
                        pltpu.VMEM((tile, POOL_WIDTH), BF16),
                        pltpu.VMEM((tile, D_MODEL), BF16)],
        compiler_params=_params(("arbitrary", "arbitrary")),
        name="mix",
    )(x_prompt, x_sample, mod, norm1_g, norm2_g, u_a, u_a, u_a, y_s5, w_g, w_pool, pool_scale, w_pa,
      w_glu, b_glu, w_pb, w_out, w_r, b_r)


def _expert_kernel(block_e_ref, n_valid_ref, x_ref, wgu_ref, wd_ref, o_ref):
    @pl.when(pl.program_id(0) < n_valid_ref[0])
    def _():
        gu = jnp.dot(x_ref[...], wgu_ref[0], preferred_element_type=F32)
        g = gu[:, :D_EXPERT]
        act = (g * jax.nn.sigmoid(g)) * gu[:, D_EXPERT:]
        o_ref[...] = jnp.dot(act.astype(BF16), wd_ref[0], preferred_element_type=F32)


def _expert_call(block_e, n_valid, x_slots, w_gu, w_d):
    grid_spec = pltpu.PrefetchScalarGridSpec(
        num_scalar_prefetch=2,
        grid=(MOE_BLOCKS,),
        in_specs=[pl.BlockSpec((MOE_ROWS, D_MODEL), lambda i, be, nv: (i, 0)),
                  pl.BlockSpec((1, D_MODEL, 2 * D_EXPERT), lambda i, be, nv: (be[i], 0, 0)),
                  pl.BlockSpec((1, D_EXPERT, D_MODEL), lambda i, be, nv: (be[i], 0, 0))],
        out_specs=pl.BlockSpec((MOE_ROWS, D_MODEL), lambda i, be, nv: (i, 0)),
    )
    return pl.pallas_call(
        _expert_kernel,
        grid_spec=grid_spec,
        out_shape=jax.ShapeDtypeStruct((MOE_SLOTS, D_MODEL), F32),
        compiler_params=_params(("arbitrary",)),
        name="experts",
    )(block_e, n_valid, x_slots, w_gu, w_d)


def _final_kernel(x1_ref, mod_ref, r0_ref, r1_ref, gate_ref, g_ref, o_ref):
    gate = gate_ref[0]
    moe = gate[:, 0:1] * r0_ref[0] + gate[:, 1:2] * r1_ref[0]
    x2 = x1_ref[0] + mod_ref[0, 5:6, :] * moe
    r = lax.rsqrt(jnp.mean(x2 * x2, axis=-1, keepdims=True) + EPS)
    o_ref[0] = x2 * r * g_ref[...]


def _final_call(x1, mod, rows0, rows1, gates, final_g, first_seq, n_seq):
    tile = TILE_FINAL
    return pl.pallas_call(
        _final_kernel,
        grid=(n_seq, SEQ // tile),
        in_specs=[pl.BlockSpec((1, tile, D_MODEL), lambda b, i: (b + first_seq, i, 0)),
                  pl.BlockSpec((1, N_MOD, D_MODEL), lambda b, i: (b + first_seq, 0, 0)),
                  pl.BlockSpec((1, tile, D_MODEL), lambda b, i: (b + first_seq, i, 0)),
                  pl.BlockSpec((1, tile, D_MODEL), lambda b, i: (b + first_seq, i, 0)),
                  pl.BlockSpec((1, tile, TOPK), lambda b, i: (b + first_seq, i, 0)),
                  _resident((1, D_MODEL))],
        out_specs=pl.BlockSpec((1, tile, D_MODEL), lambda b, i: (b, i, 0)),
        out_shape=jax.ShapeDtypeStruct((n_seq, SEQ, D_MODEL), F32),
        compiler_params=_params(("arbitrary", "arbitrary")),
        name="final",
    )(x1, mod, rows0, rows1, gates, final_g)


def _route(logits):
    t = logits.shape[0]
    grp_logits = logits[:, :N_GROUPS]
    grp_prob = jax.nn.softmax(grp_logits, axis=-1)
    g_sel = jnp.argmax(grp_logits, axis=-1)
    g_p = jnp.take_along_axis(grp_prob, g_sel[:, None], axis=1)[:, 0]
    exp_logits = logits[:, N_GROUPS:N_GROUPS + N_EXPERTS].reshape(t, N_GROUPS, EXPERTS_PER_GROUP)
    in_grp = jnp.take_along_axis(exp_logits, g_sel[:, None, None], axis=1)[:, 0]
    top_v, top_i = lax.top_k(in_grp, TOPK)
    gate = g_p[:, None] * jax.nn.softmax(top_v, axis=-1)
    eid = (g_sel[:, None] * EXPERTS_PER_GROUP + top_i).reshape(-1).astype(jnp.int32)
    a = t * TOPK
    order = jnp.argsort(eid)
    e_s = eid[order]
    counts = jnp.zeros((N_EXPERTS,), jnp.int32).at[eid].add(1)
    starts = jnp.cumsum(counts) - counts
    padded = (counts + MOE_ROWS - 1) // MOE_ROWS * MOE_ROWS
    pend = jnp.cumsum(padded)
    pstart = pend - padded
    dest_s = pstart[e_s] + jnp.arange(a, dtype=jnp.int32) - starts[e_s]
    slot_tok = jnp.zeros((MOE_SLOTS,), jnp.int32).at[dest_s].set((order // TOPK).astype(jnp.int32))
    dest = jnp.zeros((a,), jnp.int32).at[order].set(dest_s)
    block_e = jnp.minimum(jnp.searchsorted(pend, jnp.arange(MOE_BLOCKS, dtype=jnp.int32) * MOE_ROWS, side='right'),
                          N_EXPERTS - 1).astype(jnp.int32)
    n_valid = (pend[-1:] // MOE_ROWS).astype(jnp.int32)
    return gate, slot_tok, dest, block_e, n_valid


def kernel(x_prompt, x_sample, c_prompt, c_sample, w_ada, b_ada, norm1_g, w_in, w_pool, pool_scale,
           ssm_a_re_f, ssm_a_im_f, ssm_log_dt_f, ssm_b_re_f, ssm_b_im_f, ssm_c_re_f, ssm_c_im_f,
           ssm_a_re_b, ssm_a_im_b, ssm_log_dt_b, ssm_b_re_b, ssm_b_im_b, ssm_c_re_b, ssm_c_im_b,
           ssm_d, w_glu, b_glu, w_proj_a, w_proj_b, w_out, norm2_g,
           w_grp, b_grp, w_router, b_router, w_exp_gate, w_exp_up, w_exp_down, final_g):
    n_u = POOL_WIDTH + SSM_WIDTH
    c_pad = jnp.concatenate([c_prompt, c_sample, jnp.zeros((16 - N_SEQ, D_MODEL), F32)], axis=0)
    mod = _mod_call(c_pad, w_ada[0], b_ada).reshape(16, N_MOD, D_MODEL)

    w_in_bf = w_in[0].astype(BF16)
    u_a, u_b = _inproj_call(x_prompt, x_sample, mod, norm1_g, w_in_bf[:, :n_u])

    u_c = jnp.pad(u_b, ((0, SSM_ROWS_PER_CHUNK - N_SEQ), (0, 0), (0, 0)))
    u_c = u_c.reshape(SSM_ROWS_PER_CHUNK, SSM_NCHUNK, SSM_CHUNK, SSM_GROUPS, SSM_P)
    u_c = u_c.transpose(3, 1, 0, 2, 4).reshape(SSM_GROUPS, SSM_NCHUNK * SSM_ROWS_PER_CHUNK, SSM_COLS)
    fwd = (ssm_a_re_f[0], ssm_a_im_f[0], ssm_log_dt_f[0], ssm_b_re_f[0], ssm_b_im_f[0], ssm_c_re_f[0], ssm_c_im_f[0])
    bwd = (ssm_a_re_b[0], ssm_a_im_b[0], ssm_log_dt_b[0], ssm_b_re_b[0], ssm_b_im_b[0], ssm_c_re_b[0], ssm_c_im_b[0])
    y_c = _s5_call(u_c, *_s5_operators(fwd, bwd, ssm_d[0]))
    y_s5 = y_c.reshape(SSM_GROUPS, SSM_NCHUNK, SSM_ROWS_PER_CHUNK, SSM_CHUNK, SSM_P)[:, :, :N_SEQ]
    y_s5 = y_s5.transpose(2, 1, 3, 0, 4).reshape(N_SEQ, SEQ, SSM_WIDTH)

    w_r = jnp.concatenate([w_grp[0], w_router[0],
                           jnp.zeros((D_MODEL, ROUTER_COLS - N_GROUPS - N_EXPERTS), F32)], axis=1)
    b_r = jnp.concatenate([b_grp[0], b_router[0],
                           jnp.zeros((ROUTER_COLS - N_GROUPS - N_EXPERTS,), F32)])[None, :]
    x1, h2, logits = _mix_call(
        x_prompt, x_sample, mod, norm1_g, norm2_g, u_a, y_s5, w_in_bf[:, n_u:], w_pool[0].astype(BF16),
        pool_scale, w_proj_a[0].astype(BF16), w_glu[0].astype(BF16), b_glu, w_proj_b[0].astype(BF16),
        w_out[0].astype(BF16), w_r, b_r)

    gate, slot_tok, dest, block_e, n_valid = _route(logits.reshape(N_TOK, ROUTER_COLS))
    x_slots = h2.reshape(N_TOK, D_MODEL)[slot_tok]
    w_gu = jnp.concatenate([w_exp_gate[0], w_exp_up[0]], axis=-1).astype(BF16)
    y_slots = _expert_call(block_e, n_valid, x_slots, w_gu, w_exp_down[0].astype(BF16))
    dest = dest.reshape(N_TOK, TOPK)
    rows0 = y_slots[dest[:, 0]].reshape(N_SEQ, SEQ, D_MODEL)
    rows1 = y_slots[dest[:, 1]].reshape(N_SEQ, SEQ, D_MODEL)
    gates = gate.reshape(N_SEQ, SEQ, TOPK)

    final_g2 = final_g[None, :]
    y_prompt = _final_call(x1, mod, rows0, rows1, gates, final_g2, 0, N_PROMPT)
    y_sample = _final_call(x1, mod, rows0, rows1, gates, final_g2, N_PROMPT, N_SAMPLE)
    return (y_prompt, y_sample)
```

```python
import functools
import math

import jax
import jax.numpy as jnp
from jax import lax
from jax.experimental import pallas as pl
from jax.experimental.pallas import tpu as pltpu

F32 = jnp.float32
BF16 = jnp.bfloat16

D_MODEL = 2048
SEQ = 4096
N_PROMPT = 2
N_SAMPLE = 8
N_SEQ = N_PROMPT + N_SAMPLE
N_TOK = N_SEQ * SEQ
EPS = 1e-6
N_MOD = 6

POOL_WINDOWS = (2, 4, 8, 16)
POOL_GROUP = 256
POOL_WIDTH = 1024
POOL_HALO = 16

SSM_GROUPS = 32
SSM_P = 16
SSM_N = 64
SSM_WIDTH = 512
SSM_CHUNK = 16
SSM_NCHUNK = SEQ // SSM_CHUNK
SSM_ROWS_PER_CHUNK = 16
SSM_COLS = SSM_CHUNK * SSM_P

N_GROUPS = 4
EXPERTS_PER_GROUP = 8
N_EXPERTS = 32
TOPK = 2
D_EXPERT = 512
N_ASSIGN = N_TOK * TOPK
ROUTER_COLS = 128

TILE_INPROJ = 512
TILE_MIX = 256
N_TILES = N_TOK // TILE_MIX
TILES_PER_SEQ = SEQ // TILE_MIX
ROUTE_ROWS = 8
MOE_ROWS = 256
MOE_SHIFT = 8
MOE_BLOCKS = N_ASSIGN // MOE_ROWS
N_VISITS = MOE_BLOCKS + N_EXPERTS - 1

VMEM_LIMIT = 60 * 1024 * 1024


def _params(sem, vmem=VMEM_LIMIT):
    return pltpu.CompilerParams(dimension_semantics=sem, vmem_limit_bytes=vmem)


def _resident(shape):
    zeros = (0,) * len(shape)
    return pl.BlockSpec(shape, lambda *_: zeros, pipeline_mode=pl.Buffered(1))


def _ada_norm(x, gain, scale, shift):
    r = lax.rsqrt(jnp.mean(x * x, axis=-1, keepdims=True) + EPS)
    return (x * r * gain) * (1.0 + scale) + shift


def _mod_kernel(c_ref, w_ref, b_ref, o_ref):
    c = c_ref[...]
    s = c * jax.nn.sigmoid(c)
    o_ref[...] = jnp.dot(s.astype(BF16), w_ref[...].astype(BF16), preferred_element_type=F32) + b_ref[...]


def _mod_call(c_pad, w_ada, b_ada):
    n = w_ada.shape[1]
    tn = 1024
    return pl.pallas_call(
        _mod_kernel,
        grid=(n // tn,),
        in_specs=[pl.BlockSpec(c_pad.shape, lambda j: (0, 0)),
                  pl.BlockSpec((D_MODEL, tn), lambda j: (0, j)),
                  pl.BlockSpec((1, tn), lambda j: (0, j))],
        out_specs=pl.BlockSpec((c_pad.shape[0], tn), lambda j: (0, j)),
        out_shape=jax.ShapeDtypeStruct((c_pad.shape[0], n), F32),
        compiler_params=_params(("arbitrary",)),
        name="mod",
    )(c_pad, w_ada, b_ada)


def _x_specs(tile):
    last = SEQ // tile - 1
    xp = pl.BlockSpec((1, tile, D_MODEL),
                      lambda b, i: (jnp.minimum(b, N_PROMPT - 1), jnp.where(b < N_PROMPT, i, last), 0))
    xs = pl.BlockSpec((1, tile, D_MODEL),
                      lambda b, i: (jnp.maximum(b - N_PROMPT, 0), jnp.where(b < N_PROMPT, 0, i), 0))
    return xp, xs


def _load_x(xp_ref, xs_ref):
    return jnp.where(pl.program_id(0) < N_PROMPT, xp_ref[0], xs_ref[0])


def _inproj_kernel(xp_ref, xs_ref, mod_ref, g_ref, w_ref, ua_ref, ub_ref):
    x = _load_x(xp_ref, xs_ref)
    h = _ada_norm(x, g_ref[...], mod_ref[0, 1:2, :], mod_ref[0, 0:1, :])
    p = jnp.dot(h.astype(BF16), w_ref[...], preferred_element_type=F32)
    ua_ref[0] = p[:, :POOL_WIDTH].astype(BF16)
    ub_ref[0] = p[:, POOL_WIDTH:].astype(BF16)


def _inproj_call(x_prompt, x_sample, mod, norm_g, w_u):
    tile = TILE_INPROJ
    xp, xs = _x_specs(tile)
    return pl.pallas_call(
        _inproj_kernel,
        grid=(N_SEQ, SEQ // tile),
        in_specs=[xp, xs,
                  pl.BlockSpec((1, N_MOD, D_MODEL), lambda b, i: (b, 0, 0)),
                  _resident((1, D_MODEL)),
                  _resident(w_u.shape)],
        out_specs=[pl.BlockSpec((1, tile, POOL_WIDTH), lambda b, i: (b, i, 0)),
                   pl.BlockSpec((1, tile, SSM_WIDTH), lambda b, i: (b, i, 0))],
        out_shape=[jax.ShapeDtypeStruct((N_SEQ, SEQ, POOL_WIDTH), BF16),
                   jax.ShapeDtypeStruct((N_SEQ, SEQ, SSM_WIDTH), BF16)],
        compiler_params=_params(("arbitrary", "arbitrary")),
        name="inproj",
    )(x_prompt, x_sample, mod, norm_g, w_u)


def _s5_kernel(u_ref, ws_ref, t_ref, wf_ref, wb_ref, a_ref, y_ref, s_ref, xf_ref, xb_ref):
    rows = SSM_ROWS_PER_CHUNK
    nrow = SSM_NCHUNK * rows
    half = 2 * SSM_N
    u = u_ref[0]
    s_ref[...] = jnp.dot(u, ws_ref[0], preferred_element_type=F32)
    a_re = jnp.broadcast_to(a_ref[0, 0:1, :], (rows, half))
    a_im = jnp.broadcast_to(a_ref[0, 1:2, :], (rows, half))
    is_fwd = lax.broadcasted_iota(jnp.int32, (rows, half), 1) < SSM_N
    zero_block = jnp.zeros((rows, 2 * half), F32)
    xf_ref[pl.ds(0, rows), :] = zero_block
    xb_ref[pl.ds(nrow, rows), :] = zero_block

    def step(i, carry):
        x_re, x_im = carry
        rf = pl.multiple_of(i * rows, rows)
        rb = pl.multiple_of((SSM_NCHUNK - 1 - i) * rows, rows)
        sf = s_ref[pl.ds(rf, rows), :]
        sb = s_ref[pl.ds(rb, rows), :]
        s_re = jnp.where(is_fwd, sf[:, :half], sb[:, :half])
        s_im = jnp.where(is_fwd, sf[:, half:], sb[:, half:])
        n_re = a_re * x_re - a_im * x_im + s_re
        n_im = a_re * x_im + a_im * x_re + s_im
        x_new = jnp.concatenate([n_re, n_im], axis=1)
        xf_ref[pl.ds(rf + rows, rows), :] = x_new
        xb_ref[pl.ds(rb, rows), :] = x_new
        return n_re, n_im

    zero_half = jnp.zeros((rows, half), F32)
    lax.fori_loop(0, SSM_NCHUNK, step, (zero_half, zero_half))
    y = jnp.dot(u, t_ref[0], preferred_element_type=F32)
    y += jnp.dot(xf_ref[pl.ds(0, nrow), :].astype(BF16), wf_ref[0], preferred_element_type=F32)
    y += jnp.dot(xb_ref[pl.ds(rows, nrow), :].astype(BF16), wb_ref[0], preferred_element_type=F32)
    y_ref[0] = y


def _s5_call(u_chunks, ws, t, wf, wb, a16):
    nrow = SSM_NCHUNK * SSM_ROWS_PER_CHUNK
    mat = pl.BlockSpec((1, SSM_COLS, SSM_COLS), lambda g: (g, 0, 0))
    return pl.pallas_call(
        _s5_kernel,
        grid=(SSM_GROUPS,),
        in_specs=[pl.BlockSpec((1, nrow, SSM_COLS), lambda g: (g, 0, 0)), mat, mat, mat, mat,
                  pl.BlockSpec((1, 2, 2 * SSM_N), lambda g: (g, 0, 0))],
        out_specs=pl.BlockSpec((1, nrow, SSM_COLS), lambda g: (g, 0, 0)),
        out_shape=jax.ShapeDtypeStruct((SSM_GROUPS, nrow, SSM_COLS), F32),
        scratch_shapes=[pltpu.VMEM((nrow, SSM_COLS), F32),
                        pltpu.VMEM((nrow + SSM_ROWS_PER_CHUNK, SSM_COLS), F32),
                        pltpu.VMEM((nrow + SSM_ROWS_PER_CHUNK, SSM_COLS), F32)],
        compiler_params=_params(("arbitrary",)),
        name="s5",
    )(u_chunks, ws, t, wf, wb, a16)


def _s5_direction(a_re, a_im, log_dt, b_re, b_im, c_re, c_im):
    dt = jnp.exp(log_dt)[:, None]
    k = jnp.arange(SSM_CHUNK + 1, dtype=F32)[None, :, None]
    mag = jnp.exp(k * (a_re * dt)[:, None, :])
    ang = k * (a_im * dt)[:, None, :]
    pw_re = mag * jnp.cos(ang)
    pw_im = mag * jnp.sin(ang)
    ab_re, ab_im = pw_re[:, 1], pw_im[:, 1]
    den = a_re * a_re + a_im * a_im
    q_re = ((ab_re - 1.0) * a_re + ab_im * a_im) / den
    q_im = (ab_im * a_re - (ab_re - 1.0) * a_im) / den
    bb_re = q_re[:, :, None] * b_re - q_im[:, :, None] * b_im
    bb_im = q_re[:, :, None] * b_im + q_im[:, :, None] * b_re
    cp_re = c_re[:, None] * pw_re[:, :, None, :] - c_im[:, None] * pw_im[:, :, None, :]
    cp_im = c_re[:, None] * pw_im[:, :, None, :] + c_im[:, None] * pw_re[:, :, None, :]
    taps = (jnp.sum(cp_re[:, :SSM_CHUNK, :, :, None] * bb_re[:, None, None], axis=3)
            - jnp.sum(cp_im[:, :SSM_CHUNK, :, :, None] * bb_im[:, None, None], axis=3))
    return pw_re, pw_im, bb_re, bb_im, cp_re, cp_im, taps


def _s5_operators(fwd, bwd, ssm_d):
    g, q, p, n = SSM_GROUPS, SSM_CHUNK, SSM_P, SSM_N
    pf_re, pf_im, bf_re, bf_im, cf_re, cf_im, taps_f = _s5_direction(*fwd)
    pb_re, pb_im, bb_re, bb_im, cb_re, cb_im, taps_b = _s5_direction(*bwd)
    s_idx = jnp.arange(q)
    lag = s_idx[None, :] - s_idx[:, None]
    tf = jnp.where((lag >= 0)[None, :, :, None, None], taps_f[:, jnp.clip(lag, 0, q - 1)], 0.0)
    tb = jnp.where((lag <= 0)[None, :, :, None, None], taps_b[:, jnp.clip(-lag, 0, q - 1)], 0.0)
    skip = (jnp.eye(q, dtype=F32)[None, :, :, None, None] * jnp.eye(p, dtype=F32)[None, None, None]
            * ssm_d.reshape(g, 1, 1, 1, p))
    t_mat = (tf + tb + skip).transpose(0, 1, 4, 2, 3).reshape(g, q * p, q * p)

    def state_in(pw_re, pw_im, b_re, b_im, powers):
        w_re = pw_re[:, powers][:, :, None, :] * b_re.transpose(0, 2, 1)[:, None] \
            - pw_im[:, powers][:, :, None, :] * b_im.transpose(0, 2, 1)[:, None]
        w_im = pw_re[:, powers][:, :, None, :] * b_im.transpose(0, 2, 1)[:, None] \
            + pw_im[:, powers][:, :, None, :] * b_re.transpose(0, 2, 1)[:, None]
        return w_re.reshape(g, q * p, n), w_im.reshape(g, q * p, n)

    wsf_re, wsf_im = state_in(pf_re, pf_im, bf_re, bf_im, q - 1 - s_idx)
    wsb_re, wsb_im = state_in(pb_re, pb_im, bb_re, bb_im, s_idx)
    ws = jnp.concatenate([wsf_re, wsb_re, wsf_im, wsb_im], axis=2)

    def state_out(cp_re, cp_im, powers):
        o_re = cp_re[:, powers].transpose(0, 3, 1, 2).reshape(g, n, q * p)
        o_im = -cp_im[:, powers].transpose(0, 3, 1, 2).reshape(g, n, q * p)
        return o_re, o_im

    of_re, of_im = state_out(cf_re, cf_im, s_idx + 1)
    ob_re, ob_im = state_out(cb_re, cb_im, q - s_idx)
    zero = jnp.zeros_like(of_re)
    wf = jnp.concatenate([of_re, zero, of_im, zero], axis=1)
    wb = jnp.concatenate([zero, ob_re, zero, ob_im], axis=1)
    a16 = jnp.stack([jnp.concatenate([pf_re[:, q], pb_re[:, q]], axis=1),
                     jnp.concatenate([pf_im[:, q], pb_im[:, q]], axis=1)], axis=1)
    return ws.astype(BF16), t_mat.astype(BF16), wf.astype(BF16), wb.astype(BF16), a16


def _gelu_tanh(x):
    return 0.5 * x * (1.0 + jnp.tanh(math.sqrt(2.0 / math.pi) * (x + 0.044715 * (x * x * x))))


def _route_tile(logits, run_ref, route_ref, gate_ref, counts_ref):
    tile = logits.shape[0]
    neg = -jnp.inf
    lt = logits.T
    row8 = lax.broadcasted_iota(jnp.int32, (EXPERTS_PER_GROUP, tile), 0)
    gl = jnp.where(row8 < N_GROUPS, lt[N_EXPERTS:N_EXPERTS + 8], neg)
    gmax = jnp.max(gl, axis=0, keepdims=True)
    g_sel = jnp.min(jnp.where(gl == gmax, row8, 8), axis=0, keepdims=True)
    g_p = 1.0 / jnp.sum(jnp.exp(gl - gmax), axis=0, keepdims=True)
    in_grp = lt[0:EXPERTS_PER_GROUP]
    for g in range(1, N_GROUPS):
        in_grp = jnp.where(g_sel == g, lt[g * EXPERTS_PER_GROUP:(g + 1) * EXPERTS_PER_GROUP], in_grp)
    m1 = jnp.max(in_grp, axis=0, keepdims=True)
    i1 = jnp.min(jnp.where(in_grp == m1, row8, 8), axis=0, keepdims=True)
    rest = jnp.where(row8 == i1, neg, in_grp)
    m2 = jnp.max(rest, axis=0, keepdims=True)
    i2 = jnp.min(jnp.where(rest == m2, row8, 8), axis=0, keepdims=True)
    e21 = jnp.exp(m2 - m1)
    p1 = 1.0 / (1.0 + e21)
    eid1 = g_sel * EXPERTS_PER_GROUP + i1
    eid2 = g_sel * EXPERTS_PER_GROUP + i2

    row_e = lax.broadcasted_iota(jnp.int32, (N_EXPERTS, tile), 0)
    oh1 = (row_e == eid1).astype(F32)
    oh2 = (row_e == eid2).astype(F32)
    earlier = (lax.broadcasted_iota(jnp.int32, (tile, tile), 0)
               < lax.broadcasted_iota(jnp.int32, (tile, tile), 1)).astype(BF16)
    before1 = jnp.dot(oh1.astype(BF16), earlier, preferred_element_type=F32)
    before2 = jnp.dot(oh2.astype(BF16), earlier, preferred_element_type=F32)
    tot1 = jnp.sum(oh1, axis=1, keepdims=True)
    tot2 = jnp.sum(oh2, axis=1, keepdims=True)
    run = run_ref[:, 0:1]
    rank1 = jnp.sum(oh1 * (before1 + run), axis=0, keepdims=True)
    rank2 = jnp.sum(oh2 * (before2 + (run + tot1)), axis=0, keepdims=True)
    new_run = jnp.broadcast_to(run + tot1 + tot2, run_ref.shape)
    run_ref[...] = new_run
    counts_ref[...] = new_run
    zi = jnp.zeros((ROUTE_ROWS - 4, tile), jnp.int32)
    route_ref[0] = jnp.concatenate([eid1, eid2, rank1.astype(jnp.int32), rank2.astype(jnp.int32), zi], axis=0)
    zf = jnp.zeros((ROUTE_ROWS - 2, tile), F32)
    gate_ref[0] = jnp.concatenate([g_p * p1, g_p * (e21 * p1), zf], axis=0)


def _mix_kernel(xp_ref, xs_ref, mod_ref, g1_ref, g2_ref, ua_ref, ua_prev_ref, ua_next_ref, ys_ref,
                wg_ref, wpool_ref, pscale_ref, wpa_ref, wglu_ref, bglu_ref, wpb_ref, wout_ref,
                wr_ref, br_ref,
                x1_ref, h2_ref, route_ref, gate_ref, counts_ref, ext_ref, diff_ref, merged_ref, run_ref):
    tile = TILE_MIX
    i = pl.program_id(1)

    @pl.when((pl.program_id(0) == 0) & (i == 0))
    def _():
        run_ref[...] = jnp.zeros_like(run_ref)

    x = _load_x(xp_ref, xs_ref)
    h = _ada_norm(x, g1_ref[...], mod_ref[0, 1:2, :], mod_ref[0, 0:1, :]).astype(BF16)

    first = i == 0
    last = i == pl.num_programs(1) - 1
    ext_ref[pl.ds(0, POOL_HALO), :] = jnp.where(first, 0.0, ua_prev_ref[0].astype(F32))
    ext_ref[pl.ds(POOL_HALO, tile), :] = ua_ref[0].astype(F32)
    ext_ref[pl.ds(POOL_HALO + tile, POOL_HALO), :] = jnp.where(last, 0.0, ua_next_ref[0].astype(F32))
    pos = i * tile + lax.broadcasted_iota(jnp.int32, (tile, 1), 0)
    for k, w in enumerate(POOL_WINDOWS):
        cols = pl.ds(k * POOL_GROUP, POOL_GROUP)
        lo = jnp.maximum(pos - w // 2, 0)
        hi = jnp.minimum(pos + (w - 1 - w // 2), SEQ - 1)
        inv_cnt = 1.0 / (hi - lo + 1).astype(F32)
        acc = ext_ref[pl.ds(POOL_HALO - w // 2, tile), cols]
        for j in range(1, w):
            acc = acc + ext_ref[pl.ds(POOL_HALO - w // 2 + j, tile), cols]
        diff = acc * inv_cnt - ext_ref[pl.ds(POOL_HALO, tile), cols]
        mixed = jnp.dot(diff.astype(BF16), wpool_ref[k], preferred_element_type=F32)
        diff_ref[:, cols] = (mixed * pscale_ref[:, cols]).astype(BF16)

    z = _gelu_tanh(ys_ref[0])
    zg = z * jax.nn.sigmoid(jnp.dot(z.astype(BF16), wglu_ref[...], preferred_element_type=F32) + bglu_ref[...])
    zg = zg.astype(BF16)
    pa = diff_ref[...]

    chunk = 1024
    for j in range(D_MODEL // chunk):
        c0 = j * chunk
        g_a = jnp.dot(h, wg_ref[:, pl.ds(c0, chunk)], preferred_element_type=F32)
        y_a = jnp.dot(pa, wpa_ref[:, pl.ds(c0, chunk)], preferred_element_type=F32)
        m = jax.nn.sigmoid(g_a) * y_a
        g_b = jnp.dot(h, wg_ref[:, pl.ds(D_MODEL + c0, chunk)], preferred_element_type=F32)
        y_b = jnp.dot(zg, wpb_ref[:, pl.ds(c0, chunk)], preferred_element_type=F32)
        m = m + jax.nn.sigmoid(g_b) * y_b
        merged_ref[:, pl.ds(c0, chunk)] = m.astype(BF16)

    x1 = x + mod_ref[0, 2:3, :] * jnp.dot(merged_ref[...], wout_ref[...], preferred_element_type=F32)
    x1_ref[0] = x1
    h2 = _ada_norm(x1, g2_ref[...], mod_ref[0, 4:5, :], mod_ref[0, 3:4, :])
    h2_ref[0] = h2
    logits = jnp.dot(h2, wr_ref[...], preferred_element_type=F32,
                     precision=lax.Precision.HIGHEST) + br_ref[...]
    _route_tile(logits, run_ref, route_ref, gate_ref, counts_ref)


def _mix_call(x_prompt, x_sample, mod, norm1_g, norm2_g, u_a, y_s5, w_g, w_pool, pool_scale, w_pa,
              w_glu, b_glu, w_pb, w_out, w_r, b_r):
    tile = TILE_MIX
    xp, xs = _x_specs(tile)
    halo_per_tile = tile // POOL_HALO
    n_halo = SEQ // POOL_HALO
    seq_tile = lambda width: pl.BlockSpec((1, tile, width), lambda b, i: (b, i, 0))
    return pl.pallas_call(
        _mix_kernel,
        grid=(N_SEQ, SEQ // tile),
        in_specs=[xp, xs,
                  pl.BlockSpec((1, N_MOD, D_MODEL), lambda b, i: (b, 0, 0)),
                  _resident((1, D_MODEL)), _resident((1, D_MODEL)),
                  seq_tile(POOL_WIDTH),
                  pl.BlockSpec((1, POOL_HALO, POOL_WIDTH),
                               lambda b, i: (b, jnp.maximum(i * halo_per_tile - 1, 0), 0)),
                  pl.BlockSpec((1, POOL_HALO, POOL_WIDTH),
                               lambda b, i: (b, jnp.minimum((i + 1) * halo_per_tile, n_halo - 1), 0)),
                  seq_tile(SSM_WIDTH),
                  _resident(w_g.shape), _resident(w_pool.shape), _resident(pool_scale.shape),
                  _resident(w_pa.shape), _resident(w_glu.shape), _resident(b_glu.shape),
                  _resident(w_pb.shape), _resident(w_out.shape), _resident(w_r.shape), _resident(b_r.shape)],
        out_specs=[seq_tile(D_MODEL), seq_tile(D_MODEL),
                   pl.BlockSpec((1, ROUTE_ROWS, tile), lambda b, i: (b * TILES_PER_SEQ + i, 0, 0)),
                   pl.BlockSpec((1, ROUTE_ROWS, tile), lambda b, i: (b * TILES_PER_SEQ + i, 0, 0)),
                   pl.BlockSpec((N_EXPERTS, 128), lambda b, i: (0, 0))],
        out_shape=[jax.ShapeDtypeStruct((N_SEQ, SEQ, D_MODEL), F32),
                   jax.ShapeDtypeStruct((N_SEQ, SEQ, D_MODEL), F32),
                   jax.ShapeDtypeStruct((N_TILES, ROUTE_ROWS, tile), jnp.int32),
                   jax.ShapeDtypeStruct((N_TILES, ROUTE_ROWS, tile), F32),
                   jax.ShapeDtypeStruct((N_EXPERTS, 128), F32)],
        scratch_shapes=[pltpu.VMEM((tile + 2 * POOL_HALO, POOL_WIDTH), F32),
                        pltpu.VMEM((tile, POOL_WIDTH), BF16),
                        pltpu.VMEM((tile, D_MODEL), BF16),
                        pltpu.VMEM((N_EXPERTS, 128), F32)],
        compiler_params=_params(("arbitrary", "arbitrary")),
        name="mix",
    )(x_prompt, x_sample, mod, norm1_g, norm2_g, u_a, u_a, u_a, y_s5, w_g, w_pool, pool_scale, w_pa,
      w_glu, b_glu, w_pb, w_out, w_r, b_r)


def _plan_kernel(cnt_ref, route_ref, counts_ref, dest_ref, vblock_ref, vexpert_ref, vlo_ref):
    below = (lax.broadcasted_iota(jnp.int32, (N_EXPERTS, N_EXPERTS), 1)
             < lax.broadcasted_iota(jnp.int32, (N_EXPERTS, N_EXPERTS), 0)).astype(F32)
    starts = jnp.dot(below, counts_ref[...], preferred_element_type=F32, precision=lax.Precision.HIGHEST)
    starts_b = jnp.broadcast_to(starts[:, 0:1], (N_EXPERTS, TILE_MIX))
    row_e = lax.broadcasted_iota(jnp.int32, (N_EXPERTS, TILE_MIX), 0)
    zi = jnp.zeros((ROUTE_ROWS - 2, TILE_MIX), jnp.int32)

    def per_tile(t, carry):
        r = route_ref[t]
        s1 = jnp.sum(jnp.where(row_e == r[0:1], starts_b, 0.0), axis=0, keepdims=True)
        s2 = jnp.sum(jnp.where(row_e == r[1:2], starts_b, 0.0), axis=0, keepdims=True)
        dest_ref[t] = jnp.concatenate([s1.astype(jnp.int32) + r[2:3], s2.astype(jnp.int32) + r[3:4], zi], axis=0)
        return carry

    lax.fori_loop(0, N_TILES, per_tile, 0)

    def per_expert(e, carry):
        v, start, last_e = carry
        cnt = cnt_ref[e]
        end = start + cnt
        first = lax.shift_right_logical(start, MOE_SHIFT)
        n_blk = jnp.where(cnt > 0, lax.shift_right_logical(end - 1, MOE_SHIFT) - first + 1, 0)

        def per_block(k, v):
            blk = first + k
            vblock_ref[v] = blk
            vexpert_ref[v] = e
            vlo_ref[v] = jnp.maximum(start - blk * MOE_ROWS, 0)
            return v + 1

        v = lax.fori_loop(0, n_blk, per_block, v)
        return v, end, jnp.where(cnt > 0, e, last_e)

    v, _, last_e = lax.fori_loop(0, N_EXPERTS, per_expert, (0, 0, 0))

    def idle(k, carry):
        vblock_ref[k] = MOE_BLOCKS - 1
        vexpert_ref[k] = last_e
        vlo_ref[k] = MOE_ROWS
        return carry

    lax.fori_loop(v, N_VISITS, idle, 0)


def _plan_call(cnt, route, counts):
    smem = pl.BlockSpec(memory_space=pltpu.SMEM)
    vmem = pl.BlockSpec(memory_space=pltpu.VMEM)
    visits = jax.ShapeDtypeStruct((N_VISITS,), jnp.int32)
    return pl.pallas_call(
        _plan_kernel,
        in_specs=[smem, vmem, vmem],
        out_specs=[vmem, smem, smem, smem],
        out_shape=[jax.ShapeDtypeStruct((N_TILES, ROUTE_ROWS, TILE_MIX), jnp.int32), visits, visits, visits],
        name="plan",
    )(cnt, route, counts)


def _dest_spec(index_map):
    return pl.BlockSpec((1, 1, TOPK * TILE_MIX), index_map, memory_space=pltpu.SMEM)


def _dispatch_kernel(dest_ref, h_ref, xs_ref, sem):
    tile = TILE_MIX

    def row(r, carry):
        for k in range(TOPK):
            d = dest_ref[0, 0, k * tile + r]
            pltpu.make_async_copy(h_ref.at[pl.ds(r, 1)], xs_ref.at[pl.ds(d, 1)], sem).start()
        return carry

    lax.fori_loop(0, tile, row, 0, unroll=8)
    for k in range(TOPK):
        pltpu.make_async_copy(h_ref, xs_ref.at[pl.ds(0, tile)], sem).wait()


def _dispatch_call(dest, h2):
    return pl.pallas_call(
        _dispatch_kernel,
        grid=(N_TILES,),
        in_specs=[_dest_spec(lambda t: (t, 0, 0)),
                  pl.BlockSpec((TILE_MIX, D_MODEL), lambda t: (t, 0))],
        out_specs=pl.BlockSpec(memory_space=pl.ANY),
        out_shape=jax.ShapeDtypeStruct((N_ASSIGN, D_MODEL), F32),
        scratch_shapes=[pltpu.SemaphoreType.DMA],
        compiler_params=_params(("arbitrary",)),
        name="dispatch",
    )(dest, h2)


def _expert_kernel(vblock_ref, vexpert_ref, vlo_ref, x_ref, wg_ref, wu_ref, wd_ref, o_ref, wgu_s, wd_s):
    v = pl.program_id(0)
    e = vexpert_ref[v]

    @pl.when((v == 0) | (e != vexpert_ref[jnp.maximum(v - 1, 0)]))
    def _():
        wgu_s[:, :D_EXPERT] = wg_ref[0].astype(BF16)
        wgu_s[:, D_EXPERT:] = wu_ref[0].astype(BF16)
        wd_s[...] = wd_ref[0].astype(BF16)

    lo = vlo_ref[v]

    @pl.when(lo < MOE_ROWS)
    def _():
        gu = jnp.dot(x_ref[...].astype(BF16), wgu_s[...], preferred_element_type=F32)
        g = gu[:, :D_EXPERT]
        act = (g * jax.nn.sigmoid(g)) * gu[:, D_EXPERT:]
        res = jnp.dot(act.astype(BF16), wd_s[...], preferred_element_type=F32)

        @pl.when(lo == 0)
        def _():
            o_ref[...] = res

        @pl.when(lo > 0)
        def _():
            rows = lax.broadcasted_iota(jnp.int32, (MOE_ROWS, 1), 0)
            o_ref[...] = jnp.where(rows >= lo, res, o_ref[...])


def _expert_call(vblock, vexpert, vlo, x_slots, w_gate, w_up, w_down):
    grid_spec = pltpu.PrefetchScalarGridSpec(
        num_scalar_prefetch=3,
        grid=(N_VISITS,),
        in_specs=[pl.BlockSpec((MOE_ROWS, D_MODEL), lambda v, vb, ve, vl: (vb[v], 0)),
                  pl.BlockSpec((1, D_MODEL, D_EXPERT), lambda v, vb, ve, vl: (ve[v], 0, 0)),
                  pl.BlockSpec((1, D_MODEL, D_EXPERT), lambda v, vb, ve, vl: (ve[v], 0, 0)),
                  pl.BlockSpec((1, D_EXPERT, D_MODEL), lambda v, vb, ve, vl: (ve[v], 0, 0))],
        out_specs=pl.BlockSpec((MOE_ROWS, D_MODEL), lambda v, vb, ve, vl: (vb[v], 0)),
        scratch_shapes=[pltpu.VMEM((D_MODEL, 2 * D_EXPERT), BF16),
                        pltpu.VMEM((D_EXPERT, D_MODEL), BF16)],
    )
    return pl.pallas_call(
        _expert_kernel,
        grid_spec=grid_spec,
        out_shape=jax.ShapeDtypeStruct((N_ASSIGN, D_MODEL), F32),
        compiler_params=_params(("arbitrary",)),
        name="experts",
    )(vblock, vexpert, vlo, x_slots, w_gate, w_up, w_down)


def _final_kernel(dest_ref, x1_ref, mod_ref, gate_ref, g_ref, y_ref, o_ref, rows_ref, sem):
    tile = TILE_MIX

    def row(r, carry):
        for k in range(TOPK):
            d = dest_ref[0, 0, k * tile + r]
            pltpu.make_async_copy(y_ref.at[pl.ds(d, 1)], rows_ref.at[k, pl.ds(r, 1)], sem).start()
        return carry

    lax.fori_loop(0, tile, row, 0, unroll=8)
    for k in range(TOPK):
        pltpu.make_async_copy(y_ref.at[pl.ds(0, tile)], rows_ref.at[k], sem).wait()
    gate = gate_ref[0].T
    moe = gate[:, 0:1] * rows_ref[0] + gate[:, 1:2] * rows_ref[1]
    x2 = x1_ref[0] + mod_ref[0, 5:6, :] * moe
    r = lax.rsqrt(jnp.mean(x2 * x2, axis=-1, keepdims=True) + EPS)
    o_ref[0] = x2 * r * g_ref[...]


def _final_call(x1, mod, y_slots, dest, gates, final_g, first_seq, n_seq):
    tile = TILE_MIX
    tile_of = lambda b, i: (b + first_seq) * TILES_PER_SEQ + i
    return pl.pallas_call(
        _final_kernel,
        grid=(n_seq, TILES_PER_SEQ),
        in_specs=[_dest_spec(lambda b, i: (tile_of(b, i), 0, 0)),
                  pl.BlockSpec((1, tile, D_MODEL), lambda b, i: (b + first_seq, i, 0)),
                  pl.BlockSpec((1, N_MOD, D_MODEL), lambda b, i: (b + first_seq, 0, 0)),
                  pl.BlockSpec((1, ROUTE_ROWS, tile), lambda b, i: (tile_of(b, i), 0, 0)),
                  _resident((1, D_MODEL)),
                  pl.BlockSpec(memory_space=pl.ANY)],
        out_specs=pl.BlockSpec((1, tile, D_MODEL), lambda b, i: (b, i, 0)),
        out_shape=jax.ShapeDtypeStruct((n_seq, SEQ, D_MODEL), F32),
        scratch_shapes=[pltpu.VMEM((TOPK, tile, D_MODEL), F32), pltpu.SemaphoreType.DMA],
        compiler_params=_params(("arbitrary", "arbitrary")),
        name="final",
    )(dest, x1, mod, gates, final_g, y_slots)


def kernel(x_prompt, x_sample, c_prompt, c_sample, w_ada, b_ada, norm1_g, w_in, w_pool, pool_scale,
           ssm_a_re_f, ssm_a_im_f, ssm_log_dt_f, ssm_b_re_f, ssm_b_im_f, ssm_c_re_f, ssm_c_im_f,
           ssm_a_re_b, ssm_a_im_b, ssm_log_dt_b, ssm_b_re_b, ssm_b_im_b, ssm_c_re_b, ssm_c_im_b,
           ssm_d, w_glu, b_glu, w_proj_a, w_proj_b, w_out, norm2_g,
           w_grp, b_grp, w_router, b_router, w_exp_gate, w_exp_up, w_exp_down, final_g):
    n_u = POOL_WIDTH + SSM_WIDTH
    c_pad = jnp.concatenate([c_prompt, c_sample, jnp.zeros((16 - N_SEQ, D_MODEL), F32)], axis=0)
    mod = _mod_call(c_pad, w_ada[0], b_ada).reshape(16, N_MOD, D_MODEL)

    w_in_bf = w_in[0].astype(BF16)
    u_a, u_b = _inproj_call(x_prompt, x_sample, mod, norm1_g, w_in_bf[:, :n_u])

    u_c = jnp.pad(u_b, ((0, SSM_ROWS_PER_CHUNK - N_SEQ), (0, 0), (0, 0)))
    u_c = u_c.reshape(SSM_ROWS_PER_CHUNK, SSM_NCHUNK, SSM_CHUNK, SSM_GROUPS, SSM_P)
    u_c = u_c.transpose(3, 1, 0, 2, 4).reshape(SSM_GROUPS, SSM_NCHUNK * SSM_ROWS_PER_CHUNK, SSM_COLS)
    fwd = (ssm_a_re_f[0], ssm_a_im_f[0], ssm_log_dt_f[0], ssm_b_re_f[0], ssm_b_im_f[0], ssm_c_re_f[0], ssm_c_im_f[0])
    bwd = (ssm_a_re_b[0], ssm_a_im_b[0], ssm_log_dt_b[0], ssm_b_re_b[0], ssm_b_im_b[0], ssm_c_re_b[0], ssm_c_im_b[0])
    y_c = _s5_call(u_c, *_s5_operators(fwd, bwd, ssm_d[0]))
    y_s5 = y_c.reshape(SSM_GROUPS, SSM_NCHUNK, SSM_ROWS_PER_CHUNK, SSM_CHUNK, SSM_P)[:, :, :N_SEQ]
    y_s5 = y_s5.transpose(2, 1, 3, 0, 4).reshape(N_SEQ, SEQ, SSM_WIDTH)

    w_r = jnp.concatenate([w_router[0], w_grp[0],
                           jnp.zeros((D_MODEL, ROUTER_COLS - N_GROUPS - N_EXPERTS), F32)], axis=1)
    b_r = jnp.concatenate([b_router[0], b_grp[0],
                           jnp.zeros((ROUTER_COLS - N_GROUPS - N_EXPERTS,), F32)])[None, :]
    x1, h2, route, gates, counts = _mix_call(
        x_prompt, x_sample, mod, norm1_g, norm2_g, u_a, y_s5, w_in_bf[:, n_u:], w_pool[0].astype(BF16),
        pool_scale, w_proj_a[0].astype(BF16), w_glu[0].astype(BF16), b_glu, w_proj_b[0].astype(BF16),
        w_out[0].astype(BF16), w_r, b_r)

    dest8, vblock, vexpert, vlo = _plan_call(counts[:, 0].astype(jnp.int32), route, counts)
    dest = dest8[:, :TOPK, :].reshape(N_TILES, 1, TOPK * TILE_MIX)
    x_slots = _dispatch_call(dest, h2.reshape(N_TOK, D_MODEL))
    y_slots = _expert_call(vblock, vexpert, vlo, x_slots, w_exp_gate[0], w_exp_up[0], w_exp_down[0])

    final_g2 = final_g[None, :]
    y_prompt = _final_call(x1, mod, y_slots, dest, gates, final_g2, 0, N_PROMPT)
    y_sample = _final_call(x1, mod, y_slots, dest, gates, final_g2, N_PROMPT, N_SAMPLE)
    return (y_prompt, y_sample)
```

```python
import math

import jax
import jax.numpy as jnp
from jax import lax
from jax.experimental import pallas as pl
from jax.experimental.pallas import tpu as pltpu

F32 = jnp.float32
BF16 = jnp.bfloat16

D_MODEL = 2048
SEQ = 4096
N_PROMPT = 2
N_SAMPLE = 8
N_SEQ = N_PROMPT + N_SAMPLE
N_TOK = N_SEQ * SEQ
EPS = 1e-6
N_MOD = 6

POOL_WINDOWS = (2, 4, 8, 16)
POOL_GROUP = 256
POOL_WIDTH = 1024
POOL_HALO = 16

SSM_GROUPS = 32
SSM_P = 16
SSM_N = 64
SSM_WIDTH = 512
SSM_CHUNK = 16
SSM_NCHUNK = SEQ // SSM_CHUNK
SSM_COLS = SSM_CHUNK * SSM_P
SSM_STATE_ROWS = 4 * SSM_N
SSM_GROUPS_PER_STEP = 8

N_GROUPS = 4
EXPERTS_PER_GROUP = 8
N_EXPERTS = 32
TOPK = 2
D_EXPERT = 512
N_ASSIGN = N_TOK * TOPK
ROUTER_COLS = 128

TILE_INPROJ = 512
TILE_MIX = 256
N_TILES = N_TOK // TILE_MIX
TILES_PER_SEQ = SEQ // TILE_MIX
ROUTE_ROWS = 8
MOE_ROWS = 256
MOE_SHIFT = 8
MOE_BLOCKS = N_ASSIGN // MOE_ROWS
N_VISITS = MOE_BLOCKS + N_EXPERTS - 1

VMEM_LIMIT = 60 * 1024 * 1024


def _params(sem, vmem=VMEM_LIMIT):
    return pltpu.CompilerParams(dimension_semantics=sem, vmem_limit_bytes=vmem)


def _resident(shape):
    zeros = (0,) * len(shape)
    return pl.BlockSpec(shape, lambda *_: zeros, pipeline_mode=pl.Buffered(1))


def _ada_norm(x, gain, scale, shift):
    r = lax.rsqrt(jnp.mean(x * x, axis=-1, keepdims=True) + EPS)
    return (x * r * gain) * (1.0 + scale) + shift


def _regroup_matrix(n_outer, n_inner):
    n = n_outer * n_inner
    dst = lax.broadcasted_iota(jnp.int32, (n, n), 0)
    src = lax.broadcasted_iota(jnp.int32, (n, n), 1)
    shift = n_outer.bit_length() - 1
    return (src == (dst & (n_outer - 1)) * n_inner + lax.shift_right_logical(dst, shift)).astype(BF16)


def _mod_kernel(c_ref, w_ref, b_ref, o_ref):
    c = c_ref[...]
    s = c * jax.nn.sigmoid(c)
    o_ref[...] = jnp.dot(s.astype(BF16), w_ref[...].astype(BF16), preferred_element_type=F32) + b_ref[...]


def _mod_call(c_pad, w_ada, b_ada):
    n = w_ada.shape[1]
    tn = 1024
    return pl.pallas_call(
        _mod_kernel,
        grid=(n // tn,),
        in_specs=[pl.BlockSpec(c_pad.shape, lambda j: (0, 0)),
                  pl.BlockSpec((D_MODEL, tn), lambda j: (0, j)),
                  pl.BlockSpec((1, tn), lambda j: (0, j))],
        out_specs=pl.BlockSpec((c_pad.shape[0], tn), lambda j: (0, j)),
        out_shape=jax.ShapeDtypeStruct((c_pad.shape[0], n), F32),
        compiler_params=_params(("arbitrary",)),
        name="mod",
    )(c_pad, w_ada, b_ada)


def _x_specs(tile):
    last = SEQ // tile - 1
    xp = pl.BlockSpec((1, tile, D_MODEL),
                      lambda b, i: (jnp.minimum(b, N_PROMPT - 1), jnp.where(b < N_PROMPT, i, last), 0))
    xs = pl.BlockSpec((1, tile, D_MODEL),
                      lambda b, i: (jnp.maximum(b - N_PROMPT, 0), jnp.where(b < N_PROMPT, 0, i), 0))
    return xp, xs


def _load_x(xp_ref, xs_ref):
    return jnp.where(pl.program_id(0) < N_PROMPT, xp_ref[0], xs_ref[0])


def _inproj_kernel(xp_ref, xs_ref, mod_ref, g_ref, w_ref, ua_ref, ub_ref):
    x = _load_x(xp_ref, xs_ref)
    h = _ada_norm(x, g_ref[...], mod_ref[0, 1:2, :], mod_ref[0, 0:1, :])
    p = jnp.dot(h.astype(BF16), w_ref[...], preferred_element_type=F32)
    ua_ref[0] = p[:, :POOL_WIDTH].astype(BF16)
    n_chunk = TILE_INPROJ // SSM_CHUNK
    ub = jnp.dot(_regroup_matrix(n_chunk, SSM_CHUNK), p[:, POOL_WIDTH:].astype(BF16), preferred_element_type=F32)
    ub_ref[...] = ub.reshape(SSM_CHUNK, n_chunk, SSM_WIDTH).astype(BF16)


def _inproj_call(x_prompt, x_sample, mod, norm_g, w_u):
    tile = TILE_INPROJ
    xp, xs = _x_specs(tile)
    return pl.pallas_call(
        _inproj_kernel,
        grid=(N_SEQ, SEQ // tile),
        in_specs=[xp, xs,
                  pl.BlockSpec((1, N_MOD, D_MODEL), lambda b, i: (b, 0, 0)),
                  _resident((1, D_MODEL)),
                  _resident(w_u.shape)],
        out_specs=[pl.BlockSpec((1, tile, POOL_WIDTH), lambda b, i: (b, i, 0)),
                   pl.BlockSpec((None, SSM_CHUNK, tile // SSM_CHUNK, SSM_WIDTH), lambda b, i: (b, 0, i, 0))],
        out_shape=[jax.ShapeDtypeStruct((N_SEQ, SEQ, POOL_WIDTH), BF16),
                   jax.ShapeDtypeStruct((N_SEQ, SSM_CHUNK, SSM_NCHUNK, SSM_WIDTH), BF16)],
        compiler_params=_params(("arbitrary", "arbitrary")),
        name="inproj",
    )(x_prompt, x_sample, mod, norm_g, w_u)


def _s5_kernel(u_ref, wst_ref, tt_ref, wot_ref, a_ref, y_ref, ut_ref, yt_ref):
    n, nc, ng = SSM_N, SSM_NCHUNK, SSM_GROUPS_PER_STEP
    for s in range(SSM_CHUNK):
        blk = u_ref[s].astype(F32).T
        ut_ref[:, pl.ds(s * SSM_P, SSM_P), :] = blk.reshape(ng, SSM_P, nc).astype(BF16)

    lane = lax.broadcasted_iota(jnp.int32, (n, nc), 1)
    wide = lambda v: jnp.concatenate([v, v], axis=1)

    def shifted(x, d, forward):
        if forward:
            return jnp.where(lane >= d, pltpu.roll(x, d, 1), 0.0)
        return jnp.where(lane < nc - d, pltpu.roll(x, nc - d, 1), 0.0)

    def scan(x_re, x_im, a_re, a_im, forward):
        d = 1
        while d < nc:
            s_re = shifted(x_re, d, forward)
            s_im = shifted(x_im, d, forward)
            x_re, x_im = x_re + (a_re * s_re - a_im * s_im), x_im + (a_re * s_im + a_im * s_re)
            a_re, a_im = a_re * a_re - a_im * a_im, 2.0 * (a_re * a_im)
            d *= 2
        return x_re, x_im

    def per_group(g, carry):
        ut = ut_ref[g]
        a = a_ref[g]
        st = jnp.dot(wst_ref[g], ut, preferred_element_type=F32)
        f_re, f_im = scan(st[0:n], st[n:2 * n], wide(a[0:n]), wide(a[n:2 * n]), True)
        b_re, b_im = scan(st[2 * n:3 * n], st[3 * n:4 * n], wide(a[2 * n:3 * n]), wide(a[3 * n:4 * n]), False)
        carried = jnp.concatenate([shifted(f_re, 1, True), shifted(f_im, 1, True),
                                   shifted(b_re, 1, False), shifted(b_im, 1, False)], axis=0)
        yt_ref[g] = (jnp.dot(tt_ref[g], ut, preferred_element_type=F32)
                     + jnp.dot(wot_ref[g], carried.astype(BF16), preferred_element_type=F32))
        return carry

    lax.fori_loop(0, ng, per_group, 0)
    for t in range(SSM_CHUNK):
        y_ref[t] = yt_ref[:, pl.ds(t * SSM_P, SSM_P), :].reshape(ng * SSM_P, nc).T


def _s5_call(u_ph, wst, tt, wot, a16):
    ng = SSM_GROUPS_PER_STEP
    seq = pl.BlockSpec((None, SSM_CHUNK, SSM_NCHUNK, ng * SSM_P), lambda q, b: (b, 0, 0, q))
    mat = pl.BlockSpec((ng, SSM_COLS, SSM_COLS), lambda q, b: (q, 0, 0))
    return pl.pallas_call(
        _s5_kernel,
        grid=(SSM_GROUPS // ng, N_SEQ),
        in_specs=[seq, mat, mat, mat, pl.BlockSpec((ng, SSM_STATE_ROWS, 128), lambda q, b: (q, 0, 0))],
        out_specs=seq,
        out_shape=jax.ShapeDtypeStruct((N_SEQ, SSM_CHUNK, SSM_NCHUNK, SSM_WIDTH), F32),
        scratch_shapes=[pltpu.VMEM((ng, SSM_COLS, SSM_NCHUNK), BF16),
                        pltpu.VMEM((ng, SSM_COLS, SSM_NCHUNK), F32)],
        compiler_params=_params(("arbitrary", "arbitrary")),
        name="s5",
    )(u_ph, wst, tt, wot, a16)


def _s5_direction(a_re, a_im, log_dt, b_re, b_im, c_re, c_im):
    dt = jnp.exp(log_dt)[:, None]
    k = jnp.arange(SSM_CHUNK + 1, dtype=F32)[None, :, None]
    mag = jnp.exp(k * (a_re * dt)[:, None, :])
    ang = k * (a_im * dt)[:, None, :]
    pw_re = mag * jnp.cos(ang)
    pw_im = mag * jnp.sin(ang)
    ab_re, ab_im = pw_re[:, 1], pw_im[:, 1]
    den = a_re * a_re + a_im * a_im
    q_re = ((ab_re - 1.0) * a_re + ab_im * a_im) / den
    q_im = (ab_im * a_re - (ab_re - 1.0) * a_im) / den
    bb_re = q_re[:, :, None] * b_re - q_im[:, :, None] * b_im
    bb_im = q_re[:, :, None] * b_im + q_im[:, :, None] * b_re
    cp_re = c_re[:, None] * pw_re[:, :, None, :] - c_im[:, None] * pw_im[:, :, None, :]
    cp_im = c_re[:, None] * pw_im[:, :, None, :] + c_im[:, None] * pw_re[:, :, None, :]
    taps = (jnp.sum(cp_re[:, :SSM_CHUNK, :, :, None] * bb_re[:, None, None], axis=3)
            - jnp.sum(cp_im[:, :SSM_CHUNK, :, :, None] * bb_im[:, None, None], axis=3))
    return pw_re, pw_im, bb_re, bb_im, cp_re, cp_im, taps


def _s5_operators(fwd, bwd, ssm_d):
    g, q, p, n = SSM_GROUPS, SSM_CHUNK, SSM_P, SSM_N
    pf_re, pf_im, bf_re, bf_im, cf_re, cf_im, taps_f = _s5_direction(*fwd)
    pb_re, pb_im, bb_re, bb_im, cb_re, cb_im, taps_b = _s5_direction(*bwd)
    s_idx = jnp.arange(q)
    lag = s_idx[None, :] - s_idx[:, None]
    tf = jnp.where((lag >= 0)[None, :, :, None, None], taps_f[:, jnp.clip(lag, 0, q - 1)], 0.0)
    tb = jnp.where((lag <= 0)[None, :, :, None, None], taps_b[:, jnp.clip(-lag, 0, q - 1)], 0.0)
    skip = (jnp.eye(q, dtype=F32)[None, :, :, None, None] * jnp.eye(p, dtype=F32)[None, None, None]
            * ssm_d.reshape(g, 1, 1, 1, p))
    tt = (tf + tb + skip).transpose(0, 2, 3, 1, 4).reshape(g, q * p, q * p)

    def state_in(pw_re, pw_im, b_re, b_im, powers):
        w_re = pw_re[:, powers][:, :, None, :] * b_re.transpose(0, 2, 1)[:, None] \
            - pw_im[:, powers][:, :, None, :] * b_im.transpose(0, 2, 1)[:, None]
        w_im = pw_re[:, powers][:, :, None, :] * b_im.transpose(0, 2, 1)[:, None] \
            + pw_im[:, powers][:, :, None, :] * b_re.transpose(0, 2, 1)[:, None]
        return w_re.reshape(g, q * p, n), w_im.reshape(g, q * p, n)

    wsf_re, wsf_im = state_in(pf_re, pf_im, bf_re, bf_im, q - 1 - s_idx)
    wsb_re, wsb_im = state_in(pb_re, pb_im, bb_re, bb_im, s_idx)
    wst = jnp.concatenate([wsf_re, wsf_im, wsb_re, wsb_im], axis=2).transpose(0, 2, 1)

    def state_out(cp_re, cp_im, powers):
        o_re = cp_re[:, powers].reshape(g, q * p, n)
        o_im = -cp_im[:, powers].reshape(g, q * p, n)
        return o_re, o_im

    of_re, of_im = state_out(cf_re, cf_im, s_idx + 1)
    ob_re, ob_im = state_out(cb_re, cb_im, q - s_idx)
    wot = jnp.concatenate([of_re, of_im, ob_re, ob_im], axis=2)
    a16 = jnp.concatenate([pf_re[:, q], pf_im[:, q], pb_re[:, q], pb_im[:, q]], axis=1)
    a16 = jnp.broadcast_to(a16[:, :, None], (g, SSM_STATE_ROWS, 128))
    return wst.astype(BF16), tt.astype(BF16), wot.astype(BF16), a16


def _gelu_tanh(x):
    return 0.5 * x * (1.0 + jnp.tanh(math.sqrt(2.0 / math.pi) * (x + 0.044715 * (x * x * x))))


def _route_tile(logits, run_ref, route_ref, gate_ref, counts_ref):
    tile = logits.shape[0]
    neg = -jnp.inf
    lt = logits.T
    row8 = lax.broadcasted_iota(jnp.int32, (EXPERTS_PER_GROUP, tile), 0)
    gl = jnp.where(row8 < N_GROUPS, lt[N_EXPERTS:N_EXPERTS + 8], neg)
    gmax = jnp.max(gl, axis=0, keepdims=True)
    g_sel = jnp.min(jnp.where(gl == gmax, row8, 8), axis=0, keepdims=True)
    g_p = 1.0 / jnp.sum(jnp.exp(gl - gmax), axis=0, keepdims=True)
    in_grp = lt[0:EXPERTS_PER_GROUP]
    for g in range(1, N_GROUPS):
        in_grp = jnp.where(g_sel == g, lt[g * EXPERTS_PER_GROUP:(g + 1) * EXPERTS_PER_GROUP], in_grp)
    m1 = jnp.max(in_grp, axis=0, keepdims=True)
    i1 = jnp.min(jnp.where(in_grp == m1, row8, 8), axis=0, keepdims=True)
    rest = jnp.where(row8 == i1, neg, in_grp)
    m2 = jnp.max(rest, axis=0, keepdims=True)
    i2 = jnp.min(jnp.where(rest == m2, row8, 8), axis=0, keepdims=True)
    e21 = jnp.exp(m2 - m1)
    p1 = 1.0 / (1.0 + e21)
    eid1 = g_sel * EXPERTS_PER_GROUP + i1
    eid2 = g_sel * EXPERTS_PER_GROUP + i2

    row_e = lax.broadcasted_iota(jnp.int32, (N_EXPERTS, tile), 0)
    oh1 = (row_e == eid1).astype(F32)
    oh2 = (row_e == eid2).astype(F32)
    earlier = (lax.broadcasted_iota(jnp.int32, (tile, tile), 0)
               < lax.broadcasted_iota(jnp.int32, (tile, tile), 1)).astype(BF16)
    before1 = jnp.dot(oh1.astype(BF16), earlier, preferred_element_type=F32)
    before2 = jnp.dot(oh2.astype(BF16), earlier, preferred_element_type=F32)
    tot1 = jnp.sum(oh1, axis=1, keepdims=True)
    tot2 = jnp.sum(oh2, axis=1, keepdims=True)
    run = run_ref[:, 0:1]
    rank1 = jnp.sum(oh1 * (before1 + run), axis=0, keepdims=True)
    rank2 = jnp.sum(oh2 * (before2 + (run + tot1)), axis=0, keepdims=True)
    new_run = jnp.broadcast_to(run + tot1 + tot2, run_ref.shape)
    run_ref[...] = new_run
    counts_ref[...] = new_run
    zi = jnp.zeros((ROUTE_ROWS - 4, tile), jnp.int32)
    route_ref[0] = jnp.concatenate([eid1, eid2, rank1.astype(jnp.int32), rank2.astype(jnp.int32), zi], axis=0)
    zf = jnp.zeros((ROUTE_ROWS - 2, tile), F32)
    gate_ref[0] = jnp.concatenate([g_p * p1, g_p * (e21 * p1), zf], axis=0)


def _mix_kernel(xp_ref, xs_ref, mod_ref, g1_ref, g2_ref, ua_ref, ua_prev_ref, ua_next_ref, ys_ref,
                wg_ref, wpool_ref, pscale_ref, wpa_ref, wglu_ref, bglu_ref, wpb_ref, wout_ref,
                wr_ref, br_ref,
                x1_ref, h2_ref, route_ref, gate_ref, counts_ref, ext_ref, diff_ref, merged_ref, run_ref):
    tile = TILE_MIX
    i = pl.program_id(1)

    @pl.when((pl.program_id(0) == 0) & (i == 0))
    def _():
        run_ref[...] = jnp.zeros_like(run_ref)

    x = _load_x(xp_ref, xs_ref)
    h = _ada_norm(x, g1_ref[...], mod_ref[0, 1:2, :], mod_ref[0, 0:1, :]).astype(BF16)

    first = i == 0
    last = i == pl.num_programs(1) - 1
    ext_ref[pl.ds(0, POOL_HALO), :] = jnp.where(first, 0.0, ua_prev_ref[0].astype(F32))
    ext_ref[pl.ds(POOL_HALO, tile), :] = ua_ref[0].astype(F32)
    ext_ref[pl.ds(POOL_HALO + tile, POOL_HALO), :] = jnp.where(last, 0.0, ua_next_ref[0].astype(F32))
    pos = i * tile + lax.broadcasted_iota(jnp.int32, (tile, 1), 0)
    for k, w in enumerate(POOL_WINDOWS):
        cols = pl.ds(k * POOL_GROUP, POOL_GROUP)
        lo = jnp.maximum(pos - w // 2, 0)
        hi = jnp.minimum(pos + (w - 1 - w // 2), SEQ - 1)
        inv_cnt = 1.0 / (hi - lo + 1).astype(F32)
        acc = ext_ref[pl.ds(POOL_HALO - w // 2, tile), cols]
        for j in range(1, w):
            acc = acc + ext_ref[pl.ds(POOL_HALO - w // 2 + j, tile), cols]
        diff = acc * inv_cnt - ext_ref[pl.ds(POOL_HALO, tile), cols]
        mixed = jnp.dot(diff.astype(BF16), wpool_ref[k], preferred_element_type=F32)
        diff_ref[:, cols] = (mixed * pscale_ref[:, cols]).astype(BF16)

    z = _gelu_tanh(ys_ref[...].reshape(tile, SSM_WIDTH))
    zg = z * jax.nn.sigmoid(jnp.dot(z.astype(BF16), wglu_ref[...], preferred_element_type=F32) + bglu_ref[...])
    zg = jnp.dot(_regroup_matrix(SSM_CHUNK, tile // SSM_CHUNK), zg.astype(BF16),
                 preferred_element_type=F32).astype(BF16)
    pa = diff_ref[...]

    chunk = 1024
    for j in range(D_MODEL // chunk):
        c0 = j * chunk
        g_a = jnp.dot(h, wg_ref[:, pl.ds(c0, chunk)], preferred_element_type=F32)
        y_a = jnp.dot(pa, wpa_ref[:, pl.ds(c0, chunk)], preferred_element_type=F32)
        m = jax.nn.sigmoid(g_a) * y_a
        g_b = jnp.dot(h, wg_ref[:, pl.ds(D_MODEL + c0, chunk)], preferred_element_type=F32)
        y_b = jnp.dot(zg, wpb_ref[:, pl.ds(c0, chunk)], preferred_element_type=F32)
        m = m + jax.nn.sigmoid(g_b) * y_b
        merged_ref[:, pl.ds(c0, chunk)] = m.astype(BF16)

    x1 = x + mod_ref[0, 2:3, :] * jnp.dot(merged_ref[...], wout_ref[...], preferred_element_type=F32)
    x1_ref[0] = x1
    h2 = _ada_norm(x1, g2_ref[...], mod_ref[0, 4:5, :], mod_ref[0, 3:4, :])
    h2_ref[0] = h2
    logits = jnp.dot(h2, wr_ref[...], preferred_element_type=F32,
                     precision=lax.Precision.HIGHEST) + br_ref[...]
    _route_tile(logits, run_ref, route_ref, gate_ref, counts_ref)


def _mix_call(x_prompt, x_sample, mod, norm1_g, norm2_g, u_a, y_s5, w_g, w_pool, pool_scale, w_pa,
              w_glu, b_glu, w_pb, w_out, w_r, b_r):
    tile = TILE_MIX
    xp, xs = _x_specs(tile)
    halo_per_tile = tile // POOL_HALO
    n_halo = SEQ // POOL_HALO
    seq_tile = lambda width: pl.BlockSpec((1, tile, width), lambda b, i: (b, i, 0))
    return pl.pallas_call(
        _mix_kernel,
        grid=(N_SEQ, SEQ // tile),
        in_specs=[xp, xs,
                  pl.BlockSpec((1, N_MOD, D_MODEL), lambda b, i: (b, 0, 0)),
                  _resident((1, D_MODEL)), _resident((1, D_MODEL)),
                  seq_tile(POOL_WIDTH),
                  pl.BlockSpec((1, POOL_HALO, POOL_WIDTH),
                               lambda b, i: (b, jnp.maximum(i * halo_per_tile - 1, 0), 0)),
                  pl.BlockSpec((1, POOL_HALO, POOL_WIDTH),
                               lambda b, i: (b, jnp.minimum((i + 1) * halo_per_tile, n_halo - 1), 0)),
                  pl.BlockSpec((None, SSM_CHUNK, tile // SSM_CHUNK, SSM_WIDTH), lambda b, i: (b, 0, i, 0)),
                  _resident(w_g.shape), _resident(w_pool.shape), _resident(pool_scale.shape),
                  _resident(w_pa.shape), _resident(w_glu.shape), _resident(b_glu.shape),
                  _resident(w_pb.shape), _resident(w_out.shape), _resident(w_r.shape), _resident(b_r.shape)],
        out_specs=[seq_tile(D_MODEL), seq_tile(D_MODEL),
                   pl.BlockSpec((1, ROUTE_ROWS, tile), lambda b, i: (b * TILES_PER_SEQ + i, 0, 0)),
                   pl.BlockSpec((1, ROUTE_ROWS, tile), lambda b, i: (b * TILES_PER_SEQ + i, 0, 0)),
                   pl.BlockSpec((N_EXPERTS, 128), lambda b, i: (0, 0))],
        out_shape=[jax.ShapeDtypeStruct((N_SEQ, SEQ, D_MODEL), F32),
                   jax.ShapeDtypeStruct((N_SEQ, SEQ, D_MODEL), F32),
                   jax.ShapeDtypeStruct((N_TILES, ROUTE_ROWS, tile), jnp.int32),
                   jax.ShapeDtypeStruct((N_TILES, ROUTE_ROWS, tile), F32),
                   jax.ShapeDtypeStruct((N_EXPERTS, 128), F32)],
        scratch_shapes=[pltpu.VMEM((tile + 2 * POOL_HALO, POOL_WIDTH), F32),
                        pltpu.VMEM((tile, POOL_WIDTH), BF16),
                        pltpu.VMEM((tile, D_MODEL), BF16),
                        pltpu.VMEM((N_EXPERTS, 128), F32)],
        compiler_params=_params(("arbitrary", "arbitrary")),
        name="mix",
    )(x_prompt, x_sample, mod, norm1_g, norm2_g, u_a, u_a, u_a, y_s5, w_g, w_pool, pool_scale, w_pa,
      w_glu, b_glu, w_pb, w_out, w_r, b_r)


def _plan_kernel(cnt_ref, route_ref, counts_ref, dest_ref, vblock_ref, vexpert_ref, vlo_ref):
    below = (lax.broadcasted_iota(jnp.int32, (N_EXPERTS, N_EXPERTS), 1)
             < lax.broadcasted_iota(jnp.int32, (N_EXPERTS, N_EXPERTS), 0)).astype(F32)
    starts = jnp.dot(below, counts_ref[...], preferred_element_type=F32, precision=lax.Precision.HIGHEST)
    starts_b = jnp.broadcast_to(starts[:, 0:1], (N_EXPERTS, TILE_MIX))
    row_e = lax.broadcasted_iota(jnp.int32, (N_EXPERTS, TILE_MIX), 0)
    zi = jnp.zeros((ROUTE_ROWS - 2, TILE_MIX), jnp.int32)

    def per_tile(t, carry):
        r = route_ref[t]
        s1 = jnp.sum(jnp.where(row_e == r[0:1], starts_b, 0.0), axis=0, keepdims=True)
        s2 = jnp.sum(jnp.where(row_e == r[1:2], starts_b, 0.0), axis=0, keepdims=True)
        dest_ref[t] = jnp.concatenate([s1.astype(jnp.int32) + r[2:3], s2.astype(jnp.int32) + r[3:4], zi], axis=0)
        return carry

    lax.fori_loop(0, N_TILES, per_tile, 0)

    def per_expert(e, carry):
        v, start, last_e = carry
        cnt = cnt_ref[e]
        end = start + cnt
        first = lax.shift_right_logical(start, MOE_SHIFT)
        n_blk = jnp.where(cnt > 0, lax.shift_right_logical(end - 1, MOE_SHIFT) - first + 1, 0)

        def per_block(k, v):
            blk = first + k
            vblock_ref[v] = blk
            vexpert_ref[v] = e
            vlo_ref[v] = jnp.maximum(start - blk * MOE_ROWS, 0)
            return v + 1

        v = lax.fori_loop(0, n_blk, per_block, v)
        return v, end, jnp.where(cnt > 0, e, last_e)

    v, _, last_e = lax.fori_loop(0, N_EXPERTS, per_expert, (0, 0, 0))

    def idle(k, carry):
        vblock_ref[k] = MOE_BLOCKS - 1
        vexpert_ref[k] = last_e
        vlo_ref[k] = MOE_ROWS
        return carry

    lax.fori_loop(v, N_VISITS, idle, 0)


def _plan_call(cnt, route, counts):
    smem = pl.BlockSpec(memory_space=pltpu.SMEM)
    vmem = pl.BlockSpec(memory_space=pltpu.VMEM)
    visits = jax.ShapeDtypeStruct((N_VISITS,), jnp.int32)
    return pl.pallas_call(
        _plan_kernel,
        in_specs=[smem, vmem, vmem],
        out_specs=[vmem, smem, smem, smem],
        out_shape=[jax.ShapeDtypeStruct((N_TILES, ROUTE_ROWS, TILE_MIX), jnp.int32), visits, visits, visits],
        name="plan",
    )(cnt, route, counts)


def _dest_spec(index_map):
    return pl.BlockSpec((1, 1, TOPK * TILE_MIX), index_map, memory_space=pltpu.SMEM)


def _dispatch_kernel(dest_ref, h_ref, xs_ref, sem):
    tile = TILE_MIX

    def row(r, carry):
        for k in range(TOPK):
            d = dest_ref[0, 0, k * tile + r]
            pltpu.make_async_copy(h_ref.at[pl.ds(r, 1)], xs_ref.at[pl.ds(d, 1)], sem).start()
        return carry

    lax.fori_loop(0, tile, row, 0, unroll=8)
    for k in range(TOPK):
        pltpu.make_async_copy(h_ref, xs_ref.at[pl.ds(0, tile)], sem).wait()


def _dispatch_call(dest, h2):
    return pl.pallas_call(
        _dispatch_kernel,
        grid=(N_TILES,),
        in_specs=[_dest_spec(lambda t: (t, 0, 0)),
                  pl.BlockSpec((TILE_MIX, D_MODEL), lambda t: (t, 0))],
        out_specs=pl.BlockSpec(memory_space=pl.ANY),
        out_shape=jax.ShapeDtypeStruct((N_ASSIGN, D_MODEL), F32),
        scratch_shapes=[pltpu.SemaphoreType.DMA],
        compiler_params=_params(("arbitrary",)),
        name="dispatch",
    )(dest, h2)


def _expert_kernel(vblock_ref, vexpert_ref, vlo_ref, x_ref, wg_ref, wu_ref, wd_ref, o_ref, wgu_s, wd_s):
    v = pl.program_id(0)
    e = vexpert_ref[v]

    @pl.when((v == 0) | (e != vexpert_ref[jnp.maximum(v - 1, 0)]))
    def _():
        wgu_s[:, :D_EXPERT] = wg_ref[0].astype(BF16)
        wgu_s[:, D_EXPERT:] = wu_ref[0].astype(BF16)
        wd_s[...] = wd_ref[0].astype(BF16)

    lo = vlo_ref[v]

    @pl.when(lo < MOE_ROWS)
    def _():
        gu = jnp.dot(x_ref[...].astype(BF16), wgu_s[...], preferred_element_type=F32)
        g = gu[:, :D_EXPERT]
        act = (g * jax.nn.sigmoid(g)) * gu[:, D_EXPERT:]
        res = jnp.dot(act.astype(BF16), wd_s[...], preferred_element_type=F32)

        @pl.when(lo == 0)
        def _():
            o_ref[...] = res

        @pl.when(lo > 0)
        def _():
            rows = lax.broadcasted_iota(jnp.int32, (MOE_ROWS, 1), 0)
            o_ref[...] = jnp.where(rows >= lo, res, o_ref[...])


def _expert_call(vblock, vexpert, vlo, x_slots, w_gate, w_up, w_down):
    grid_spec = pltpu.PrefetchScalarGridSpec(
        num_scalar_prefetch=3,
        grid=(N_VISITS,),
        in_specs=[pl.BlockSpec((MOE_ROWS, D_MODEL), lambda v, vb, ve, vl: (vb[v], 0)),
                  pl.BlockSpec((1, D_MODEL, D_EXPERT), lambda v, vb, ve, vl: (ve[v], 0, 0)),
                  pl.BlockSpec((1, D_MODEL, D_EXPERT), lambda v, vb, ve, vl: (ve[v], 0, 0)),
                  pl.BlockSpec((1, D_EXPERT, D_MODEL), lambda v, vb, ve, vl: (ve[v], 0, 0))],
        out_specs=pl.BlockSpec((MOE_ROWS, D_MODEL), lambda v, vb, ve, vl: (vb[v], 0)),
        scratch_shapes=[pltpu.VMEM((D_MODEL, 2 * D_EXPERT), BF16),
                        pltpu.VMEM((D_EXPERT, D_MODEL), BF16)],
    )
    return pl.pallas_call(
        _expert_kernel,
        grid_spec=grid_spec,
        out_shape=jax.ShapeDtypeStruct((N_ASSIGN, D_MODEL), F32),
        compiler_params=_params(("arbitrary",)),
        name="experts",
    )(vblock, vexpert, vlo, x_slots, w_gate, w_up, w_down)


def _final_kernel(dest_ref, x1_ref, mod_ref, gate_ref, g_ref, y_ref, o_ref, rows_ref, sem):
    tile = TILE_MIX

    def row(r, carry):
        for k in range(TOPK):
            d = dest_ref[0, 0, k * tile + r]
            pltpu.make_async_copy(y_ref.at[pl.ds(d, 1)], rows_ref.at[k, pl.ds(r, 1)], sem).start()
        return carry

    lax.fori_loop(0, tile, row, 0, unroll=8)
    for k in range(TOPK):
        pltpu.make_async_copy(y_ref.at[pl.ds(0, tile)], rows_ref.at[k], sem).wait()
    gate = gate_ref[0].T
    moe = gate[:, 0:1] * rows_ref[0] + gate[:, 1:2] * rows_ref[1]
    x2 = x1_ref[0] + mod_ref[0, 5:6, :] * moe
    r = lax.rsqrt(jnp.mean(x2 * x2, axis=-1, keepdims=True) + EPS)
    o_ref[0] = x2 * r * g_ref[...]


def _final_call(x1, mod, y_slots, dest, gates, final_g, first_seq, n_seq):
    tile = TILE_MIX
    tile_of = lambda b, i: (b + first_seq) * TILES_PER_SEQ + i
    return pl.pallas_call(
        _final_kernel,
        grid=(n_seq, TILES_PER_SEQ),
        in_specs=[_dest_spec(lambda b, i: (tile_of(b, i), 0, 0)),
                  pl.BlockSpec((1, tile, D_MODEL), lambda b, i: (b + first_seq, i, 0)),
                  pl.BlockSpec((1, N_MOD, D_MODEL), lambda b, i: (b + first_seq, 0, 0)),
                  pl.BlockSpec((1, ROUTE_ROWS, tile), lambda b, i: (tile_of(b, i), 0, 0)),
                  _resident((1, D_MODEL)),
                  pl.BlockSpec(memory_space=pl.ANY)],
        out_specs=pl.BlockSpec((1, tile, D_MODEL), lambda b, i: (b, i, 0)),
        out_shape=jax.ShapeDtypeStruct((n_seq, SEQ, D_MODEL), F32),
        scratch_shapes=[pltpu.VMEM((TOPK, tile, D_MODEL), F32), pltpu.SemaphoreType.DMA],
        compiler_params=_params(("arbitrary", "arbitrary")),
        name="final",
    )(dest, x1, mod, gates, final_g, y_slots)


def kernel(x_prompt, x_sample, c_prompt, c_sample, w_ada, b_ada, norm1_g, w_in, w_pool, pool_scale,
           ssm_a_re_f, ssm_a_im_f, ssm_log_dt_f, ssm_b_re_f, ssm_b_im_f, ssm_c_re_f, ssm_c_im_f,
           ssm_a_re_b, ssm_a_im_b, ssm_log_dt_b, ssm_b_re_b, ssm_b_im_b, ssm_c_re_b, ssm_c_im_b,
           ssm_d, w_glu, b_glu, w_proj_a, w_proj_b, w_out, norm2_g,
           w_grp, b_grp, w_router, b_router, w_exp_gate, w_exp_up, w_exp_down, final_g):
    n_u = POOL_WIDTH + SSM_WIDTH
    c_pad = jnp.concatenate([c_prompt, c_sample, jnp.zeros((16 - N_SEQ, D_MODEL), F32)], axis=0)
    mod = _mod_call(c_pad, w_ada[0], b_ada).reshape(16, N_MOD, D_MODEL)

    w_in_bf = w_in[0].astype(BF16)
    u_a, u_b = _inproj_call(x_prompt, x_sample, mod, norm1_g, w_in_bf[:, :n_u])

    fwd = (ssm_a_re_f[0], ssm_a_im_f[0], ssm_log_dt_f[0], ssm_b_re_f[0], ssm_b_im_f[0], ssm_c_re_f[0], ssm_c_im_f[0])
    bwd = (ssm_a_re_b[0], ssm_a_im_b[0], ssm_log_dt_b[0], ssm_b_re_b[0], ssm_b_im_b[0], ssm_c_re_b[0], ssm_c_im_b[0])
    y_s5 = _s5_call(u_b, *_s5_operators(fwd, bwd, ssm_d[0]))

    w_r = jnp.concatenate([w_router[0], w_grp[0],
                           jnp.zeros((D_MODEL, ROUTER_COLS - N_GROUPS - N_EXPERTS), F32)], axis=1)
    b_r = jnp.concatenate([b_router[0], b_grp[0],
                           jnp.zeros((ROUTER_COLS - N_GROUPS - N_EXPERTS,), F32)])[None, :]
    x1, h2, route, gates, counts = _mix_call(
        x_prompt, x_sample, mod, norm1_g, norm2_g, u_a, y_s5, w_in_bf[:, n_u:], w_pool[0].astype(BF16),
        pool_scale, w_proj_a[0].astype(BF16), w_glu[0].astype(BF16), b_glu, w_proj_b[0].astype(BF16),
        w_out[0].astype(BF16), w_r, b_r)

    dest8, vblock, vexpert, vlo = _plan_call(counts[:, 0].astype(jnp.int32), route, counts)
    dest = dest8[:, :TOPK, :].reshape(N_TILES, 1, TOPK * TILE_MIX)
    x_slots = _dispatch_call(dest, h2.reshape(N_TOK, D_MODEL))
    y_slots = _expert_call(vblock, vexpert, vlo, x_slots, w_exp_gate[0], w_exp_up[0], w_exp_down[0])

    final_g2 = final_g[None, :]
    y_prompt = _final_call(x1, mod, y_slots, dest, gates, final_g2, 0, N_PROMPT)
    y_sample = _final_call(x1, mod, y_slots, dest, gates, final_g2, N_PROMPT, N_SAMPLE)
    return (y_prompt, y_sample)
```

```python
import math

import jax
import jax.numpy as jnp
from jax import lax
from jax.experimental import pallas as pl
from jax.experimental.pallas import tpu as pltpu

F32 = jnp.float32
BF16 = jnp.bfloat16

D_MODEL = 2048
SEQ = 4096
N_PROMPT = 2
N_SAMPLE = 8
N_SEQ = N_PROMPT + N_SAMPLE
N_TOK = N_SEQ * SEQ
EPS = 1e-6
N_MOD = 6

POOL_WINDOWS = (2, 4, 8, 16)
POOL_GROUP = 256
POOL_WIDTH = 1024
POOL_HALO = 16

SSM_GROUPS = 32
SSM_P = 16
SSM_N = 64
SSM_WIDTH = 512
SSM_CHUNK = 16
SSM_NCHUNK = SEQ // SSM_CHUNK
SSM_COLS = SSM_CHUNK * SSM_P
SSM_STATE_ROWS = 4 * SSM_N
SSM_GROUPS_PER_STEP = 8
SSM_SCAN_STEPS = 8

N_GROUPS = 4
EXPERTS_PER_GROUP = 8
N_EXPERTS = 32
TOPK = 2
D_EXPERT = 512
N_ASSIGN = N_TOK * TOPK
ROUTER_COLS = 128

TILE_INPROJ = 512
TILE_MIX = 256
N_TILES = N_TOK // TILE_MIX
TILES_PER_SEQ = SEQ // TILE_MIX
ROUTE_ROWS = 8
MOE_ROWS = 256
MOE_SHIFT = 8
MOE_BLOCKS = N_ASSIGN // MOE_ROWS
N_VISITS = MOE_BLOCKS + N_EXPERTS - 1

VMEM_LIMIT = 60 * 1024 * 1024


def _params(sem, vmem=VMEM_LIMIT):
    return pltpu.CompilerParams(dimension_semantics=sem, vmem_limit_bytes=vmem)


def _resident(shape):
    zeros = (0,) * len(shape)
    return pl.BlockSpec(shape, lambda *_: zeros, pipeline_mode=pl.Buffered(1))


def _ada_norm(x, gain, scale, shift):
    r = lax.rsqrt(jnp.mean(x * x, axis=-1, keepdims=True) + EPS)
    return (x * r * gain) * (1.0 + scale) + shift


def _regroup_matrix(n_outer, n_inner):
    n = n_outer * n_inner
    dst = lax.broadcasted_iota(jnp.int32, (n, n), 0)
    src = lax.broadcasted_iota(jnp.int32, (n, n), 1)
    shift = n_outer.bit_length() - 1
    return (src == (dst & (n_outer - 1)) * n_inner + lax.shift_right_logical(dst, shift)).astype(BF16)


def _mod_kernel(c_ref, w_ref, b_ref, o_ref):
    c = c_ref[...]
    s = c * jax.nn.sigmoid(c)
    o_ref[...] = jnp.dot(s.astype(BF16), w_ref[...].astype(BF16), preferred_element_type=F32) + b_ref[...]


def _mod_call(c_pad, w_ada, b_ada):
    n = w_ada.shape[1]
    tn = 1024
    return pl.pallas_call(
        _mod_kernel,
        grid=(n // tn,),
        in_specs=[pl.BlockSpec(c_pad.shape, lambda j: (0, 0)),
                  pl.BlockSpec((D_MODEL, tn), lambda j: (0, j)),
                  pl.BlockSpec((1, tn), lambda j: (0, j))],
        out_specs=pl.BlockSpec((c_pad.shape[0], tn), lambda j: (0, j)),
        out_shape=jax.ShapeDtypeStruct((c_pad.shape[0], n), F32),
        compiler_params=_params(("arbitrary",)),
        name="mod",
    )(c_pad, w_ada, b_ada)


def _x_specs(tile):
    last = SEQ // tile - 1
    xp = pl.BlockSpec((1, tile, D_MODEL),
                      lambda b, i: (jnp.minimum(b, N_PROMPT - 1), jnp.where(b < N_PROMPT, i, last), 0))
    xs = pl.BlockSpec((1, tile, D_MODEL),
                      lambda b, i: (jnp.maximum(b - N_PROMPT, 0), jnp.where(b < N_PROMPT, 0, i), 0))
    return xp, xs


def _load_x(xp_ref, xs_ref):
    return jnp.where(pl.program_id(0) < N_PROMPT, xp_ref[0], xs_ref[0])


def _inproj_kernel(xp_ref, xs_ref, mod_ref, g_ref, w_ref, ua_ref, ub_ref):
    x = _load_x(xp_ref, xs_ref)
    h = _ada_norm(x, g_ref[...], mod_ref[0, 1:2, :], mod_ref[0, 0:1, :])
    p = jnp.dot(h.astype(BF16), w_ref[...], preferred_element_type=F32)
    ua_ref[0] = p[:, :POOL_WIDTH].astype(BF16)
    n_chunk = TILE_INPROJ // SSM_CHUNK
    ub = jnp.dot(_regroup_matrix(n_chunk, SSM_CHUNK), p[:, POOL_WIDTH:].astype(BF16), preferred_element_type=F32)
    ub_ref[...] = ub.reshape(SSM_CHUNK, n_chunk, SSM_WIDTH).astype(BF16)


def _inproj_call(x_prompt, x_sample, mod, norm_g, w_u):
    tile = TILE_INPROJ
    xp, xs = _x_specs(tile)
    return pl.pallas_call(
        _inproj_kernel,
        grid=(N_SEQ, SEQ // tile),
        in_specs=[xp, xs,
                  pl.BlockSpec((1, N_MOD, D_MODEL), lambda b, i: (b, 0, 0)),
                  _resident((1, D_MODEL)),
                  _resident(w_u.shape)],
        out_specs=[pl.BlockSpec((1, tile, POOL_WIDTH), lambda b, i: (b, i, 0)),
                   pl.BlockSpec((None, SSM_CHUNK, tile // SSM_CHUNK, SSM_WIDTH), lambda b, i: (b, 0, i, 0))],
        out_shape=[jax.ShapeDtypeStruct((N_SEQ, SEQ, POOL_WIDTH), BF16),
                   jax.ShapeDtypeStruct((N_SEQ, SSM_CHUNK, SSM_NCHUNK, SSM_WIDTH), BF16)],
        compiler_params=_params(("arbitrary", "arbitrary")),
        name="inproj",
    )(x_prompt, x_sample, mod, norm_g, w_u)


def _s5_kernel(u_ref, wst_ref, tt_ref, wot_ref, a_ref, y_ref, ut_ref, yt_ref, apow_ref):
    n, nc, ng = SSM_N, SSM_NCHUNK, SSM_GROUPS_PER_STEP

    @pl.when(pl.program_id(1) == 0)
    def _():
        for g in range(ng):
            a = a_ref[g]
            for k in range(SSM_SCAN_STEPS):
                apow_ref[g, k] = a
                f_re, f_im, b_re, b_im = a[0:n], a[n:2 * n], a[2 * n:3 * n], a[3 * n:4 * n]
                a = jnp.concatenate([f_re * f_re - f_im * f_im, 2.0 * (f_re * f_im),
                                     b_re * b_re - b_im * b_im, 2.0 * (b_re * b_im)], axis=0)

    for s in range(SSM_CHUNK):
        blk = u_ref[s].astype(F32).T
        ut_ref[:, pl.ds(s * SSM_P, SSM_P), :] = blk.reshape(ng, SSM_P, nc).astype(BF16)

    lane = lax.broadcasted_iota(jnp.int32, (n, nc), 1)
    wide = lambda v: jnp.concatenate([v, v], axis=1)

    def shifted(x, d, forward):
        if forward:
            return jnp.where(lane >= d, pltpu.roll(x, d, 1), 0.0)
        return jnp.where(lane < nc - d, pltpu.roll(x, nc - d, 1), 0.0)

    def scan(x_re, x_im, g, re_row, im_row, forward):
        for k in range(SSM_SCAN_STEPS):
            a_re = wide(apow_ref[g, k, pl.ds(re_row, n), :])
            a_im = wide(apow_ref[g, k, pl.ds(im_row, n), :])
            s_re = shifted(x_re, 2 ** k, forward)
            s_im = shifted(x_im, 2 ** k, forward)
            x_re, x_im = x_re + (a_re * s_re - a_im * s_im), x_im + (a_re * s_im + a_im * s_re)
        return x_re, x_im

    def per_group(g, carry):
        ut = ut_ref[g]
        st = jnp.dot(wst_ref[g], ut, preferred_element_type=F32)
        f_re, f_im = scan(st[0:n], st[n:2 * n], g, 0, n, True)
        b_re, b_im = scan(st[2 * n:3 * n], st[3 * n:4 * n], g, 2 * n, 3 * n, False)
        carried = jnp.concatenate([shifted(f_re, 1, True), shifted(f_im, 1, True),
                                   shifted(b_re, 1, False), shifted(b_im, 1, False)], axis=0)
        yt_ref[g] = (jnp.dot(tt_ref[g], ut, preferred_element_type=F32)
                     + jnp.dot(wot_ref[g], carried.astype(BF16), preferred_element_type=F32))
        return carry

    lax.fori_loop(0, ng, per_group, 0)
    for t in range(SSM_CHUNK):
        y_ref[t] = yt_ref[:, pl.ds(t * SSM_P, SSM_P), :].reshape(ng * SSM_P, nc).T


def _s5_call(u_ph, wst, tt, wot, a16):
    ng = SSM_GROUPS_PER_STEP
    seq = pl.BlockSpec((None, SSM_CHUNK, SSM_NCHUNK, ng * SSM_P), lambda q, b: (b, 0, 0, q))
    mat = pl.BlockSpec((ng, SSM_COLS, SSM_COLS), lambda q, b: (q, 0, 0))
    return pl.pallas_call(
        _s5_kernel,
        grid=(SSM_GROUPS // ng, N_SEQ),
        in_specs=[seq, mat, mat, mat, pl.BlockSpec((ng, SSM_STATE_ROWS, 128), lambda q, b: (q, 0, 0))],
        out_specs=seq,
        out_shape=jax.ShapeDtypeStruct((N_SEQ, SSM_CHUNK, SSM_NCHUNK, SSM_WIDTH), F32),
        scratch_shapes=[pltpu.VMEM((ng, SSM_COLS, SSM_NCHUNK), BF16),
                        pltpu.VMEM((ng, SSM_COLS, SSM_NCHUNK), F32),
                        pltpu.VMEM((ng, SSM_SCAN_STEPS, SSM_STATE_ROWS, 128), F32)],
        compiler_params=_params(("arbitrary", "arbitrary")),
        name="s5",
    )(u_ph, wst, tt, wot, a16)


def _s5_direction(a_re, a_im, log_dt, b_re, b_im, c_re, c_im):
    dt = jnp.exp(log_dt)[:, None]
    k = jnp.arange(SSM_CHUNK + 1, dtype=F32)[None, :, None]
    mag = jnp.exp(k * (a_re * dt)[:, None, :])
    ang = k * (a_im * dt)[:, None, :]
    pw_re = mag * jnp.cos(ang)
    pw_im = mag * jnp.sin(ang)
    ab_re, ab_im = pw_re[:, 1], pw_im[:, 1]
    den = a_re * a_re + a_im * a_im
    q_re = ((ab_re - 1.0) * a_re + ab_im * a_im) / den
    q_im = (ab_im * a_re - (ab_re - 1.0) * a_im) / den
    bb_re = q_re[:, :, None] * b_re - q_im[:, :, None] * b_im
    bb_im = q_re[:, :, None] * b_im + q_im[:, :, None] * b_re
    cp_re = c_re[:, None] * pw_re[:, :, None, :] - c_im[:, None] * pw_im[:, :, None, :]
    cp_im = c_re[:, None] * pw_im[:, :, None, :] + c_im[:, None] * pw_re[:, :, None, :]
    contract = lambda c, b: jnp.einsum('gkqn,gnp->gkqp', c[:, :SSM_CHUNK], b, precision=lax.Precision.HIGHEST)
    taps = contract(cp_re, bb_re) - contract(cp_im, bb_im)
    return pw_re, pw_im, bb_re, bb_im, cp_re, cp_im, taps


def _s5_operators(fwd, bwd, ssm_d):
    g, q, p, n = SSM_GROUPS, SSM_CHUNK, SSM_P, SSM_N
    pf_re, pf_im, bf_re, bf_im, cf_re, cf_im, taps_f = _s5_direction(*fwd)
    pb_re, pb_im, bb_re, bb_im, cb_re, cb_im, taps_b = _s5_direction(*bwd)
    s_idx = jnp.arange(q)
    lag = s_idx[None, :] - s_idx[:, None]
    tf = jnp.where((lag >= 0)[None, :, :, None, None], taps_f[:, jnp.clip(lag, 0, q - 1)], 0.0)
    tb = jnp.where((lag <= 0)[None, :, :, None, None], taps_b[:, jnp.clip(-lag, 0, q - 1)], 0.0)
    skip = (jnp.eye(q, dtype=F32)[None, :, :, None, None] * jnp.eye(p, dtype=F32)[None, None, None]
            * ssm_d.reshape(g, 1, 1, 1, p))
    tt = (tf + tb + skip).transpose(0, 2, 3, 1, 4).reshape(g, q * p, q * p)

    def state_in(pw_re, pw_im, b_re, b_im, powers):
        w_re = pw_re[:, powers][:, :, None, :] * b_re.transpose(0, 2, 1)[:, None] \
            - pw_im[:, powers][:, :, None, :] * b_im.transpose(0, 2, 1)[:, None]
        w_im = pw_re[:, powers][:, :, None, :] * b_im.transpose(0, 2, 1)[:, None] \
            + pw_im[:, powers][:, :, None, :] * b_re.transpose(0, 2, 1)[:, None]
        return w_re.reshape(g, q * p, n), w_im.reshape(g, q * p, n)

    wsf_re, wsf_im = state_in(pf_re, pf_im, bf_re, bf_im, q - 1 - s_idx)
    wsb_re, wsb_im = state_in(pb_re, pb_im, bb_re, bb_im, s_idx)
    wst = jnp.concatenate([wsf_re, wsf_im, wsb_re, wsb_im], axis=2).transpose(0, 2, 1)

    def state_out(cp_re, cp_im, powers):
        o_re = cp_re[:, powers].reshape(g, q * p, n)
        o_im = -cp_im[:, powers].reshape(g, q * p, n)
        return o_re, o_im

    of_re, of_im = state_out(cf_re, cf_im, s_idx + 1)
    ob_re, ob_im = state_out(cb_re, cb_im, q - s_idx)
    wot = jnp.concatenate([of_re, of_im, ob_re, ob_im], axis=2)
    a16 = jnp.concatenate([pf_re[:, q], pf_im[:, q], pb_re[:, q], pb_im[:, q]], axis=1)
    a16 = jnp.broadcast_to(a16[:, :, None], (g, SSM_STATE_ROWS, 128))
    return wst.astype(BF16), tt.astype(BF16), wot.astype(BF16), a16


def _gelu_tanh(x):
    return 0.5 * x * (1.0 + jnp.tanh(math.sqrt(2.0 / math.pi) * (x + 0.044715 * (x * x * x))))


def _route_tile(logits, run_ref, route_ref, gate_ref, counts_ref):
    tile = logits.shape[0]
    neg = -jnp.inf
    lt = logits.T
    row8 = lax.broadcasted_iota(jnp.int32, (EXPERTS_PER_GROUP, tile), 0)
    gl = jnp.where(row8 < N_GROUPS, lt[N_EXPERTS:N_EXPERTS + 8], neg)
    gmax = jnp.max(gl, axis=0, keepdims=True)
    g_sel = jnp.min(jnp.where(gl == gmax, row8, 8), axis=0, keepdims=True)
    g_p = 1.0 / jnp.sum(jnp.exp(gl - gmax), axis=0, keepdims=True)
    in_grp = lt[0:EXPERTS_PER_GROUP]
    for g in range(1, N_GROUPS):
        in_grp = jnp.where(g_sel == g, lt[g * EXPERTS_PER_GROUP:(g + 1) * EXPERTS_PER_GROUP], in_grp)
    m1 = jnp.max(in_grp, axis=0, keepdims=True)
    i1 = jnp.min(jnp.where(in_grp == m1, row8, 8), axis=0, keepdims=True)
    rest = jnp.where(row8 == i1, neg, in_grp)
    m2 = jnp.max(rest, axis=0, keepdims=True)
    i2 = jnp.min(jnp.where(rest == m2, row8, 8), axis=0, keepdims=True)
    e21 = jnp.exp(m2 - m1)
    p1 = 1.0 / (1.0 + e21)
    eid1 = g_sel * EXPERTS_PER_GROUP + i1
    eid2 = g_sel * EXPERTS_PER_GROUP + i2

    row_e = lax.broadcasted_iota(jnp.int32, (N_EXPERTS, tile), 0)
    oh1 = (row_e == eid1).astype(F32)
    oh2 = (row_e == eid2).astype(F32)
    earlier = (lax.broadcasted_iota(jnp.int32, (tile, tile), 0)
               < lax.broadcasted_iota(jnp.int32, (tile, tile), 1)).astype(BF16)
    before1 = jnp.dot(oh1.astype(BF16), earlier, preferred_element_type=F32)
    before2 = jnp.dot(oh2.astype(BF16), earlier, preferred_element_type=F32)
    tot1 = jnp.sum(oh1, axis=1, keepdims=True)
    tot2 = jnp.sum(oh2, axis=1, keepdims=True)
    run = run_ref[:, 0:1]
    rank1 = jnp.sum(oh1 * (before1 + run), axis=0, keepdims=True)
    rank2 = jnp.sum(oh2 * (before2 + (run + tot1)), axis=0, keepdims=True)
    new_run = jnp.broadcast_to(run + tot1 + tot2, run_ref.shape)
    run_ref[...] = new_run
    counts_ref[...] = new_run
    zi = jnp.zeros((ROUTE_ROWS - 4, tile), jnp.int32)
    route_ref[0] = jnp.concatenate([eid1, eid2, rank1.astype(jnp.int32), rank2.astype(jnp.int32), zi], axis=0)
    zf = jnp.zeros((ROUTE_ROWS - 2, tile), F32)
    gate_ref[0] = jnp.concatenate([g_p * p1, g_p * (e21 * p1), zf], axis=0)


def _mix_kernel(xp_ref, xs_ref, mod_ref, g1_ref, g2_ref, ua_ref, ua_prev_ref, ua_next_ref, ys_ref,
                wg_ref, wpool_ref, pscale_ref, wpa_ref, wglu_ref, bglu_ref, wpb_ref, wout_ref,
                wr_ref, br_ref,
                x1_ref, h2_ref, route_ref, gate_ref, counts_ref, ext_ref, diff_ref, merged_ref, run_ref):
    tile = TILE_MIX
    i = pl.program_id(1)

    @pl.when((pl.program_id(0) == 0) & (i == 0))
    def _():
        run_ref[...] = jnp.zeros_like(run_ref)

    x = _load_x(xp_ref, xs_ref)
    h = _ada_norm(x, g1_ref[...], mod_ref[0, 1:2, :], mod_ref[0, 0:1, :]).astype(BF16)

    first = i == 0
    last = i == pl.num_programs(1) - 1
    ext_ref[pl.ds(0, POOL_HALO), :] = jnp.where(first, 0.0, ua_prev_ref[0].astype(F32))
    ext_ref[pl.ds(POOL_HALO, tile), :] = ua_ref[0].astype(F32)
    ext_ref[pl.ds(POOL_HALO + tile, POOL_HALO), :] = jnp.where(last, 0.0, ua_next_ref[0].astype(F32))
    pos = i * tile + lax.broadcasted_iota(jnp.int32, (tile, 1), 0)
    for k, w in enumerate(POOL_WINDOWS):
        cols = pl.ds(k * POOL_GROUP, POOL_GROUP)
        lo = jnp.maximum(pos - w // 2, 0)
        hi = jnp.minimum(pos + (w - 1 - w // 2), SEQ - 1)
        inv_cnt = 1.0 / (hi - lo + 1).astype(F32)
        acc = ext_ref[pl.ds(POOL_HALO - w // 2, tile), cols]
        for j in range(1, w):
            acc = acc + ext_ref[pl.ds(POOL_HALO - w // 2 + j, tile), cols]
        diff = acc * inv_cnt - ext_ref[pl.ds(POOL_HALO, tile), cols]
        mixed = jnp.dot(diff.astype(BF16), wpool_ref[k], preferred_element_type=F32)
        diff_ref[:, cols] = (mixed * pscale_ref[:, cols]).astype(BF16)

    z = _gelu_tanh(ys_ref[...].reshape(tile, SSM_WIDTH))
    zg = z * jax.nn.sigmoid(jnp.dot(z.astype(BF16), wglu_ref[...], preferred_element_type=F32) + bglu_ref[...])
    zg = jnp.dot(_regroup_matrix(SSM_CHUNK, tile // SSM_CHUNK), zg.astype(BF16),
                 preferred_element_type=F32).astype(BF16)
    pa = diff_ref[...]

    chunk = 1024
    for j in range(D_MODEL // chunk):
        c0 = j * chunk
        g_a = jnp.dot(h, wg_ref[:, pl.ds(c0, chunk)], preferred_element_type=F32)
        y_a = jnp.dot(pa, wpa_ref[:, pl.ds(c0, chunk)], preferred_element_type=F32)
        m = jax.nn.sigmoid(g_a) * y_a
        g_b = jnp.dot(h, wg_ref[:, pl.ds(D_MODEL + c0, chunk)], preferred_element_type=F32)
        y_b = jnp.dot(zg, wpb_ref[:, pl.ds(c0, chunk)], preferred_element_type=F32)
        m = m + jax.nn.sigmoid(g_b) * y_b
        merged_ref[:, pl.ds(c0, chunk)] = m.astype(BF16)

    x1 = x + mod_ref[0, 2:3, :] * jnp.dot(merged_ref[...], wout_ref[...], preferred_element_type=F32)
    x1_ref[0] = x1
    h2 = _ada_norm(x1, g2_ref[...], mod_ref[0, 4:5, :], mod_ref[0, 3:4, :])
    h2_ref[0] = h2
    h2_hi = h2.astype(BF16)
    h2_lo = (h2 - h2_hi.astype(F32)).astype(BF16)
    logits = (jnp.dot(h2_hi, wr_ref[0], preferred_element_type=F32)
              + (jnp.dot(h2_lo, wr_ref[0], preferred_element_type=F32)
                 + jnp.dot(h2_hi, wr_ref[1], preferred_element_type=F32))) + br_ref[...]
    _route_tile(logits, run_ref, route_ref, gate_ref, counts_ref)


def _mix_call(x_prompt, x_sample, mod, norm1_g, norm2_g, u_a, y_s5, w_g, w_pool, pool_scale, w_pa,
              w_glu, b_glu, w_pb, w_out, w_r, b_r):
    tile = TILE_MIX
    xp, xs = _x_specs(tile)
    halo_per_tile = tile // POOL_HALO
    n_halo = SEQ // POOL_HALO
    seq_tile = lambda width: pl.BlockSpec((1, tile, width), lambda b, i: (b, i, 0))
    return pl.pallas_call(
        _mix_kernel,
        grid=(N_SEQ, SEQ // tile),
        in_specs=[xp, xs,
                  pl.BlockSpec((1, N_MOD, D_MODEL), lambda b, i: (b, 0, 0)),
                  _resident((1, D_MODEL)), _resident((1, D_MODEL)),
                  seq_tile(POOL_WIDTH),
                  pl.BlockSpec((1, POOL_HALO, POOL_WIDTH),
                               lambda b, i: (b, jnp.maximum(i * halo_per_tile - 1, 0), 0)),
                  pl.BlockSpec((1, POOL_HALO, POOL_WIDTH),
                               lambda b, i: (b, jnp.minimum((i + 1) * halo_per_tile, n_halo - 1), 0)),
                  pl.BlockSpec((None, SSM_CHUNK, tile // SSM_CHUNK, SSM_WIDTH), lambda b, i: (b, 0, i, 0)),
                  _resident(w_g.shape), _resident(w_pool.shape), _resident(pool_scale.shape),
                  _resident(w_pa.shape), _resident(w_glu.shape), _resident(b_glu.shape),
                  _resident(w_pb.shape), _resident(w_out.shape), _resident(w_r.shape), _resident(b_r.shape)],
        out_specs=[seq_tile(D_MODEL), seq_tile(D_MODEL),
                   pl.BlockSpec((1, ROUTE_ROWS, tile), lambda b, i: (b * TILES_PER_SEQ + i, 0, 0)),
                   pl.BlockSpec((1, ROUTE_ROWS, tile), lambda b, i: (b * TILES_PER_SEQ + i, 0, 0)),
                   pl.BlockSpec((N_EXPERTS, 128), lambda b, i: (0, 0))],
        out_shape=[jax.ShapeDtypeStruct((N_SEQ, SEQ, D_MODEL), F32),
                   jax.ShapeDtypeStruct((N_SEQ, SEQ, D_MODEL), F32),
                   jax.ShapeDtypeStruct((N_TILES, ROUTE_ROWS, tile), jnp.int32),
                   jax.ShapeDtypeStruct((N_TILES, ROUTE_ROWS, tile), F32),
                   jax.ShapeDtypeStruct((N_EXPERTS, 128), F32)],
        scratch_shapes=[pltpu.VMEM((tile + 2 * POOL_HALO, POOL_WIDTH), F32),
                        pltpu.VMEM((tile, POOL_WIDTH), BF16),
                        pltpu.VMEM((tile, D_MODEL), BF16),
                        pltpu.VMEM((N_EXPERTS, 128), F32)],
        compiler_params=_params(("arbitrary", "arbitrary")),
        name="mix",
    )(x_prompt, x_sample, mod, norm1_g, norm2_g, u_a, u_a, u_a, y_s5, w_g, w_pool, pool_scale, w_pa,
      w_glu, b_glu, w_pb, w_out, w_r, b_r)


def _plan_kernel(cnt_ref, route_ref, counts_ref, dest_ref, vblock_ref, vexpert_ref, vlo_ref):
    below = (lax.broadcasted_iota(jnp.int32, (N_EXPERTS, N_EXPERTS), 1)
             < lax.broadcasted_iota(jnp.int32, (N_EXPERTS, N_EXPERTS), 0)).astype(F32)
    starts = jnp.dot(below, counts_ref[...], preferred_element_type=F32, precision=lax.Precision.HIGHEST)
    starts_b = jnp.broadcast_to(starts[:, 0:1], (N_EXPERTS, TILE_MIX))
    row_e = lax.broadcasted_iota(jnp.int32, (N_EXPERTS, TILE_MIX), 0)
    zi = jnp.zeros((ROUTE_ROWS - 2, TILE_MIX), jnp.int32)

    def per_tile(t, carry):
        r = route_ref[t]
        s1 = jnp.sum(jnp.where(row_e == r[0:1], starts_b, 0.0), axis=0, keepdims=True)
        s2 = jnp.sum(jnp.where(row_e == r[1:2], starts_b, 0.0), axis=0, keepdims=True)
        dest_ref[t] = jnp.concatenate([s1.astype(jnp.int32) + r[2:3], s2.astype(jnp.int32) + r[3:4], zi], axis=0)
        return carry

    lax.fori_loop(0, N_TILES, per_tile, 0)

    def per_expert(e, carry):
        v, start, last_e = carry
        cnt = cnt_ref[e]
        end = start + cnt
        first = lax.shift_right_logical(start, MOE_SHIFT)
        n_blk = jnp.where(cnt > 0, lax.shift_right_logical(end - 1, MOE_SHIFT) - first + 1, 0)

        def per_block(k, v):
            blk = first + k
            vblock_ref[v] = blk
            vexpert_ref[v] = e
            vlo_ref[v] = jnp.maximum(start - blk * MOE_ROWS, 0)
            return v + 1

        v = lax.fori_loop(0, n_blk, per_block, v)
        return v, end, jnp.where(cnt > 0, e, last_e)

    v, _, last_e = lax.fori_loop(0, N_EXPERTS, per_expert, (0, 0, 0))

    def idle(k, carry):
        vblock_ref[k] = MOE_BLOCKS - 1
        vexpert_ref[k] = last_e
        vlo_ref[k] = MOE_ROWS
        return carry

    lax.fori_loop(v, N_VISITS, idle, 0)


def _plan_call(cnt, route, counts):
    smem = pl.BlockSpec(memory_space=pltpu.SMEM)
    vmem = pl.BlockSpec(memory_space=pltpu.VMEM)
    visits = jax.ShapeDtypeStruct((N_VISITS,), jnp.int32)
    return pl.pallas_call(
        _plan_kernel,
        in_specs=[smem, vmem, vmem],
        out_specs=[vmem, smem, smem, smem],
        out_shape=[jax.ShapeDtypeStruct((N_TILES, ROUTE_ROWS, TILE_MIX), jnp.int32), visits, visits, visits],
        name="plan",
    )(cnt, route, counts)


def _dest_spec(index_map):
    return pl.BlockSpec((1, 1, TOPK * TILE_MIX), index_map, memory_space=pltpu.SMEM)


def _dispatch_kernel(dest_ref, h_ref, xs_ref, sem):
    tile = TILE_MIX

    def row(r, carry):
        for k in range(TOPK):
            d = dest_ref[0, 0, k * tile + r]
            pltpu.make_async_copy(h_ref.at[pl.ds(r, 1)], xs_ref.at[pl.ds(d, 1)], sem).start()
        return carry

    lax.fori_loop(0, tile, row, 0, unroll=8)
    for k in range(TOPK):
        pltpu.make_async_copy(h_ref, xs_ref.at[pl.ds(0, tile)], sem).wait()


def _dispatch_call(dest, h2):
    return pl.pallas_call(
        _dispatch_kernel,
        grid=(N_TILES,),
        in_specs=[_dest_spec(lambda t: (t, 0, 0)),
                  pl.BlockSpec((TILE_MIX, D_MODEL), lambda t: (t, 0))],
        out_specs=pl.BlockSpec(memory_space=pl.ANY),
        out_shape=jax.ShapeDtypeStruct((N_ASSIGN, D_MODEL), F32),
        scratch_shapes=[pltpu.SemaphoreType.DMA],
        compiler_params=_params(("arbitrary",)),
        name="dispatch",
    )(dest, h2)


def _expert_kernel(vblock_ref, vexpert_ref, vlo_ref, x_ref, wg_ref, wu_ref, wd_ref, o_ref, wgu_s, wd_s):
    v = pl.program_id(0)
    e = vexpert_ref[v]

    @pl.when((v == 0) | (e != vexpert_ref[jnp.maximum(v - 1, 0)]))
    def _():
        wgu_s[:, :D_EXPERT] = wg_ref[0].astype(BF16)
        wgu_s[:, D_EXPERT:] = wu_ref[0].astype(BF16)
        wd_s[...] = wd_ref[0].astype(BF16)

    lo = vlo_ref[v]

    @pl.when(lo < MOE_ROWS)
    def _():
        gu = jnp.dot(x_ref[...].astype(BF16), wgu_s[...], preferred_element_type=F32)
        g = gu[:, :D_EXPERT]
        act = (g * jax.nn.sigmoid(g)) * gu[:, D_EXPERT:]
        res = jnp.dot(act.astype(BF16), wd_s[...], preferred_element_type=F32)

        @pl.when(lo == 0)
        def _():
            o_ref[...] = res

        @pl.when(lo > 0)
        def _():
            rows = lax.broadcasted_iota(jnp.int32, (MOE_ROWS, 1), 0)
            o_ref[...] = jnp.where(rows >= lo, res, o_ref[...])


def _expert_call(vblock, vexpert, vlo, x_slots, w_gate, w_up, w_down):
    grid_spec = pltpu.PrefetchScalarGridSpec(
        num_scalar_prefetch=3,
        grid=(N_VISITS,),
        in_specs=[pl.BlockSpec((MOE_ROWS, D_MODEL), lambda v, vb, ve, vl: (vb[v], 0)),
                  pl.BlockSpec((1, D_MODEL, D_EXPERT), lambda v, vb, ve, vl: (ve[v], 0, 0)),
                  pl.BlockSpec((1, D_MODEL, D_EXPERT), lambda v, vb, ve, vl: (ve[v], 0, 0)),
                  pl.BlockSpec((1, D_EXPERT, D_MODEL), lambda v, vb, ve, vl: (ve[v], 0, 0))],
        out_specs=pl.BlockSpec((MOE_ROWS, D_MODEL), lambda v, vb, ve, vl: (vb[v], 0)),
        scratch_shapes=[pltpu.VMEM((D_MODEL, 2 * D_EXPERT), BF16),
                        pltpu.VMEM((D_EXPERT, D_MODEL), BF16)],
    )
    return pl.pallas_call(
        _expert_kernel,
        grid_spec=grid_spec,
        out_shape=jax.ShapeDtypeStruct((N_ASSIGN, D_MODEL), F32),
        compiler_params=_params(("arbitrary",)),
        name="experts",
    )(vblock, vexpert, vlo, x_slots, w_gate, w_up, w_down)


def _final_kernel(dest_ref, x1_ref, mod_ref, gate_ref, g_ref, y_ref, o_ref, rows_ref, sem):
    tile = TILE_MIX

    def row(r, carry):
        for k in range(TOPK):
            d = dest_ref[0, 0, k * tile + r]
            pltpu.make_async_copy(y_ref.at[pl.ds(d, 1)], rows_ref.at[k, pl.ds(r, 1)], sem).start()
        return carry

    lax.fori_loop(0, tile, row, 0, unroll=8)
    for k in range(TOPK):
        pltpu.make_async_copy(y_ref.at[pl.ds(0, tile)], rows_ref.at[k], sem).wait()
    gate = gate_ref[0].T
    moe = gate[:, 0:1] * rows_ref[0] + gate[:, 1:2] * rows_ref[1]
    x2 = x1_ref[0] + mod_ref[0, 5:6, :] * moe
    r = lax.rsqrt(jnp.mean(x2 * x2, axis=-1, keepdims=True) + EPS)
    o_ref[0] = x2 * r * g_ref[...]


def _final_call(x1, mod, y_slots, dest, gates, final_g, first_seq, n_seq):
    tile = TILE_MIX
    tile_of = lambda b, i: (b + first_seq) * TILES_PER_SEQ + i
    return pl.pallas_call(
        _final_kernel,
        grid=(n_seq, TILES_PER_SEQ),
        in_specs=[_dest_spec(lambda b, i: (tile_of(b, i), 0, 0)),
                  pl.BlockSpec((1, tile, D_MODEL), lambda b, i: (b + first_seq, i, 0)),
                  pl.BlockSpec((1, N_MOD, D_MODEL), lambda b, i: (b + first_seq, 0, 0)),
                  pl.BlockSpec((1, ROUTE_ROWS, tile), lambda b, i: (tile_of(b, i), 0, 0)),
                  _resident((1, D_MODEL)),
                  pl.BlockSpec(memory_space=pl.ANY)],
        out_specs=pl.BlockSpec((1, tile, D_MODEL), lambda b, i: (b, i, 0)),
        out_shape=jax.ShapeDtypeStruct((n_seq, SEQ, D_MODEL), F32),
        scratch_shapes=[pltpu.VMEM((TOPK, tile, D_MODEL), F32), pltpu.SemaphoreType.DMA],
        compiler_params=_params(("arbitrary", "arbitrary")),
        name="final",
    )(dest, x1, mod, gates, final_g, y_slots)


def kernel(x_prompt, x_sample, c_prompt, c_sample, w_ada, b_ada, norm1_g, w_in, w_pool, pool_scale,
           ssm_a_re_f, ssm_a_im_f, ssm_log_dt_f, ssm_b_re_f, ssm_b_im_f, ssm_c_re_f, ssm_c_im_f,
           ssm_a_re_b, ssm_a_im_b, ssm_log_dt_b, ssm_b_re_b, ssm_b_im_b, ssm_c_re_b, ssm_c_im_b,
           ssm_d, w_glu, b_glu, w_proj_a, w_proj_b, w_out, norm2_g,
           w_grp, b_grp, w_router, b_router, w_exp_gate, w_exp_up, w_exp_down, final_g):
    n_u = POOL_WIDTH + SSM_WIDTH
    c_pad = jnp.concatenate([c_prompt, c_sample, jnp.zeros((16 - N_SEQ, D_MODEL), F32)], axis=0)
    mod = _mod_call(c_pad, w_ada[0], b_ada).reshape(16, N_MOD, D_MODEL)

    w_in_bf = w_in[0].astype(BF16)
    u_a, u_b = _inproj_call(x_prompt, x_sample, mod, norm1_g, w_in_bf[:, :n_u])

    fwd = (ssm_a_re_f[0], ssm_a_im_f[0], ssm_log_dt_f[0], ssm_b_re_f[0], ssm_b_im_f[0], ssm_c_re_f[0], ssm_c_im_f[0])
    bwd = (ssm_a_re_b[0], ssm_a_im_b[0], ssm_log_dt_b[0], ssm_b_re_b[0], ssm_b_im_b[0], ssm_c_re_b[0], ssm_c_im_b[0])
    y_s5 = _s5_call(u_b, *_s5_operators(fwd, bwd, ssm_d[0]))

    w_r = jnp.concatenate([w_router[0], w_grp[0],
                           jnp.zeros((D_MODEL, ROUTER_COLS - N_GROUPS - N_EXPERTS), F32)], axis=1)
    b_r = jnp.concatenate([b_router[0], b_grp[0],
                           jnp.zeros((ROUTER_COLS - N_GROUPS - N_EXPERTS,), F32)])[None, :]
    w_r_hi = w_r.astype(BF16)
    w_r = jnp.stack([w_r_hi, (w_r - w_r_hi.astype(F32)).astype(BF16)])
    x1, h2, route, gates, counts = _mix_call(
        x_prompt, x_sample, mod, norm1_g, norm2_g, u_a, y_s5, w_in_bf[:, n_u:], w_pool[0].astype(BF16),
        pool_scale, w_proj_a[0].astype(BF16), w_glu[0].astype(BF16), b_glu, w_proj_b[0].astype(BF16),
        w_out[0].astype(BF16), w_r, b_r)

    dest8, vblock, vexpert, vlo = _plan_call(counts[:, 0].astype(jnp.int32), route, counts)
    dest = dest8[:, :TOPK, :].reshape(N_TILES, 1, TOPK * TILE_MIX)
    x_slots = _dispatch_call(dest, h2.reshape(N_TOK, D_MODEL))
    y_slots = _expert_call(vblock, vexpert, vlo, x_slots, w_exp_gate[0], w_exp_up[0], w_exp_down[0])

    final_g2 = final_g[None, :]
    y_prompt = _final_call(x1, mod, y_slots, dest, gates, final_g2, 0, N_PROMPT)
    y_sample = _final_call(x1, mod, y_slots, dest, gates, final_g2, N_PROMPT, N_SAMPLE)
    return (y_prompt, y_sample)
```

```python
import math

import jax
import jax.numpy as jnp
from jax import lax
from jax.experimental import pallas as pl
from jax.experimental.pallas import tpu as pltpu

F32 = jnp.float32
BF16 = jnp.bfloat16

D_MODEL = 2048
SEQ = 4096
N_PROMPT = 2
N_SAMPLE = 8
N_SEQ = N_PROMPT + N_SAMPLE
N_TOK = N_SEQ * SEQ
EPS = 1e-6
N_MOD = 6

POOL_WINDOWS = (2, 4, 8, 16)
POOL_GROUP = 256
POOL_WIDTH = 1024
POOL_HALO = 16

SSM_GROUPS = 32
SSM_P = 16
SSM_N = 64
SSM_WIDTH = 512
SSM_CHUNK = 32
SSM_NCHUNK = SEQ // SSM_CHUNK
SSM_COLS = SSM_CHUNK * SSM_P
SSM_STATE_ROWS = 4 * SSM_N
SSM_GROUPS_PER_STEP = 8
SSM_SCAN_STEPS = 7

N_GROUPS = 4
EXPERTS_PER_GROUP = 8
N_EXPERTS = 32
TOPK = 2
D_EXPERT = 512
N_ASSIGN = N_TOK * TOPK
ROUTER_COLS = 128

TILE_INPROJ = 512
TILE_MIX = 256
N_TILES = N_TOK // TILE_MIX
TILES_PER_SEQ = SEQ // TILE_MIX
ROUTE_ROWS = 8
MOE_ROWS = 256
MOE_SHIFT = 8
MOE_BLOCKS = N_ASSIGN // MOE_ROWS
N_VISITS = MOE_BLOCKS + N_EXPERTS - 1

VMEM_LIMIT = 60 * 1024 * 1024


def _params(sem, vmem=VMEM_LIMIT):
    return pltpu.CompilerParams(dimension_semantics=sem, vmem_limit_bytes=vmem)


def _resident(shape):
    zeros = (0,) * len(shape)
    return pl.BlockSpec(shape, lambda *_: zeros, pipeline_mode=pl.Buffered(1))


def _ada_norm(x, gain, scale, shift):
    r = lax.rsqrt(jnp.mean(x * x, axis=-1, keepdims=True) + EPS)
    return (x * r * gain) * (1.0 + scale) + shift


def _regroup_matrix(n_outer, n_inner):
    n = n_outer * n_inner
    dst = lax.broadcasted_iota(jnp.int32, (n, n), 0)
    src = lax.broadcasted_iota(jnp.int32, (n, n), 1)
    shift = n_outer.bit_length() - 1
    return (src == (dst & (n_outer - 1)) * n_inner + lax.shift_right_logical(dst, shift)).astype(BF16)


def _mod_kernel(c_ref, w_ref, b_ref, o_ref):
    c = c_ref[...]
    s = c * jax.nn.sigmoid(c)
    o_ref[...] = jnp.dot(s.astype(BF16), w_ref[...].astype(BF16), preferred_element_type=F32) + b_ref[...]


def _mod_call(c_pad, w_ada, b_ada):
    n = w_ada.shape[1]
    tn = 1024
    return pl.pallas_call(
        _mod_kernel,
        grid=(n // tn,),
        in_specs=[pl.BlockSpec(c_pad.shape, lambda j: (0, 0)),
                  pl.BlockSpec((D_MODEL, tn), lambda j: (0, j)),
                  pl.BlockSpec((1, tn), lambda j: (0, j))],
        out_specs=pl.BlockSpec((c_pad.shape[0], tn), lambda j: (0, j)),
        out_shape=jax.ShapeDtypeStruct((c_pad.shape[0], n), F32),
        compiler_params=_params(("arbitrary",)),
        name="mod",
    )(c_pad, w_ada, b_ada)


def _x_specs(tile):
    last = SEQ // tile - 1
    xp = pl.BlockSpec((1, tile, D_MODEL),
                      lambda b, i: (jnp.minimum(b, N_PROMPT - 1), jnp.where(b < N_PROMPT, i, last), 0))
    xs = pl.BlockSpec((1, tile, D_MODEL),
                      lambda b, i: (jnp.maximum(b - N_PROMPT, 0), jnp.where(b < N_PROMPT, 0, i), 0))
    return xp, xs


def _load_x(xp_ref, xs_ref):
    return jnp.where(pl.program_id(0) < N_PROMPT, xp_ref[0], xs_ref[0])


def _inproj_kernel(xp_ref, xs_ref, mod_ref, g_ref, w_ref, ua_ref, ub_ref):
    x = _load_x(xp_ref, xs_ref)
    h = _ada_norm(x, g_ref[...], mod_ref[0, 1:2, :], mod_ref[0, 0:1, :])
    p = jnp.dot(h.astype(BF16), w_ref[...], preferred_element_type=F32)
    ua_ref[0] = p[:, :POOL_WIDTH].astype(BF16)
    n_chunk = TILE_INPROJ // SSM_CHUNK
    ub = jnp.dot(_regroup_matrix(n_chunk, SSM_CHUNK), p[:, POOL_WIDTH:].astype(BF16), preferred_element_type=F32)
    ub_ref[...] = ub.reshape(SSM_CHUNK, n_chunk, SSM_WIDTH).astype(BF16)


def _inproj_call(x_prompt, x_sample, mod, norm_g, w_u):
    tile = TILE_INPROJ
    xp, xs = _x_specs(tile)
    return pl.pallas_call(
        _inproj_kernel,
        grid=(N_SEQ, SEQ // tile),
        in_specs=[xp, xs,
                  pl.BlockSpec((1, N_MOD, D_MODEL), lambda b, i: (b, 0, 0)),
                  _resident((1, D_MODEL)),
                  _resident(w_u.shape)],
        out_specs=[pl.BlockSpec((1, tile, POOL_WIDTH), lambda b, i: (b, i, 0)),
                   pl.BlockSpec((None, SSM_CHUNK, tile // SSM_CHUNK, SSM_WIDTH), lambda b, i: (b, 0, i, 0))],
        out_shape=[jax.ShapeDtypeStruct((N_SEQ, SEQ, POOL_WIDTH), BF16),
                   jax.ShapeDtypeStruct((N_SEQ, SSM_CHUNK, SSM_NCHUNK, SSM_WIDTH), BF16)],
        compiler_params=_params(("arbitrary", "arbitrary")),
        name="inproj",
    )(x_prompt, x_sample, mod, norm_g, w_u)


def _s5_kernel(u_ref, wst_ref, tt_ref, wot_ref, a_ref, y_ref, ut_ref, yt_ref, apow_ref):
    n, nc, ng = SSM_N, SSM_NCHUNK, SSM_GROUPS_PER_STEP

    @pl.when(pl.program_id(1) == 0)
    def _():
        for g in range(ng):
            a = a_ref[g]
            for k in range(SSM_SCAN_STEPS):
                apow_ref[g, k] = a
                f_re, f_im, b_re, b_im = a[0:n], a[n:2 * n], a[2 * n:3 * n], a[3 * n:4 * n]
                a = jnp.concatenate([f_re * f_re - f_im * f_im, 2.0 * (f_re * f_im),
                                     b_re * b_re - b_im * b_im, 2.0 * (b_re * b_im)], axis=0)

    for s in range(SSM_CHUNK):
        blk = u_ref[s].astype(F32).T
        ut_ref[:, pl.ds(s * SSM_P, SSM_P), :] = blk.reshape(ng, SSM_P, nc).astype(BF16)

    lane = lax.broadcasted_iota(jnp.int32, (n, nc), 1)

    def shifted(x, d, forward):
        if forward:
            return jnp.where(lane >= d, pltpu.roll(x, d, 1), 0.0)
        return jnp.where(lane < nc - d, pltpu.roll(x, nc - d, 1), 0.0)

    def scan(x_re, x_im, g, re_row, im_row, forward):
        for k in range(SSM_SCAN_STEPS):
            a_re = apow_ref[g, k, pl.ds(re_row, n), :]
            a_im = apow_ref[g, k, pl.ds(im_row, n), :]
            s_re = shifted(x_re, 2 ** k, forward)
            s_im = shifted(x_im, 2 ** k, forward)
            x_re, x_im = x_re + (a_re * s_re - a_im * s_im), x_im + (a_re * s_im + a_im * s_re)
        return x_re, x_im

    def per_group(g, carry):
        ut = ut_ref[g]
        st = jnp.dot(wst_ref[g], ut, preferred_element_type=F32)
        f_re, f_im = scan(st[0:n], st[n:2 * n], g, 0, n, True)
        b_re, b_im = scan(st[2 * n:3 * n], st[3 * n:4 * n], g, 2 * n, 3 * n, False)
        carried = jnp.concatenate([shifted(f_re, 1, True), shifted(f_im, 1, True),
                                   shifted(b_re, 1, False), shifted(b_im, 1, False)], axis=0)
        yt_ref[g] = (jnp.dot(tt_ref[g], ut, preferred_element_type=F32)
                     + jnp.dot(wot_ref[g], carried.astype(BF16), preferred_element_type=F32))
        return carry

    lax.fori_loop(0, ng, per_group, 0, unroll=2)
    for t in range(SSM_CHUNK):
        y_ref[t] = yt_ref[:, pl.ds(t * SSM_P, SSM_P), :].reshape(ng * SSM_P, nc).T


def _s5_call(u_ph, wst, tt, wot, a_chunk):
    ng = SSM_GROUPS_PER_STEP
    seq = pl.BlockSpec((None, SSM_CHUNK, SSM_NCHUNK, ng * SSM_P), lambda q, b: (b, 0, 0, q))
    mat = lambda rows, cols: pl.BlockSpec((ng, rows, cols), lambda q, b: (q, 0, 0))
    return pl.pallas_call(
        _s5_kernel,
        grid=(SSM_GROUPS // ng, N_SEQ),
        in_specs=[seq, mat(SSM_STATE_ROWS, SSM_COLS), mat(SSM_COLS, SSM_COLS), mat(SSM_COLS, SSM_STATE_ROWS),
                  mat(SSM_STATE_ROWS, SSM_NCHUNK)],
        out_specs=seq,
        out_shape=jax.ShapeDtypeStruct((N_SEQ, SSM_CHUNK, SSM_NCHUNK, SSM_WIDTH), F32),
        scratch_shapes=[pltpu.VMEM((ng, SSM_COLS, SSM_NCHUNK), BF16),
                        pltpu.VMEM((ng, SSM_COLS, SSM_NCHUNK), F32),
                        pltpu.VMEM((ng, SSM_SCAN_STEPS, SSM_STATE_ROWS, SSM_NCHUNK), F32)],
        compiler_params=_params(("arbitrary", "arbitrary")),
        name="s5",
    )(u_ph, wst, tt, wot, a_chunk)


def _s5_direction(a_re, a_im, log_dt, b_re, b_im, c_re, c_im):
    dt = jnp.exp(log_dt)[:, None]
    k = jnp.arange(SSM_CHUNK + 1, dtype=F32)[None, :, None]
    mag = jnp.exp(k * (a_re * dt)[:, None, :])
    ang = k * (a_im * dt)[:, None, :]
    pw_re = mag * jnp.cos(ang)
    pw_im = mag * jnp.sin(ang)
    ab_re, ab_im = pw_re[:, 1], pw_im[:, 1]
    den = a_re * a_re + a_im * a_im
    q_re = ((ab_re - 1.0) * a_re + ab_im * a_im) / den
    q_im = (ab_im * a_re - (ab_re - 1.0) * a_im) / den
    bb_re = q_re[:, :, None] * b_re - q_im[:, :, None] * b_im
    bb_im = q_re[:, :, None] * b_im + q_im[:, :, None] * b_re
    cp_re = c_re[:, None] * pw_re[:, :, None, :] - c_im[:, None] * pw_im[:, :, None, :]
    cp_im = c_re[:, None] * pw_im[:, :, None, :] + c_im[:, None] * pw_re[:, :, None, :]
    contract = lambda c, b: jnp.einsum('gkqn,gnp->gkqp', c[:, :SSM_CHUNK], b, precision=lax.Precision.HIGHEST)
    taps = contract(cp_re, bb_re) - contract(cp_im, bb_im)
    return pw_re, pw_im, bb_re, bb_im, cp_re, cp_im, taps


def _s5_operators(fwd, bwd, ssm_d):
    g, q, p, n = SSM_GROUPS, SSM_CHUNK, SSM_P, SSM_N
    pf_re, pf_im, bf_re, bf_im, cf_re, cf_im, taps_f = _s5_direction(*fwd)
    pb_re, pb_im, bb_re, bb_im, cb_re, cb_im, taps_b = _s5_direction(*bwd)
    s_idx = jnp.arange(q)
    lag = s_idx[None, :] - s_idx[:, None]
    tf = jnp.where((lag >= 0)[None, :, :, None, None], taps_f[:, jnp.clip(lag, 0, q - 1)], 0.0)
    tb = jnp.where((lag <= 0)[None, :, :, None, None], taps_b[:, jnp.clip(-lag, 0, q - 1)], 0.0)
    skip = (jnp.eye(q, dtype=F32)[None, :, :, None, None] * jnp.eye(p, dtype=F32)[None, None, None]
            * ssm_d.reshape(g, 1, 1, 1, p))
    tt = (tf + tb + skip).transpose(0, 2, 3, 1, 4).reshape(g, q * p, q * p)

    def state_in(pw_re, pw_im, b_re, b_im, powers):
        w_re = pw_re[:, powers][:, :, None, :] * b_re.transpose(0, 2, 1)[:, None] \
            - pw_im[:, powers][:, :, None, :] * b_im.transpose(0, 2, 1)[:, None]
        w_im = pw_re[:, powers][:, :, None, :] * b_im.transpose(0, 2, 1)[:, None] \
            + pw_im[:, powers][:, :, None, :] * b_re.transpose(0, 2, 1)[:, None]
        return w_re.reshape(g, q * p, n), w_im.reshape(g, q * p, n)

    wsf_re, wsf_im = state_in(pf_re, pf_im, bf_re, bf_im, q - 1 - s_idx)
    wsb_re, wsb_im = state_in(pb_re, pb_im, bb_re, bb_im, s_idx)
    wst = jnp.concatenate([wsf_re, wsf_im, wsb_re, wsb_im], axis=2).transpose(0, 2, 1)

    def state_out(cp_re, cp_im, powers):
        o_re = cp_re[:, powers].reshape(g, q * p, n)
        o_im = -cp_im[:, powers].reshape(g, q * p, n)
        return o_re, o_im

    of_re, of_im = state_out(cf_re, cf_im, s_idx + 1)
    ob_re, ob_im = state_out(cb_re, cb_im, q - s_idx)
    wot = jnp.concatenate([of_re, of_im, ob_re, ob_im], axis=2)
    a_chunk = jnp.concatenate([pf_re[:, q], pf_im[:, q], pb_re[:, q], pb_im[:, q]], axis=1)
    a_chunk = jnp.broadcast_to(a_chunk[:, :, None], (g, SSM_STATE_ROWS, SSM_NCHUNK))
    return wst.astype(BF16), tt.astype(BF16), wot.astype(BF16), a_chunk


def _gelu_tanh(x):
    return 0.5 * x * (1.0 + jnp.tanh(math.sqrt(2.0 / math.pi) * (x + 0.044715 * (x * x * x))))


def _route_tile(logits, run_ref, route_ref, gate_ref, counts_ref):
    tile = logits.shape[0]
    neg = -jnp.inf
    lt = logits.T
    row8 = lax.broadcasted_iota(jnp.int32, (EXPERTS_PER_GROUP, tile), 0)
    gl = jnp.where(row8 < N_GROUPS, lt[N_EXPERTS:N_EXPERTS + 8], neg)
    gmax = jnp.max(gl, axis=0, keepdims=True)
    g_sel = jnp.min(jnp.where(gl == gmax, row8, 8), axis=0, keepdims=True)
    g_p = 1.0 / jnp.sum(jnp.exp(gl - gmax), axis=0, keepdims=True)
    in_grp = lt[0:EXPERTS_PER_GROUP]
    for g in range(1, N_GROUPS):
        in_grp = jnp.where(g_sel == g, lt[g * EXPERTS_PER_GROUP:(g + 1) * EXPERTS_PER_GROUP], in_grp)
    m1 = jnp.max(in_grp, axis=0, keepdims=True)
    i1 = jnp.min(jnp.where(in_grp == m1, row8, 8), axis=0, keepdims=True)
    rest = jnp.where(row8 == i1, neg, in_grp)
    m2 = jnp.max(rest, axis=0, keepdims=True)
    i2 = jnp.min(jnp.where(rest == m2, row8, 8), axis=0, keepdims=True)
    e21 = jnp.exp(m2 - m1)
    p1 = 1.0 / (1.0 + e21)
    eid1 = g_sel * EXPERTS_PER_GROUP + i1
    eid2 = g_sel * EXPERTS_PER_GROUP + i2

    row_e = lax.broadcasted_iota(jnp.int32, (N_EXPERTS, tile), 0)
    oh1 = (row_e == eid1).astype(F32)
    oh2 = (row_e == eid2).astype(F32)
    earlier = (lax.broadcasted_iota(jnp.int32, (tile, tile), 0)
               < lax.broadcasted_iota(jnp.int32, (tile, tile), 1)).astype(BF16)
    before1 = jnp.dot(oh1.astype(BF16), earlier, preferred_element_type=F32)
    before2 = jnp.dot(oh2.astype(BF16), earlier, preferred_element_type=F32)
    tot1 = jnp.sum(oh1, axis=1, keepdims=True)
    tot2 = jnp.sum(oh2, axis=1, keepdims=True)
    run = run_ref[:, 0:1]
    rank1 = jnp.sum(oh1 * (before1 + run), axis=0, keepdims=True)
    rank2 = jnp.sum(oh2 * (before2 + (run + tot1)), axis=0, keepdims=True)
    new_run = jnp.broadcast_to(run + tot1 + tot2, run_ref.shape)
    run_ref[...] = new_run
    counts_ref[...] = new_run
    zi = jnp.zeros((ROUTE_ROWS - 4, tile), jnp.int32)
    route_ref[0] = jnp.concatenate([eid1, eid2, rank1.astype(jnp.int32), rank2.astype(jnp.int32), zi], axis=0)
    zf = jnp.zeros((ROUTE_ROWS - 2, tile), F32)
    gate_ref[0] = jnp.concatenate([g_p * p1, g_p * (e21 * p1), zf], axis=0)


def _mix_kernel(xp_ref, xs_ref, mod_ref, g1_ref, g2_ref, ua_ref, ua_prev_ref, ua_next_ref, ys_ref,
                wg_ref, wpool_ref, pscale_ref, wpa_ref, wglu_ref, bglu_ref, wpb_ref, wout_ref,
                wr_ref, br_ref,
                x1_ref, h2_ref, route_ref, gate_ref, counts_ref, ext_ref, diff_ref, merged_ref, run_ref):
    tile = TILE_MIX
    i = pl.program_id(1)

    @pl.when((pl.program_id(0) == 0) & (i == 0))
    def _():
        run_ref[...] = jnp.zeros_like(run_ref)

    x = _load_x(xp_ref, xs_ref)
    h = _ada_norm(x, g1_ref[...], mod_ref[0, 1:2, :], mod_ref[0, 0:1, :]).astype(BF16)

    first = i == 0
    last = i == pl.num_programs(1) - 1
    ext_ref[pl.ds(0, POOL_HALO), :] = jnp.where(first, 0.0, ua_prev_ref[0].astype(F32))
    ext_ref[pl.ds(POOL_HALO, tile), :] = ua_ref[0].astype(F32)
    ext_ref[pl.ds(POOL_HALO + tile, POOL_HALO), :] = jnp.where(last, 0.0, ua_next_ref[0].astype(F32))
    pos = i * tile + lax.broadcasted_iota(jnp.int32, (tile, 1), 0)
    for k, w in enumerate(POOL_WINDOWS):
        cols = pl.ds(k * POOL_GROUP, POOL_GROUP)
        lo = jnp.maximum(pos - w // 2, 0)
        hi = jnp.minimum(pos + (w - 1 - w // 2), SEQ - 1)
        inv_cnt = 1.0 / (hi - lo + 1).astype(F32)
        acc = ext_ref[pl.ds(POOL_HALO - w // 2, tile), cols]
        for j in range(1, w):
            acc = acc + ext_ref[pl.ds(POOL_HALO - w // 2 + j, tile), cols]
        diff = acc * inv_cnt - ext_ref[pl.ds(POOL_HALO, tile), cols]
        mixed = jnp.dot(diff.astype(BF16), wpool_ref[k], preferred_element_type=F32)
        diff_ref[:, cols] = (mixed * pscale_ref[:, cols]).astype(BF16)

    z = _gelu_tanh(ys_ref[...].reshape(tile, SSM_WIDTH))
    zg = z * jax.nn.sigmoid(jnp.dot(z.astype(BF16), wglu_ref[...], preferred_element_type=F32) + bglu_ref[...])
    zg = jnp.dot(_regroup_matrix(SSM_CHUNK, tile // SSM_CHUNK), zg.astype(BF16),
                 preferred_element_type=F32).astype(BF16)
    pa = diff_ref[...]

    chunk = 1024
    for j in range(D_MODEL // chunk):
        c0 = j * chunk
        g_a = jnp.dot(h, wg_ref[:, pl.ds(c0, chunk)], preferred_element_type=F32)
        y_a = jnp.dot(pa, wpa_ref[:, pl.ds(c0, chunk)], preferred_element_type=F32)
        m = jax.nn.sigmoid(g_a) * y_a
        g_b = jnp.dot(h, wg_ref[:, pl.ds(D_MODEL + c0, chunk)], preferred_element_type=F32)
        y_b = jnp.dot(zg, wpb_ref[:, pl.ds(c0, chunk)], preferred_element_type=F32)
        m = m + jax.nn.sigmoid(g_b) * y_b
        merged_ref[:, pl.ds(c0, chunk)] = m.astype(BF16)

    x1 = x + mod_ref[0, 2:3, :] * jnp.dot(merged_ref[...], wout_ref[...], preferred_element_type=F32)
    x1_ref[0] = x1
    h2 = _ada_norm(x1, g2_ref[...], mod_ref[0, 4:5, :], mod_ref[0, 3:4, :])
    h2_ref[0] = h2
    h2_hi = h2.astype(BF16)
    h2_lo = (h2 - h2_hi.astype(F32)).astype(BF16)
    logits = (jnp.dot(h2_hi, wr_ref[0], preferred_element_type=F32)
              + (jnp.dot(h2_lo, wr_ref[0], preferred_element_type=F32)
                 + jnp.dot(h2_hi, wr_ref[1], preferred_element_type=F32))) + br_ref[...]
    _route_tile(logits, run_ref, route_ref, gate_ref, counts_ref)


def _mix_call(x_prompt, x_sample, mod, norm1_g, norm2_g, u_a, y_s5, w_g, w_pool, pool_scale, w_pa,
              w_glu, b_glu, w_pb, w_out, w_r, b_r):
    tile = TILE_MIX
    xp, xs = _x_specs(tile)
    halo_per_tile = tile // POOL_HALO
    n_halo = SEQ // POOL_HALO
    seq_tile = lambda width: pl.BlockSpec((1, tile, width), lambda b, i: (b, i, 0))
    return pl.pallas_call(
        _mix_kernel,
        grid=(N_SEQ, SEQ // tile),
        in_specs=[xp, xs,
                  pl.BlockSpec((1, N_MOD, D_MODEL), lambda b, i: (b, 0, 0)),
                  _resident((1, D_MODEL)), _resident((1, D_MODEL)),
                  seq_tile(POOL_WIDTH),
                  pl.BlockSpec((1, POOL_HALO, POOL_WIDTH),
                               lambda b, i: (b, jnp.maximum(i * halo_per_tile - 1, 0), 0)),
                  pl.BlockSpec((1, POOL_HALO, POOL_WIDTH),
                               lambda b, i: (b, jnp.minimum((i + 1) * halo_per_tile, n_halo - 1), 0)),
                  pl.BlockSpec((None, SSM_CHUNK, tile // SSM_CHUNK, SSM_WIDTH), lambda b, i: (b, 0, i, 0)),
                  _resident(w_g.shape), _resident(w_pool.shape), _resident(pool_scale.shape),
                  _resident(w_pa.shape), _resident(w_glu.shape), _resident(b_glu.shape),
                  _resident(w_pb.shape), _resident(w_out.shape), _resident(w_r.shape), _resident(b_r.shape)],
        out_specs=[seq_tile(D_MODEL), seq_tile(D_MODEL),
                   pl.BlockSpec((1, ROUTE_ROWS, tile), lambda b, i: (b * TILES_PER_SEQ + i, 0, 0)),
                   pl.BlockSpec((1, ROUTE_ROWS, tile), lambda b, i: (b * TILES_PER_SEQ + i, 0, 0)),
                   pl.BlockSpec((N_EXPERTS, 128), lambda b, i: (0, 0))],
        out_shape=[jax.ShapeDtypeStruct((N_SEQ, SEQ, D_MODEL), F32),
                   jax.ShapeDtypeStruct((N_SEQ, SEQ, D_MODEL), F32),
                   jax.ShapeDtypeStruct((N_TILES, ROUTE_ROWS, tile), jnp.int32),
                   jax.ShapeDtypeStruct((N_TILES, ROUTE_ROWS, tile), F32),
                   jax.ShapeDtypeStruct((N_EXPERTS, 128), F32)],
        scratch_shapes=[pltpu.VMEM((tile + 2 * POOL_HALO, POOL_WIDTH), F32),
                        pltpu.VMEM((tile, POOL_WIDTH), BF16),
                        pltpu.VMEM((tile, D_MODEL), BF16),
                        pltpu.VMEM((N_EXPERTS, 128), F32)],
        compiler_params=_params(("arbitrary", "arbitrary")),
        name="mix",
    )(x_prompt, x_sample, mod, norm1_g, norm2_g, u_a, u_a, u_a, y_s5, w_g, w_pool, pool_scale, w_pa,
      w_glu, b_glu, w_pb, w_out, w_r, b_r)


def _plan_kernel(cnt_ref, route_ref, counts_ref, dest_ref, vblock_ref, vexpert_ref, vlo_ref):
    below = (lax.broadcasted_iota(jnp.int32, (N_EXPERTS, N_EXPERTS), 1)
             < lax.broadcasted_iota(jnp.int32, (N_EXPERTS, N_EXPERTS), 0)).astype(F32)
    starts = jnp.dot(below, counts_ref[...], preferred_element_type=F32, precision=lax.Precision.HIGHEST)
    starts_b = jnp.broadcast_to(starts[:, 0:1], (N_EXPERTS, TILE_MIX))
    row_e = lax.broadcasted_iota(jnp.int32, (N_EXPERTS, TILE_MIX), 0)
    zi = jnp.zeros((ROUTE_ROWS - 2, TILE_MIX), jnp.int32)

    def per_tile(t, carry):
        r = route_ref[t]
        s1 = jnp.sum(jnp.where(row_e == r[0:1], starts_b, 0.0), axis=0, keepdims=True)
        s2 = jnp.sum(jnp.where(row_e == r[1:2], starts_b, 0.0), axis=0, keepdims=True)
        dest_ref[t] = jnp.concatenate([s1.astype(jnp.int32) + r[2:3], s2.astype(jnp.int32) + r[3:4], zi], axis=0)
        return carry

    lax.fori_loop(0, N_TILES, per_tile, 0)

    def per_expert(e, carry):
        v, start, last_e = carry
        cnt = cnt_ref[e]
        end = start + cnt
        first = lax.shift_right_logical(start, MOE_SHIFT)
        n_blk = jnp.where(cnt > 0, lax.shift_right_logical(end - 1, MOE_SHIFT) - first + 1, 0)

        def per_block(k, v):
            blk = first + k
            vblock_ref[v] = blk
            vexpert_ref[v] = e
            vlo_ref[v] = jnp.maximum(start - blk * MOE_ROWS, 0)
            return v + 1

        v = lax.fori_loop(0, n_blk, per_block, v)
        return v, end, jnp.where(cnt > 0, e, last_e)

    v, _, last_e = lax.fori_loop(0, N_EXPERTS, per_expert, (0, 0, 0))

    def idle(k, carry):
        vblock_ref[k] = MOE_BLOCKS - 1
        vexpert_ref[k] = last_e
        vlo_ref[k] = MOE_ROWS
        return carry

    lax.fori_loop(v, N_VISITS, idle, 0)


def _plan_call(cnt, route, counts):
    smem = pl.BlockSpec(memory_space=pltpu.SMEM)
    vmem = pl.BlockSpec(memory_space=pltpu.VMEM)
    visits = jax.ShapeDtypeStruct((N_VISITS,), jnp.int32)
    return pl.pallas_call(
        _plan_kernel,
        in_specs=[smem, vmem, vmem],
        out_specs=[vmem, smem, smem, smem],
        out_shape=[jax.ShapeDtypeStruct((N_TILES, ROUTE_ROWS, TILE_MIX), jnp.int32), visits, visits, visits],
        name="plan",
    )(cnt, route, counts)


def _dest_spec(index_map):
    return pl.BlockSpec((1, 1, TOPK * TILE_MIX), index_map, memory_space=pltpu.SMEM)


def _dispatch_kernel(dest_ref, h_ref, xs_ref, sem):
    tile = TILE_MIX

    def row(r, carry):
        for k in range(TOPK):
            d = dest_ref[0, 0, k * tile + r]
            pltpu.make_async_copy(h_ref.at[pl.ds(r, 1)], xs_ref.at[pl.ds(d, 1)], sem).start()
        return carry

    lax.fori_loop(0, tile, row, 0, unroll=8)
    for k in range(TOPK):
        pltpu.make_async_copy(h_ref, xs_ref.at[pl.ds(0, tile)], sem).wait()


def _dispatch_call(dest, h2):
    return pl.pallas_call(
        _dispatch_kernel,
        grid=(N_TILES,),
        in_specs=[_dest_spec(lambda t: (t, 0, 0)),
                  pl.BlockSpec((TILE_MIX, D_MODEL), lambda t: (t, 0))],
        out_specs=pl.BlockSpec(memory_space=pl.ANY),
        out_shape=jax.ShapeDtypeStruct((N_ASSIGN, D_MODEL), F32),
        scratch_shapes=[pltpu.SemaphoreType.DMA],
        compiler_params=_params(("arbitrary",)),
        name="dispatch",
    )(dest, h2)


def _expert_kernel(vblock_ref, vexpert_ref, vlo_ref, x_ref, wg_ref, wu_ref, wd_ref, o_ref, wgu_s, wd_s):
    v = pl.program_id(0)
    e = vexpert_ref[v]

    @pl.when((v == 0) | (e != vexpert_ref[jnp.maximum(v - 1, 0)]))
    def _():
        wgu_s[:, :D_EXPERT] = wg_ref[0].astype(BF16)
        wgu_s[:, D_EXPERT:] = wu_ref[0].astype(BF16)
        wd_s[...] = wd_ref[0].astype(BF16)

    lo = vlo_ref[v]

    @pl.when(lo < MOE_ROWS)
    def _():
        gu = jnp.dot(x_ref[...].astype(BF16), wgu_s[...], preferred_element_type=F32)
        g = gu[:, :D_EXPERT]
        act = (g * jax.nn.sigmoid(g)) * gu[:, D_EXPERT:]
        res = jnp.dot(act.astype(BF16), wd_s[...], preferred_element_type=F32)

        @pl.when(lo == 0)
        def _():
            o_ref[...] = res

        @pl.when(lo > 0)
        def _():
            rows = lax.broadcasted_iota(jnp.int32, (MOE_ROWS, 1), 0)
            o_ref[...] = jnp.where(rows >= lo, res, o_ref[...])


def _expert_call(vblock, vexpert, vlo, x_slots, w_gate, w_up, w_down):
    grid_spec = pltpu.PrefetchScalarGridSpec(
        num_scalar_prefetch=3,
        grid=(N_VISITS,),
        in_specs=[pl.BlockSpec((MOE_ROWS, D_MODEL), lambda v, vb, ve, vl: (vb[v], 0)),
                  pl.BlockSpec((1, D_MODEL, D_EXPERT), lambda v, vb, ve, vl: (ve[v], 0, 0)),
                  pl.BlockSpec((1, D_MODEL, D_EXPERT), lambda v, vb, ve, vl: (ve[v], 0, 0)),
                  pl.BlockSpec((1, D_EXPERT, D_MODEL), lambda v, vb, ve, vl: (ve[v], 0, 0))],
        out_specs=pl.BlockSpec((MOE_ROWS, D_MODEL), lambda v, vb, ve, vl: (vb[v], 0)),
        scratch_shapes=[pltpu.VMEM((D_MODEL, 2 * D_EXPERT), BF16),
                        pltpu.VMEM((D_EXPERT, D_MODEL), BF16)],
    )
    return pl.pallas_call(
        _expert_kernel,
        grid_spec=grid_spec,
        out_shape=jax.ShapeDtypeStruct((N_ASSIGN, D_MODEL), F32),
        compiler_params=_params(("arbitrary",)),
        name="experts",
    )(vblock, vexpert, vlo, x_slots, w_gate, w_up, w_down)


def _final_kernel(dest_ref, dest_next_ref, x1_ref, mod_ref, gate_ref, g_ref, y_ref, o_ref, rows_ref, sem):
    tile = TILE_MIX
    step = pl.program_id(0) * pl.num_programs(1) + pl.program_id(1)
    n_steps = pl.num_programs(0) * pl.num_programs(1)
    slot = lax.rem(step, 2)

    def fetch(dst_ref, into):
        def row(r, carry):
            for k in range(TOPK):
                d = dst_ref[0, 0, k * tile + r]
                pltpu.make_async_copy(y_ref.at[pl.ds(d, 1)], rows_ref.at[into, k, pl.ds(r, 1)], sem.at[into]).start()
            return carry

        lax.fori_loop(0, tile, row, 0, unroll=8)

    @pl.when(step == 0)
    def _():
        fetch(dest_ref, 0)

    @pl.when(step + 1 < n_steps)
    def _():
        fetch(dest_next_ref, 1 - slot)

    for k in range(TOPK):
        pltpu.make_async_copy(y_ref.at[pl.ds(0, tile)], rows_ref.at[slot, k], sem.at[slot]).wait()
    gate = gate_ref[0].T
    moe = gate[:, 0:1] * rows_ref[slot, 0] + gate[:, 1:2] * rows_ref[slot, 1]
    x2 = x1_ref[0] + mod_ref[0, 5:6, :] * moe
    r = lax.rsqrt(jnp.mean(x2 * x2, axis=-1, keepdims=True) + EPS)
    o_ref[0] = x2 * r * g_ref[...]


def _final_call(x1, mod, y_slots, dest, gates, final_g, first_seq, n_seq):
    tile = TILE_MIX
    tile_of = lambda b, i: (b + first_seq) * TILES_PER_SEQ + i
    last_tile = (first_seq + n_seq) * TILES_PER_SEQ - 1
    return pl.pallas_call(
        _final_kernel,
        grid=(n_seq, TILES_PER_SEQ),
        in_specs=[_dest_spec(lambda b, i: (tile_of(b, i), 0, 0)),
                  _dest_spec(lambda b, i: (jnp.minimum(tile_of(b, i) + 1, last_tile), 0, 0)),
                  pl.BlockSpec((1, tile, D_MODEL), lambda b, i: (b + first_seq, i, 0)),
                  pl.BlockSpec((1, N_MOD, D_MODEL), lambda b, i: (b + first_seq, 0, 0)),
                  pl.BlockSpec((1, ROUTE_ROWS, tile), lambda b, i: (tile_of(b, i), 0, 0)),
                  _resident((1, D_MODEL)),
                  pl.BlockSpec(memory_space=pl.ANY)],
        out_specs=pl.BlockSpec((1, tile, D_MODEL), lambda b, i: (b, i, 0)),
        out_shape=jax.ShapeDtypeStruct((n_seq, SEQ, D_MODEL), F32),
        scratch_shapes=[pltpu.VMEM((2, TOPK, tile, D_MODEL), F32), pltpu.SemaphoreType.DMA((2,))],
        compiler_params=_params(("arbitrary", "arbitrary")),
        name="final",
    )(dest, dest, x1, mod, gates, final_g, y_slots)


def kernel(x_prompt, x_sample, c_prompt, c_sample, w_ada, b_ada, norm1_g, w_in, w_pool, pool_scale,
           ssm_a_re_f, ssm_a_im_f, ssm_log_dt_f, ssm_b_re_f, ssm_b_im_f, ssm_c_re_f, ssm_c_im_f,
           ssm_a_re_b, ssm_a_im_b, ssm_log_dt_b, ssm_b_re_b, ssm_b_im_b, ssm_c_re_b, ssm_c_im_b,
           ssm_d, w_glu, b_glu, w_proj_a, w_proj_b, w_out, norm2_g,
           w_grp, b_grp, w_router, b_router, w_exp_gate, w_exp_up, w_exp_down, final_g):
    n_u = POOL_WIDTH + SSM_WIDTH
    c_pad = jnp.concatenate([c_prompt, c_sample, jnp.zeros((16 - N_SEQ, D_MODEL), F32)], axis=0)
    mod = _mod_call(c_pad, w_ada[0], b_ada).reshape(16, N_MOD, D_MODEL)

    w_in_bf = w_in[0].astype(BF16)
    u_a, u_b = _inproj_call(x_prompt, x_sample, mod, norm1_g, w_in_bf[:, :n_u])

    fwd = (ssm_a_re_f[0], ssm_a_im_f[0], ssm_log_dt_f[0], ssm_b_re_f[0], ssm_b_im_f[0], ssm_c_re_f[0], ssm_c_im_f[0])
    bwd = (ssm_a_re_b[0], ssm_a_im_b[0], ssm_log_dt_b[0], ssm_b_re_b[0], ssm_b_im_b[0], ssm_c_re_b[0], ssm_c_im_b[0])
    y_s5 = _s5_call(u_b, *_s5_operators(fwd, bwd, ssm_d[0]))

    w_r = jnp.concatenate([w_router[0], w_grp[0],
                           jnp.zeros((D_MODEL, ROUTER_COLS - N_GROUPS - N_EXPERTS), F32)], axis=1)
    b_r = jnp.concatenate([b_router[0], b_grp[0],
                           jnp.zeros((ROUTER_COLS - N_GROUPS - N_EXPERTS,), F32)])[None, :]
    w_r_hi = w_r.astype(BF16)
    w_r = jnp.stack([w_r_hi, (w_r - w_r_hi.astype(F32)).astype(BF16)])
    x1, h2, route, gates, counts = _mix_call(
        x_prompt, x_sample, mod, norm1_g, norm2_g, u_a, y_s5, w_in_bf[:, n_u:], w_pool[0].astype(BF16),
        pool_scale, w_proj_a[0].astype(BF16), w_glu[0].astype(BF16), b_glu, w_proj_b[0].astype(BF16),
        w_out[0].astype(BF16), w_r, b_r)

    dest8, vblock, vexpert, vlo = _plan_call(counts[:, 0].astype(jnp.int32), route, counts)
    dest = dest8[:, :TOPK, :].reshape(N_TILES, 1, TOPK * TILE_MIX)
    x_slots = _dispatch_call(dest, h2.reshape(N_TOK, D_MODEL))
    y_slots = _expert_call(vblock, vexpert, vlo, x_slots, w_exp_gate[0], w_exp_up[0], w_exp_down[0])

    final_g2 = final_g[None, :]
    y_prompt = _final_call(x1, mod, y_slots, dest, gates, final_g2, 0, N_PROMPT)
    y_sample = _final_call(x1, mod, y_slots, dest, gates, final_g2, N_PROMPT, N_SAMPLE)
    return (y_prompt, y_sample)
```

```python
import math

import jax
import jax.numpy as jnp
from jax import lax
from jax.experimental import pallas as pl
from jax.experimental.pallas import tpu as pltpu

F32 = jnp.float32
BF16 = jnp.bfloat16

D_MODEL = 2048
SEQ = 4096
N_PROMPT = 2
N_SAMPLE = 8
N_SEQ = N_PROMPT + N_SAMPLE
N_TOK = N_SEQ * SEQ
EPS = 1e-6
N_MOD = 6

POOL_WINDOWS = (2, 4, 8, 16)
POOL_GROUP = 256
POOL_WIDTH = 1024
POOL_HALO = 16

SSM_GROUPS = 32
SSM_P = 16
SSM_N = 64
SSM_WIDTH = 512
SSM_CHUNK = 32
SSM_NCHUNK = SEQ // SSM_CHUNK
SSM_COLS = SSM_CHUNK * SSM_P
SSM_STATE_ROWS = 4 * SSM_N
SSM_GROUPS_PER_STEP = 8
SSM_SCAN_STEPS = 7
SSM_TAP_LANES = 2 * SSM_COLS

N_GROUPS = 4
EXPERTS_PER_GROUP = 8
N_EXPERTS = 32
TOPK = 2
D_EXPERT = 512
N_ASSIGN = N_TOK * TOPK
ROUTER_COLS = 128

TILE_INPROJ = 512
TILE_MIX = 256
N_TILES = N_TOK // TILE_MIX
TILES_PER_SEQ = SEQ // TILE_MIX
ROUTE_ROWS = 8
MOE_ROWS = 256
MOE_SHIFT = 8
MOE_BLOCKS = N_ASSIGN // MOE_ROWS
N_VISITS = MOE_BLOCKS + N_EXPERTS - 1

VMEM_LIMIT = 60 * 1024 * 1024


def _params(sem, vmem=VMEM_LIMIT):
    return pltpu.CompilerParams(dimension_semantics=sem, vmem_limit_bytes=vmem)


def _resident(shape):
    zeros = (0,) * len(shape)
    return pl.BlockSpec(shape, lambda *_: zeros, pipeline_mode=pl.Buffered(1))


def _ada_norm(x, gain, scale, shift):
    r = lax.rsqrt(jnp.mean(x * x, axis=-1, keepdims=True) + EPS)
    return (x * r * gain) * (1.0 + scale) + shift


def _regroup_matrix(n_outer, n_inner):
    n = n_outer * n_inner
    dst = lax.broadcasted_iota(jnp.int32, (n, n), 0)
    src = lax.broadcasted_iota(jnp.int32, (n, n), 1)
    shift = n_outer.bit_length() - 1
    return (src == (dst & (n_outer - 1)) * n_inner + lax.shift_right_logical(dst, shift)).astype(BF16)


def _mod_kernel(c_ref, w_ref, b_ref, o_ref):
    c = c_ref[...]
    s = c * jax.nn.sigmoid(c)
    o_ref[...] = jnp.dot(s.astype(BF16), w_ref[...].astype(BF16), preferred_element_type=F32) + b_ref[...]


def _mod_call(c_pad, w_ada, b_ada):
    n = w_ada.shape[1]
    tn = 1024
    return pl.pallas_call(
        _mod_kernel,
        grid=(n // tn,),
        in_specs=[pl.BlockSpec(c_pad.shape, lambda j: (0, 0)),
                  pl.BlockSpec((D_MODEL, tn), lambda j: (0, j)),
                  pl.BlockSpec((1, tn), lambda j: (0, j))],
        out_specs=pl.BlockSpec((c_pad.shape[0], tn), lambda j: (0, j)),
        out_shape=jax.ShapeDtypeStruct((c_pad.shape[0], n), F32),
        compiler_params=_params(("arbitrary",)),
        name="mod",
    )(c_pad, w_ada, b_ada)


def _x_specs(tile):
    last = SEQ // tile - 1
    xp = pl.BlockSpec((1, tile, D_MODEL),
                      lambda b, i: (jnp.minimum(b, N_PROMPT - 1), jnp.where(b < N_PROMPT, i, last), 0))
    xs = pl.BlockSpec((1, tile, D_MODEL),
                      lambda b, i: (jnp.maximum(b - N_PROMPT, 0), jnp.where(b < N_PROMPT, 0, i), 0))
    return xp, xs


def _load_x(xp_ref, xs_ref):
    return jnp.where(pl.program_id(0) < N_PROMPT, xp_ref[0], xs_ref[0])


def _inproj_kernel(xp_ref, xs_ref, mod_ref, g_ref, w_ref, ua_ref, ub_ref):
    x = _load_x(xp_ref, xs_ref)
    h = _ada_norm(x, g_ref[...], mod_ref[0, 1:2, :], mod_ref[0, 0:1, :])
    p = jnp.dot(h.astype(BF16), w_ref[...], preferred_element_type=F32)
    ua_ref[0] = p[:, :POOL_WIDTH].astype(BF16)
    n_chunk = TILE_INPROJ // SSM_CHUNK
    ub = jnp.dot(_regroup_matrix(n_chunk, SSM_CHUNK), p[:, POOL_WIDTH:].astype(BF16), preferred_element_type=F32)
    ub_ref[...] = ub.reshape(SSM_CHUNK, n_chunk, SSM_WIDTH).astype(BF16)


def _inproj_call(x_prompt, x_sample, mod, norm_g, w_u):
    tile = TILE_INPROJ
    xp, xs = _x_specs(tile)
    return pl.pallas_call(
        _inproj_kernel,
        grid=(N_SEQ, SEQ // tile),
        in_specs=[xp, xs,
                  pl.BlockSpec((1, N_MOD, D_MODEL), lambda b, i: (b, 0, 0)),
                  _resident((1, D_MODEL)),
                  _resident(w_u.shape)],
        out_specs=[pl.BlockSpec((1, tile, POOL_WIDTH), lambda b, i: (b, i, 0)),
                   pl.BlockSpec((None, SSM_CHUNK, tile // SSM_CHUNK, SSM_WIDTH), lambda b, i: (b, 0, i, 0))],
        out_shape=[jax.ShapeDtypeStruct((N_SEQ, SEQ, POOL_WIDTH), BF16),
                   jax.ShapeDtypeStruct((N_SEQ, SSM_CHUNK, SSM_NCHUNK, SSM_WIDTH), BF16)],
        compiler_params=_params(("arbitrary", "arbitrary")),
        name="inproj",
    )(x_prompt, x_sample, mod, norm_g, w_u)


def _s5_kernel(u_ref, wst_ref, taps_ref, wot_ref, a_ref, y_ref, ut_ref, yt_ref, apow_ref, tt_ref):
    n, nc, ng = SSM_N, SSM_NCHUNK, SSM_GROUPS_PER_STEP

    @pl.when(pl.program_id(1) == 0)
    def _():
        for g in range(ng):
            a = a_ref[g]
            for k in range(SSM_SCAN_STEPS):
                apow_ref[g, k] = a
                f_re, f_im, b_re, b_im = a[0:n], a[n:2 * n], a[2 * n:3 * n], a[3 * n:4 * n]
                a = jnp.concatenate([f_re * f_re - f_im * f_im, 2.0 * (f_re * f_im),
                                     b_re * b_re - b_im * b_im, 2.0 * (b_re * b_im)], axis=0)
            strip = taps_ref[g]
            for t in range(SSM_CHUNK):
                start = (SSM_CHUNK - 1 - t) * SSM_P
                window = pltpu.roll(strip, SSM_TAP_LANES - start, 1) if start else strip
                tt_ref[g, pl.ds(t * SSM_P, SSM_P), :] = window[:, :SSM_COLS].astype(BF16)

    for s in range(SSM_CHUNK):
        blk = u_ref[s].astype(F32).T
        ut_ref[:, pl.ds(s * SSM_P, SSM_P), :] = blk.reshape(ng, SSM_P, nc).astype(BF16)

    lane = lax.broadcasted_iota(jnp.int32, (n, nc), 1)

    def shifted(x, d, forward):
        if forward:
            return jnp.where(lane >= d, pltpu.roll(x, d, 1), 0.0)
        return jnp.where(lane < nc - d, pltpu.roll(x, nc - d, 1), 0.0)

    def scan(x_re, x_im, g, re_row, im_row, forward):
        for k in range(SSM_SCAN_STEPS):
            a_re = apow_ref[g, k, pl.ds(re_row, n), :]
            a_im = apow_ref[g, k, pl.ds(im_row, n), :]
            s_re = shifted(x_re, 2 ** k, forward)
            s_im = shifted(x_im, 2 ** k, forward)
            x_re, x_im = x_re + (a_re * s_re - a_im * s_im), x_im + (a_re * s_im + a_im * s_re)
        return x_re, x_im

    def per_group(g, carry):
        ut = ut_ref[g]
        st = jnp.dot(wst_ref[g], ut, preferred_element_type=F32)
        f_re, f_im = scan(st[0:n], st[n:2 * n], g, 0, n, True)
        b_re, b_im = scan(st[2 * n:3 * n], st[3 * n:4 * n], g, 2 * n, 3 * n, False)
        carried = jnp.concatenate([shifted(f_re, 1, True), shifted(f_im, 1, True),
                                   shifted(b_re, 1, False), shifted(b_im, 1, False)], axis=0)
        yt_ref[g] = (jnp.dot(tt_ref[g], ut, preferred_element_type=F32)
                     + jnp.dot(wot_ref[g], carried.astype(BF16), preferred_element_type=F32))
        return carry

    lax.fori_loop(0, ng, per_group, 0, unroll=2)
    for t in range(SSM_CHUNK):
        y_ref[t] = yt_ref[:, pl.ds(t * SSM_P, SSM_P), :].reshape(ng * SSM_P, nc).T


def _s5_call(u_ph, wst, tap_strip, wot, a_chunk):
    ng = SSM_GROUPS_PER_STEP
    seq = pl.BlockSpec((None, SSM_CHUNK, SSM_NCHUNK, ng * SSM_P), lambda q, b: (b, 0, 0, q))
    mat = lambda rows, cols: pl.BlockSpec((ng, rows, cols), lambda q, b: (q, 0, 0))
    return pl.pallas_call(
        _s5_kernel,
        grid=(SSM_GROUPS // ng, N_SEQ),
        in_specs=[seq, mat(SSM_STATE_ROWS, SSM_COLS), mat(SSM_P, SSM_TAP_LANES), mat(SSM_COLS, SSM_STATE_ROWS),
                  mat(SSM_STATE_ROWS, SSM_NCHUNK)],
        out_specs=seq,
        out_shape=jax.ShapeDtypeStruct((N_SEQ, SSM_CHUNK, SSM_NCHUNK, SSM_WIDTH), F32),
        scratch_shapes=[pltpu.VMEM((ng, SSM_COLS, SSM_NCHUNK), BF16),
                        pltpu.VMEM((ng, SSM_COLS, SSM_NCHUNK), F32),
                        pltpu.VMEM((ng, SSM_SCAN_STEPS, SSM_STATE_ROWS, SSM_NCHUNK), F32),
                        pltpu.VMEM((ng, SSM_COLS, SSM_COLS), BF16)],
        compiler_params=_params(("arbitrary", "arbitrary")),
        name="s5",
    )(u_ph, wst, tap_strip, wot, a_chunk)


def _s5_direction(a_re, a_im, log_dt, b_re, b_im, c_re, c_im):
    dt = jnp.exp(log_dt)[:, None]
    k = jnp.arange(SSM_CHUNK + 1, dtype=F32)[None, :, None]
    mag = jnp.exp(k * (a_re * dt)[:, None, :])
    ang = k * (a_im * dt)[:, None, :]
    pw_re = mag * jnp.cos(ang)
    pw_im = mag * jnp.sin(ang)
    ab_re, ab_im = pw_re[:, 1], pw_im[:, 1]
    den = a_re * a_re + a_im * a_im
    q_re = ((ab_re - 1.0) * a_re + ab_im * a_im) / den
    q_im = (ab_im * a_re - (ab_re - 1.0) * a_im) / den
    bb_re = q_re[:, :, None] * b_re - q_im[:, :, None] * b_im
    bb_im = q_re[:, :, None] * b_im + q_im[:, :, None] * b_re
    cp_re = c_re[:, None] * pw_re[:, :, None, :] - c_im[:, None] * pw_im[:, :, None, :]
    cp_im = c_re[:, None] * pw_im[:, :, None, :] + c_im[:, None] * pw_re[:, :, None, :]
    contract = lambda c, b: jnp.einsum('gkqn,gnp->gkqp', c[:, :SSM_CHUNK], b, precision=lax.Precision.HIGHEST)
    taps = contract(cp_re, bb_re) - contract(cp_im, bb_im)
    return pw_re, pw_im, bb_re, bb_im, cp_re, cp_im, taps


def _s5_operators(fwd, bwd, ssm_d):
    g, q, p, n = SSM_GROUPS, SSM_CHUNK, SSM_P, SSM_N
    pf_re, pf_im, bf_re, bf_im, cf_re, cf_im, taps_f = _s5_direction(*fwd)
    pb_re, pb_im, bb_re, bb_im, cb_re, cb_im, taps_b = _s5_direction(*bwd)
    s_idx = jnp.arange(q)

    centre = taps_f[:, 0] + taps_b[:, 0] + jnp.eye(p, dtype=F32)[None] * ssm_d.reshape(g, 1, p)
    by_lag = jnp.concatenate([taps_b[:, :0:-1], centre[:, None], taps_f[:, 1:]], axis=1)
    strip = by_lag[:, ::-1].transpose(0, 2, 1, 3).reshape(g, p, (2 * q - 1) * p)
    strip = jnp.pad(strip, ((0, 0), (0, 0), (0, SSM_TAP_LANES - (2 * q - 1) * p)))

    def state_in(pw_re, pw_im, b_re, b_im, powers):
        a_re = pw_re.transpose(0, 2, 1)[:, :, powers][:, :, :, None]
        a_im = pw_im.transpose(0, 2, 1)[:, :, powers][:, :, :, None]
        b_re, b_im = b_re[:, :, None, :], b_im[:, :, None, :]
        return (a_re * b_re - a_im * b_im).reshape(g, n, q * p), (a_re * b_im + a_im * b_re).reshape(g, n, q * p)

    wsf_re, wsf_im = state_in(pf_re, pf_im, bf_re, bf_im, q - 1 - s_idx)
    wsb_re, wsb_im = state_in(pb_re, pb_im, bb_re, bb_im, s_idx)
    wst = jnp.concatenate([wsf_re, wsf_im, wsb_re, wsb_im], axis=1)

    def state_out(cp_re, cp_im, powers):
        o_re = cp_re[:, powers].reshape(g, q * p, n)
        o_im = -cp_im[:, powers].reshape(g, q * p, n)
        return o_re, o_im

    of_re, of_im = state_out(cf_re, cf_im, s_idx + 1)
    ob_re, ob_im = state_out(cb_re, cb_im, q - s_idx)
    wot = jnp.concatenate([of_re, of_im, ob_re, ob_im], axis=2)
    a_chunk = jnp.concatenate([pf_re[:, q], pf_im[:, q], pb_re[:, q], pb_im[:, q]], axis=1)
    a_chunk = jnp.broadcast_to(a_chunk[:, :, None], (g, SSM_STATE_ROWS, SSM_NCHUNK))
    return wst.astype(BF16), strip, wot.astype(BF16), a_chunk


def _gelu_tanh(x):
    return 0.5 * x * (1.0 + jnp.tanh(math.sqrt(2.0 / math.pi) * (x + 0.044715 * (x * x * x))))


def _route_tile(logits, run_ref, route_ref, gate_ref, counts_ref):
    tile = logits.shape[0]
    neg = -jnp.inf
    lt = logits.T
    row8 = lax.broadcasted_iota(jnp.int32, (EXPERTS_PER_GROUP, tile), 0)
    gl = jnp.where(row8 < N_GROUPS, lt[N_EXPERTS:N_EXPERTS + 8], neg)
    gmax = jnp.max(gl, axis=0, keepdims=True)
    g_sel = jnp.min(jnp.where(gl == gmax, row8, 8), axis=0, keepdims=True)
    g_p = 1.0 / jnp.sum(jnp.exp(gl - gmax), axis=0, keepdims=True)
    in_grp = lt[0:EXPERTS_PER_GROUP]
    for g in range(1, N_GROUPS):
        in_grp = jnp.where(g_sel == g, lt[g * EXPERTS_PER_GROUP:(g + 1) * EXPERTS_PER_GROUP], in_grp)
    m1 = jnp.max(in_grp, axis=0, keepdims=True)
    i1 = jnp.min(jnp.where(in_grp == m1, row8, 8), axis=0, keepdims=True)
    rest = jnp.where(row8 == i1, neg, in_grp)
    m2 = jnp.max(rest, axis=0, keepdims=True)
    i2 = jnp.min(jnp.where(rest == m2, row8, 8), axis=0, keepdims=True)
    e21 = jnp.exp(m2 - m1)
    p1 = 1.0 / (1.0 + e21)
    eid1 = g_sel * EXPERTS_PER_GROUP + i1
    eid2 = g_sel * EXPERTS_PER_GROUP + i2

    row_e = lax.broadcasted_iota(jnp.int32, (N_EXPERTS, tile), 0)
    oh1 = (row_e == eid1).astype(F32)
    oh2 = (row_e == eid2).astype(F32)
    earlier = (lax.broadcasted_iota(jnp.int32, (tile, tile), 0)
               < lax.broadcasted_iota(jnp.int32, (tile, tile), 1)).astype(BF16)
    before1 = jnp.dot(oh1.astype(BF16), earlier, preferred_element_type=F32)
    before2 = jnp.dot(oh2.astype(BF16), earlier, preferred_element_type=F32)
    tot1 = jnp.sum(oh1, axis=1, keepdims=True)
    tot2 = jnp.sum(oh2, axis=1, keepdims=True)
    run = run_ref[:, 0:1]
    rank1 = jnp.sum(oh1 * (before1 + run), axis=0, keepdims=True)
    rank2 = jnp.sum(oh2 * (before2 + (run + tot1)), axis=0, keepdims=True)
    new_run = jnp.broadcast_to(run + tot1 + tot2, run_ref.shape)
    run_ref[...] = new_run
    counts_ref[...] = new_run
    zi = jnp.zeros((ROUTE_ROWS - 4, tile), jnp.int32)
    route_ref[0] = jnp.concatenate([eid1, eid2, rank1.astype(jnp.int32), rank2.astype(jnp.int32), zi], axis=0)
    zf = jnp.zeros((ROUTE_ROWS - 2, tile), F32)
    gate_ref[0] = jnp.concatenate([g_p * p1, g_p * (e21 * p1), zf], axis=0)


def _mix_kernel(xp_ref, xs_ref, mod_ref, g1_ref, g2_ref, ua_ref, ua_prev_ref, ua_next_ref, ys_ref,
                wg_ref, wpool_ref, pscale_ref, wpa_ref, wglu_ref, bglu_ref, wpb_ref, wout_ref,
                wr_ref, br_ref,
                x1_ref, h2_ref, route_ref, gate_ref, counts_ref, ext_ref, diff_ref, merged_ref, run_ref):
    tile = TILE_MIX
    i = pl.program_id(1)

    @pl.when((pl.program_id(0) == 0) & (i == 0))
    def _():
        run_ref[...] = jnp.zeros_like(run_ref)

    x = _load_x(xp_ref, xs_ref)
    h = _ada_norm(x, g1_ref[...], mod_ref[0, 1:2, :], mod_ref[0, 0:1, :]).astype(BF16)

    first = i == 0
    last = i == pl.num_programs(1) - 1
    ext_ref[pl.ds(0, POOL_HALO), :] = jnp.where(first, 0.0, ua_prev_ref[0].astype(F32))
    ext_ref[pl.ds(POOL_HALO, tile), :] = ua_ref[0].astype(F32)
    ext_ref[pl.ds(POOL_HALO + tile, POOL_HALO), :] = jnp.where(last, 0.0, ua_next_ref[0].astype(F32))
    pos = i * tile + lax.broadcasted_iota(jnp.int32, (tile, 1), 0)
    for k, w in enumerate(POOL_WINDOWS):
        cols = pl.ds(k * POOL_GROUP, POOL_GROUP)
        lo = jnp.maximum(pos - w // 2, 0)
        hi = jnp.minimum(pos + (w - 1 - w // 2), SEQ - 1)
        inv_cnt = 1.0 / (hi - lo + 1).astype(F32)
        acc = ext_ref[pl.ds(POOL_HALO - w // 2, tile), cols]
        for j in range(1, w):
            acc = acc + ext_ref[pl.ds(POOL_HALO - w // 2 + j, tile), cols]
        diff = acc * inv_cnt - ext_ref[pl.ds(POOL_HALO, tile), cols]
        mixed = jnp.dot(diff.astype(BF16), wpool_ref[k], preferred_element_type=F32)
        diff_ref[:, cols] = (mixed * pscale_ref[:, cols]).astype(BF16)

    z = _gelu_tanh(ys_ref[...].reshape(tile, SSM_WIDTH))
    zg = z * jax.nn.sigmoid(jnp.dot(z.astype(BF16), wglu_ref[...], preferred_element_type=F32) + bglu_ref[...])
    zg = jnp.dot(_regroup_matrix(SSM_CHUNK, tile // SSM_CHUNK), zg.astype(BF16),
                 preferred_element_type=F32).astype(BF16)
    pa = diff_ref[...]

    chunk = 1024
    for j in range(D_MODEL // chunk):
        c0 = j * chunk
        g_a = jnp.dot(h, wg_ref[:, pl.ds(c0, chunk)], preferred_element_type=F32)
        y_a = jnp.dot(pa, wpa_ref[:, pl.ds(c0, chunk)], preferred_element_type=F32)
        m = jax.nn.sigmoid(g_a) * y_a
        g_b = jnp.dot(h, wg_ref[:, pl.ds(D_MODEL + c0, chunk)], preferred_element_type=F32)
        y_b = jnp.dot(zg, wpb_ref[:, pl.ds(c0, chunk)], preferred_element_type=F32)
        m = m + jax.nn.sigmoid(g_b) * y_b
        merged_ref[:, pl.ds(c0, chunk)] = m.astype(BF16)

    x1 = x + mod_ref[0, 2:3, :] * jnp.dot(merged_ref[...], wout_ref[...], preferred_element_type=F32)
    x1_ref[0] = x1
    h2 = _ada_norm(x1, g2_ref[...], mod_ref[0, 4:5, :], mod_ref[0, 3:4, :])
    h2_ref[0] = h2
    h2_hi = h2.astype(BF16)
    h2_lo = (h2 - h2_hi.astype(F32)).astype(BF16)
    logits = (jnp.dot(h2_hi, wr_ref[0], preferred_element_type=F32)
              + (jnp.dot(h2_lo, wr_ref[0], preferred_element_type=F32)
                 + jnp.dot(h2_hi, wr_ref[1], preferred_element_type=F32))) + br_ref[...]
    _route_tile(logits, run_ref, route_ref, gate_ref, counts_ref)


def _mix_call(x_prompt, x_sample, mod, norm1_g, norm2_g, u_a, y_s5, w_g, w_pool, pool_scale, w_pa,
              w_glu, b_glu, w_pb, w_out, w_r, b_r):
    tile = TILE_MIX
    xp, xs = _x_specs(tile)
    halo_per_tile = tile // POOL_HALO
    n_halo = SEQ // POOL_HALO
    seq_tile = lambda width: pl.BlockSpec((1, tile, width), lambda b, i: (b, i, 0))
    return pl.pallas_call(
        _mix_kernel,
        grid=(N_SEQ, SEQ // tile),
        in_specs=[xp, xs,
                  pl.BlockSpec((1, N_MOD, D_MODEL), lambda b, i: (b, 0, 0)),
                  _resident((1, D_MODEL)), _resident((1, D_MODEL)),
                  seq_tile(POOL_WIDTH),
                  pl.BlockSpec((1, POOL_HALO, POOL_WIDTH),
                               lambda b, i: (b, jnp.maximum(i * halo_per_tile - 1, 0), 0)),
                  pl.BlockSpec((1, POOL_HALO, POOL_WIDTH),
                               lambda b, i: (b, jnp.minimum((i + 1) * halo_per_tile, n_halo - 1), 0)),
                  pl.BlockSpec((None, SSM_CHUNK, tile // SSM_CHUNK, SSM_WIDTH), lambda b, i: (b, 0, i, 0)),
                  _resident(w_g.shape), _resident(w_pool.shape), _resident(pool_scale.shape),
                  _resident(w_pa.shape), _resident(w_glu.shape), _resident(b_glu.shape),
                  _resident(w_pb.shape), _resident(w_out.shape), _resident(w_r.shape), _resident(b_r.shape)],
        out_specs=[seq_tile(D_MODEL), seq_tile(D_MODEL),
                   pl.BlockSpec((1, ROUTE_ROWS, tile), lambda b, i: (b * TILES_PER_SEQ + i, 0, 0)),
                   pl.BlockSpec((1, ROUTE_ROWS, tile), lambda b, i: (b * TILES_PER_SEQ + i, 0, 0)),
                   pl.BlockSpec((N_EXPERTS, 128), lambda b, i: (0, 0))],
        out_shape=[jax.ShapeDtypeStruct((N_SEQ, SEQ, D_MODEL), F32),
                   jax.ShapeDtypeStruct((N_SEQ, SEQ, D_MODEL), F32),
                   jax.ShapeDtypeStruct((N_TILES, ROUTE_ROWS, tile), jnp.int32),
                   jax.ShapeDtypeStruct((N_TILES, ROUTE_ROWS, tile), F32),
                   jax.ShapeDtypeStruct((N_EXPERTS, 128), F32)],
        scratch_shapes=[pltpu.VMEM((tile + 2 * POOL_HALO, POOL_WIDTH), F32),
                        pltpu.VMEM((tile, POOL_WIDTH), BF16),
                        pltpu.VMEM((tile, D_MODEL), BF16),
                        pltpu.VMEM((N_EXPERTS, 128), F32)],
        compiler_params=_params(("arbitrary", "arbitrary")),
        name="mix",
    )(x_prompt, x_sample, mod, norm1_g, norm2_g, u_a, u_a, u_a, y_s5, w_g, w_pool, pool_scale, w_pa,
      w_glu, b_glu, w_pb, w_out, w_r, b_r)


def _plan_kernel(cnt_ref, route_ref, counts_ref, dest_ref, vblock_ref, vexpert_ref, vlo_ref, vnext_ref):
    below = (lax.broadcasted_iota(jnp.int32, (N_EXPERTS, N_EXPERTS), 1)
             < lax.broadcasted_iota(jnp.int32, (N_EXPERTS, N_EXPERTS), 0)).astype(F32)
    starts = jnp.dot(below, counts_ref[...], preferred_element_type=F32, precision=lax.Precision.HIGHEST)
    starts_b = jnp.broadcast_to(starts[:, 0:1], (N_EXPERTS, TILE_MIX))
    row_e = lax.broadcasted_iota(jnp.int32, (N_EXPERTS, TILE_MIX), 0)
    zi = jnp.zeros((ROUTE_ROWS - 2, TILE_MIX), jnp.int32)

    def per_tile(t, carry):
        r = route_ref[t]
        s1 = jnp.sum(jnp.where(row_e == r[0:1], starts_b, 0.0), axis=0, keepdims=True)
        s2 = jnp.sum(jnp.where(row_e == r[1:2], starts_b, 0.0), axis=0, keepdims=True)
        dest_ref[t] = jnp.concatenate([s1.astype(jnp.int32) + r[2:3], s2.astype(jnp.int32) + r[3:4], zi], axis=0)
        return carry

    lax.fori_loop(0, N_TILES, per_tile, 0)

    def per_expert(e, carry):
        v, start, last_e = carry
        cnt = cnt_ref[e]
        end = start + cnt
        first = lax.shift_right_logical(start, MOE_SHIFT)
        n_blk = jnp.where(cnt > 0, lax.shift_right_logical(end - 1, MOE_SHIFT) - first + 1, 0)

        def per_block(k, v):
            blk = first + k
            vblock_ref[v] = blk
            vexpert_ref[v] = e
            vlo_ref[v] = jnp.maximum(start - blk * MOE_ROWS, 0)
            return v + 1

        v = lax.fori_loop(0, n_blk, per_block, v)
        return v, end, jnp.where(cnt > 0, e, last_e)

    v, _, last_e = lax.fori_loop(0, N_EXPERTS, per_expert, (0, 0, 0))

    def idle(k, carry):
        vblock_ref[k] = MOE_BLOCKS - 1
        vexpert_ref[k] = last_e
        vlo_ref[k] = MOE_ROWS
        return carry

    lax.fori_loop(v, N_VISITS, idle, 0)

    def following(i, carry):
        nxt, later = carry
        k = N_VISITS - 1 - i
        e = vexpert_ref[k]
        nxt = jnp.where(e != later, later, nxt)
        vnext_ref[k] = nxt
        return nxt, e

    lax.fori_loop(0, N_VISITS, following, (-1, -1))


def _plan_call(cnt, route, counts):
    smem = pl.BlockSpec(memory_space=pltpu.SMEM)
    vmem = pl.BlockSpec(memory_space=pltpu.VMEM)
    visits = jax.ShapeDtypeStruct((N_VISITS,), jnp.int32)
    return pl.pallas_call(
        _plan_kernel,
        in_specs=[smem, vmem, vmem],
        out_specs=[vmem, smem, smem, smem, smem],
        out_shape=[jax.ShapeDtypeStruct((N_TILES, ROUTE_ROWS, TILE_MIX), jnp.int32), visits, visits, visits, visits],
        name="plan",
    )(cnt, route, counts)


def _dest_spec(index_map):
    return pl.BlockSpec((1, 1, TOPK * TILE_MIX), index_map, memory_space=pltpu.SMEM)


def _dispatch_kernel(dest_ref, h_ref, xs_ref, sem):
    tile = TILE_MIX

    def row(r, carry):
        for k in range(TOPK):
            d = dest_ref[0, 0, k * tile + r]
            pltpu.make_async_copy(h_ref.at[pl.ds(r, 1)], xs_ref.at[pl.ds(d, 1)], sem).start()
        return carry

    lax.fori_loop(0, tile, row, 0, unroll=8)
    for k in range(TOPK):
        pltpu.make_async_copy(h_ref, xs_ref.at[pl.ds(0, tile)], sem).wait()


def _dispatch_call(dest, h2):
    return pl.pallas_call(
        _dispatch_kernel,
        grid=(N_TILES,),
        in_specs=[_dest_spec(lambda t: (t, 0, 0)),
                  pl.BlockSpec((TILE_MIX, D_MODEL), lambda t: (t, 0))],
        out_specs=pl.BlockSpec(memory_space=pl.ANY),
        out_shape=jax.ShapeDtypeStruct((N_ASSIGN, D_MODEL), F32),
        scratch_shapes=[pltpu.SemaphoreType.DMA],
        compiler_params=_params(("arbitrary",)),
        name="dispatch",
    )(dest, h2)


def _expert_kernel(vblock_ref, vexpert_ref, vlo_ref, vnext_ref, x_ref, wg_ref, wu_ref, wd_ref, o_ref,
                   wgu_s, wd_s, gbuf, ubuf, dbuf, run_ref, sem):
    v = pl.program_id(0)
    e = vexpert_ref[v]

    def weight_copies(expert, slot):
        return [pltpu.make_async_copy(src.at[expert], dst.at[slot], sem.at[slot])
                for src, dst in ((wg_ref, gbuf), (wu_ref, ubuf), (wd_ref, dbuf))]

    @pl.when(v == 0)
    def _():
        run_ref[0] = 0
        for c in weight_copies(e, 0):
            c.start()

    @pl.when((v == 0) | (e != vexpert_ref[jnp.maximum(v - 1, 0)]))
    def _():
        slot = run_ref[0] & 1
        run_ref[0] = run_ref[0] + 1
        for c in weight_copies(e, slot):
            c.wait()
        nxt = vnext_ref[v]

        @pl.when(nxt >= 0)
        def _():
            for c in weight_copies(nxt, 1 - slot):
                c.start()

        wgu_s[:, :D_EXPERT] = gbuf[slot].astype(BF16)
        wgu_s[:, D_EXPERT:] = ubuf[slot].astype(BF16)
        wd_s[...] = dbuf[slot].astype(BF16)

    lo = vlo_ref[v]

    @pl.when(lo < MOE_ROWS)
    def _():
        gu = jnp.dot(x_ref[...].astype(BF16), wgu_s[...], preferred_element_type=F32)
        g = gu[:, :D_EXPERT]
        act = (g * jax.nn.sigmoid(g)) * gu[:, D_EXPERT:]
        res = jnp.dot(act.astype(BF16), wd_s[...], preferred_element_type=F32)

        @pl.when(lo == 0)
        def _():
            o_ref[...] = res

        @pl.when(lo > 0)
        def _():
            rows = lax.broadcasted_iota(jnp.int32, (MOE_ROWS, 1), 0)
            o_ref[...] = jnp.where(rows >= lo, res, o_ref[...])


def _expert_call(vblock, vexpert, vlo, vnext, x_slots, w_gate, w_up, w_down):
    hbm = pl.BlockSpec(memory_space=pl.ANY)
    grid_spec = pltpu.PrefetchScalarGridSpec(
        num_scalar_prefetch=4,
        grid=(N_VISITS,),
        in_specs=[pl.BlockSpec((MOE_ROWS, D_MODEL), lambda v, vb, ve, vl, vn: (vb[v], 0)), hbm, hbm, hbm],
        out_specs=pl.BlockSpec((MOE_ROWS, D_MODEL), lambda v, vb, ve, vl, vn: (vb[v], 0)),
        scratch_shapes=[pltpu.VMEM((D_MODEL, 2 * D_EXPERT), BF16),
                        pltpu.VMEM((D_EXPERT, D_MODEL), BF16),
                        pltpu.VMEM((2, D_MODEL, D_EXPERT), F32),
                        pltpu.VMEM((2, D_MODEL, D_EXPERT), F32),
                        pltpu.VMEM((2, D_EXPERT, D_MODEL), F32),
                        pltpu.SMEM((1,), jnp.int32),
                        pltpu.SemaphoreType.DMA((2,))],
    )
    return pl.pallas_call(
        _expert_kernel,
        grid_spec=grid_spec,
        out_shape=jax.ShapeDtypeStruct((N_ASSIGN, D_MODEL), F32),
        compiler_params=_params(("arbitrary",)),
        name="experts",
    )(vblock, vexpert, vlo, vnext, x_slots, w_gate, w_up, w_down)


def _final_kernel(dest_ref, dest_next_ref, x1_ref, mod_ref, gate_ref, g_ref, y_ref, o_ref, rows_ref, sem):
    tile = TILE_MIX
    step = pl.program_id(0) * pl.num_programs(1) + pl.program_id(1)
    n_steps = pl.num_programs(0) * pl.num_programs(1)
    slot = lax.rem(step, 2)

    def fetch(dst_ref, into):
        def row(r, carry):
            for k in range(TOPK):
                d = dst_ref[0, 0, k * tile + r]
                pltpu.make_async_copy(y_ref.at[pl.ds(d, 1)], rows_ref.at[into, k, pl.ds(r, 1)], sem.at[into]).start()
            return carry

        lax.fori_loop(0, tile, row, 0, unroll=8)

    @pl.when(step == 0)
    def _():
        fetch(dest_ref, 0)

    @pl.when(step + 1 < n_steps)
    def _():
        fetch(dest_next_ref, 1 - slot)

    for k in range(TOPK):
        pltpu.make_async_copy(y_ref.at[pl.ds(0, tile)], rows_ref.at[slot, k], sem.at[slot]).wait()
    gate = gate_ref[0].T
    moe = gate[:, 0:1] * rows_ref[slot, 0] + gate[:, 1:2] * rows_ref[slot, 1]
    x2 = x1_ref[0] + mod_ref[0, 5:6, :] * moe
    r = lax.rsqrt(jnp.mean(x2 * x2, axis=-1, keepdims=True) + EPS)
    o_ref[0] = x2 * r * g_ref[...]


def _final_call(x1, mod, y_slots, dest, gates, final_g, first_seq, n_seq):
    tile = TILE_MIX
    tile_of = lambda b, i: (b + first_seq) * TILES_PER_SEQ + i
    last_tile = (first_seq + n_seq) * TILES_PER_SEQ - 1
    return pl.pallas_call(
        _final_kernel,
        grid=(n_seq, TILES_PER_SEQ),
        in_specs=[_dest_spec(lambda b, i: (tile_of(b, i), 0, 0)),
                  _dest_spec(lambda b, i: (jnp.minimum(tile_of(b, i) + 1, last_tile), 0, 0)),
                  pl.BlockSpec((1, tile, D_MODEL), lambda b, i: (b + first_seq, i, 0)),
                  pl.BlockSpec((1, N_MOD, D_MODEL), lambda b, i: (b + first_seq, 0, 0)),
                  pl.BlockSpec((1, ROUTE_ROWS, tile), lambda b, i: (tile_of(b, i), 0, 0)),
                  _resident((1, D_MODEL)),
                  pl.BlockSpec(memory_space=pl.ANY)],
        out_specs=pl.BlockSpec((1, tile, D_MODEL), lambda b, i: (b, i, 0)),
        out_shape=jax.ShapeDtypeStruct((n_seq, SEQ, D_MODEL), F32),
        scratch_shapes=[pltpu.VMEM((2, TOPK, tile, D_MODEL), F32), pltpu.SemaphoreType.DMA((2,))],
        compiler_params=_params(("arbitrary", "arbitrary")),
        name="final",
    )(dest, dest, x1, mod, gates, final_g, y_slots)


def kernel(x_prompt, x_sample, c_prompt, c_sample, w_ada, b_ada, norm1_g, w_in, w_pool, pool_scale,
           ssm_a_re_f, ssm_a_im_f, ssm_log_dt_f, ssm_b_re_f, ssm_b_im_f, ssm_c_re_f, ssm_c_im_f,
           ssm_a_re_b, ssm_a_im_b, ssm_log_dt_b, ssm_b_re_b, ssm_b_im_b, ssm_c_re_b, ssm_c_im_b,
           ssm_d, w_glu, b_glu, w_proj_a, w_proj_b, w_out, norm2_g,
           w_grp, b_grp, w_router, b_router, w_exp_gate, w_exp_up, w_exp_down, final_g):
    n_u = POOL_WIDTH + SSM_WIDTH
    c_pad = jnp.concatenate([c_prompt, c_sample, jnp.zeros((16 - N_SEQ, D_MODEL), F32)], axis=0)
    mod = _mod_call(c_pad, w_ada[0], b_ada).reshape(16, N_MOD, D_MODEL)

    w_in_bf = w_in[0].astype(BF16)
    u_a, u_b = _inproj_call(x_prompt, x_sample, mod, norm1_g, w_in_bf[:, :n_u])

    fwd = (ssm_a_re_f[0], ssm_a_im_f[0], ssm_log_dt_f[0], ssm_b_re_f[0], ssm_b_im_f[0], ssm_c_re_f[0], ssm_c_im_f[0])
    bwd = (ssm_a_re_b[0], ssm_a_im_b[0], ssm_log_dt_b[0], ssm_b_re_b[0], ssm_b_im_b[0], ssm_c_re_b[0], ssm_c_im_b[0])
    y_s5 = _s5_call(u_b, *_s5_operators(fwd, bwd, ssm_d[0]))

    w_r = jnp.concatenate([w_router[0], w_grp[0],
                           jnp.zeros((D_MODEL, ROUTER_COLS - N_GROUPS - N_EXPERTS), F32)], axis=1)
    b_r = jnp.concatenate([b_router[0], b_grp[0],
                           jnp.zeros((ROUTER_COLS - N_GROUPS - N_EXPERTS,), F32)])[None, :]
    w_r_hi = w_r.astype(BF16)
    w_r = jnp.stack([w_r_hi, (w_r - w_r_hi.astype(F32)).astype(BF16)])
    x1, h2, route, gates, counts = _mix_call(
        x_prompt, x_sample, mod, norm1_g, norm2_g, u_a, y_s5, w_in_bf[:, n_u:], w_pool[0].astype(BF16),
        pool_scale, w_proj_a[0].astype(BF16), w_glu[0].astype(BF16), b_glu, w_proj_b[0].astype(BF16),
        w_out[0].astype(BF16), w_r, b_r)

    dest8, vblock, vexpert, vlo, vnext = _plan_call(counts[:, 0].astype(jnp.int32), route, counts)
    dest = dest8[:, :TOPK, :].reshape(N_TILES, 1, TOPK * TILE_MIX)
    x_slots = _dispatch_call(dest, h2.reshape(N_TOK, D_MODEL))
    y_slots = _expert_call(vblock, vexpert, vlo, vnext, x_slots, w_exp_gate[0], w_exp_up[0], w_exp_down[0])

    final_g2 = final_g[None, :]
    y_prompt = _final_call(x1, mod, y_slots, dest, gates, final_g2, 0, N_PROMPT)
    y_sample = _final_call(x1, mod, y_slots, dest, gates, final_g2, N_PROMPT, N_SAMPLE)
    return (y_prompt, y_sample)
```

```python
import math

import jax
import jax.numpy as jnp
from jax import lax
from jax.experimental import pallas as pl
from jax.experimental.pallas import tpu as pltpu

F32 = jnp.float32
BF16 = jnp.bfloat16

D_MODEL = 2048
SEQ = 4096
N_PROMPT = 2
N_SAMPLE = 8
N_SEQ = N_PROMPT + N_SAMPLE
N_TOK = N_SEQ * SEQ
EPS = 1e-6
N_MOD = 6

POOL_WINDOWS = (2, 4, 8, 16)
POOL_GROUP = 256
POOL_WIDTH = 1024
POOL_HALO = 16

SSM_GROUPS = 32
SSM_P = 16
SSM_N = 64
SSM_WIDTH = 512
SSM_CHUNK = 32
SSM_NCHUNK = SEQ // SSM_CHUNK
SSM_COLS = SSM_CHUNK * SSM_P
SSM_STATE_ROWS = 4 * SSM_N
SSM_GROUPS_PER_STEP = 8
SSM_SCAN_STEPS = 7
SSM_TAP_LANES = 2 * SSM_COLS

N_GROUPS = 4
EXPERTS_PER_GROUP = 8
N_EXPERTS = 32
TOPK = 2
D_EXPERT = 512
N_ASSIGN = N_TOK * TOPK
ROUTER_COLS = 128

TILE_INPROJ = 512
TILE_MIX = 256
N_TILES = N_TOK // TILE_MIX
TILES_PER_SEQ = SEQ // TILE_MIX
ROUTE_ROWS = 8
D_PACKED = D_MODEL // 2
MOE_ROWS = 256
MOE_SHIFT = 8
MOE_BLOCKS = N_ASSIGN // MOE_ROWS
N_VISITS = MOE_BLOCKS + N_EXPERTS - 1

VMEM_LIMIT = 60 * 1024 * 1024


def _params(sem, vmem=VMEM_LIMIT):
    return pltpu.CompilerParams(dimension_semantics=sem, vmem_limit_bytes=vmem)


def _resident(shape):
    zeros = (0,) * len(shape)
    return pl.BlockSpec(shape, lambda *_: zeros, pipeline_mode=pl.Buffered(1))


def _ada_norm(x, gain, scale, shift):
    r = lax.rsqrt(jnp.mean(x * x, axis=-1, keepdims=True) + EPS)
    return (x * r * gain) * (1.0 + scale) + shift


def _regroup_matrix(n_outer, n_inner):
    n = n_outer * n_inner
    dst = lax.broadcasted_iota(jnp.int32, (n, n), 0)
    src = lax.broadcasted_iota(jnp.int32, (n, n), 1)
    shift = n_outer.bit_length() - 1
    return (src == (dst & (n_outer - 1)) * n_inner + lax.shift_right_logical(dst, shift)).astype(BF16)


def _pack_bf16_pairs(x):
    w = x.shape[1] // 2
    lo = lax.bitcast_convert_type(x[:, :w].astype(BF16).astype(F32), jnp.uint32)
    hi = lax.bitcast_convert_type(x[:, w:].astype(BF16).astype(F32), jnp.uint32)
    return hi | (lo >> 16)


def _unpack_bf16_pairs(p):
    lo = lax.bitcast_convert_type(p << 16, F32)
    hi = lax.bitcast_convert_type(p & jnp.uint32(0xFFFF0000), F32)
    return lo, hi


def _mod_kernel(c_ref, w_ref, b_ref, o_ref):
    c = c_ref[...]
    s = c * jax.nn.sigmoid(c)
    o_ref[...] = jnp.dot(s.astype(BF16), w_ref[...].astype(BF16), preferred_element_type=F32) + b_ref[...]


def _mod_call(c_pad, w_ada, b_ada):
    n = w_ada.shape[1]
    tn = 1024
    return pl.pallas_call(
        _mod_kernel,
        grid=(n // tn,),
        in_specs=[pl.BlockSpec(c_pad.shape, lambda j: (0, 0)),
                  pl.BlockSpec((D_MODEL, tn), lambda j: (0, j)),
                  pl.BlockSpec((1, tn), lambda j: (0, j))],
        out_specs=pl.BlockSpec((c_pad.shape[0], tn), lambda j: (0, j)),
        out_shape=jax.ShapeDtypeStruct((c_pad.shape[0], n), F32),
        compiler_params=_params(("arbitrary",)),
        name="mod",
    )(c_pad, w_ada, b_ada)


def _x_specs(tile):
    last = SEQ // tile - 1
    xp = pl.BlockSpec((1, tile, D_MODEL),
                      lambda b, i: (jnp.minimum(b, N_PROMPT - 1), jnp.where(b < N_PROMPT, i, last), 0))
    xs = pl.BlockSpec((1, tile, D_MODEL),
                      lambda b, i: (jnp.maximum(b - N_PROMPT, 0), jnp.where(b < N_PROMPT, 0, i), 0))
    return xp, xs


def _load_x(xp_ref, xs_ref):
    return jnp.where(pl.program_id(0) < N_PROMPT, xp_ref[0], xs_ref[0])


def _inproj_kernel(xp_ref, xs_ref, mod_ref, g_ref, w_ref, ua_ref, ub_ref):
    x = _load_x(xp_ref, xs_ref)
    h = _ada_norm(x, g_ref[...], mod_ref[0, 1:2, :], mod_ref[0, 0:1, :])
    p = jnp.dot(h.astype(BF16), w_ref[...], preferred_element_type=F32)
    ua_ref[0] = p[:, :POOL_WIDTH].astype(BF16)
    n_chunk = TILE_INPROJ // SSM_CHUNK
    ub = jnp.dot(_regroup_matrix(n_chunk, SSM_CHUNK), p[:, POOL_WIDTH:].astype(BF16), preferred_element_type=F32)
    ub_ref[...] = ub.reshape(SSM_CHUNK, n_chunk, SSM_WIDTH).astype(BF16)


def _inproj_call(x_prompt, x_sample, mod, norm_g, w_u):
    tile = TILE_INPROJ
    xp, xs = _x_specs(tile)
    return pl.pallas_call(
        _inproj_kernel,
        grid=(N_SEQ, SEQ // tile),
        in_specs=[xp, xs,
                  pl.BlockSpec((1, N_MOD, D_MODEL), lambda b, i: (b, 0, 0)),
                  _resident((1, D_MODEL)),
                  _resident(w_u.shape)],
        out_specs=[pl.BlockSpec((1, tile, POOL_WIDTH), lambda b, i: (b, i, 0)),
                   pl.BlockSpec((None, SSM_CHUNK, tile // SSM_CHUNK, SSM_WIDTH), lambda b, i: (b, 0, i, 0))],
        out_shape=[jax.ShapeDtypeStruct((N_SEQ, SEQ, POOL_WIDTH), BF16),
                   jax.ShapeDtypeStruct((N_SEQ, SSM_CHUNK, SSM_NCHUNK, SSM_WIDTH), BF16)],
        compiler_params=_params(("arbitrary", "arbitrary")),
        name="inproj",
    )(x_prompt, x_sample, mod, norm_g, w_u)


def _s5_kernel(u_ref, wst_ref, taps_ref, wot_ref, a_ref, y_ref, ut_ref, yt_ref, apow_ref, tt_ref):
    n, nc, ng = SSM_N, SSM_NCHUNK, SSM_GROUPS_PER_STEP

    @pl.when(pl.program_id(1) == 0)
    def _():
        for g in range(ng):
            a = a_ref[g]
            for k in range(SSM_SCAN_STEPS):
                apow_ref[g, k] = a
                f_re, f_im, b_re, b_im = a[0:n], a[n:2 * n], a[2 * n:3 * n], a[3 * n:4 * n]
                a = jnp.concatenate([f_re * f_re - f_im * f_im, 2.0 * (f_re * f_im),
                                     b_re * b_re - b_im * b_im, 2.0 * (b_re * b_im)], axis=0)
            strip = taps_ref[g]
            for t in range(SSM_CHUNK):
                start = (SSM_CHUNK - 1 - t) * SSM_P
                window = pltpu.roll(strip, SSM_TAP_LANES - start, 1) if start else strip
                tt_ref[g, pl.ds(t * SSM_P, SSM_P), :] = window[:, :SSM_COLS].astype(BF16)

    for s in range(SSM_CHUNK):
        blk = u_ref[s].astype(F32).T
        ut_ref[:, pl.ds(s * SSM_P, SSM_P), :] = blk.reshape(ng, SSM_P, nc).astype(BF16)

    lane = lax.broadcasted_iota(jnp.int32, (n, nc), 1)

    def shifted(x, d, forward):
        if forward:
            return jnp.where(lane >= d, pltpu.roll(x, d, 1), 0.0)
        return jnp.where(lane < nc - d, pltpu.roll(x, nc - d, 1), 0.0)

    def scan(x_re, x_im, g, re_row, im_row, forward):
        for k in range(SSM_SCAN_STEPS):
            a_re = apow_ref[g, k, pl.ds(re_row, n), :]
            a_im = apow_ref[g, k, pl.ds(im_row, n), :]
            s_re = shifted(x_re, 2 ** k, forward)
            s_im = shifted(x_im, 2 ** k, forward)
            x_re, x_im = x_re + (a_re * s_re - a_im * s_im), x_im + (a_re * s_im + a_im * s_re)
        return x_re, x_im

    def per_group(g, carry):
        ut = ut_ref[g]
        st = jnp.dot(wst_ref[g], ut, preferred_element_type=F32)
        f_re, f_im = scan(st[0:n], st[n:2 * n], g, 0, n, True)
        b_re, b_im = scan(st[2 * n:3 * n], st[3 * n:4 * n], g, 2 * n, 3 * n, False)
        carried = jnp.concatenate([shifted(f_re, 1, True), shifted(f_im, 1, True),
                                   shifted(b_re, 1, False), shifted(b_im, 1, False)], axis=0)
        yt_ref[g] = (jnp.dot(tt_ref[g], ut, preferred_element_type=F32)
                     + jnp.dot(wot_ref[g], carried.astype(BF16), preferred_element_type=F32))
        return carry

    lax.fori_loop(0, ng, per_group, 0, unroll=2)
    for t in range(SSM_CHUNK):
        y_ref[t] = yt_ref[:, pl.ds(t * SSM_P, SSM_P), :].reshape(ng * SSM_P, nc).T


def _s5_call(u_ph, wst, tap_strip, wot, a_chunk):
    ng = SSM_GROUPS_PER_STEP
    seq = pl.BlockSpec((None, SSM_CHUNK, SSM_NCHUNK, ng * SSM_P), lambda q, b: (b, 0, 0, q))
    mat = lambda rows, cols: pl.BlockSpec((ng, rows, cols), lambda q, b: (q, 0, 0))
    return pl.pallas_call(
        _s5_kernel,
        grid=(SSM_GROUPS // ng, N_SEQ),
        in_specs=[seq, mat(SSM_STATE_ROWS, SSM_COLS), mat(SSM_P, SSM_TAP_LANES), mat(SSM_COLS, SSM_STATE_ROWS),
                  mat(SSM_STATE_ROWS, SSM_NCHUNK)],
        out_specs=seq,
        out_shape=jax.ShapeDtypeStruct((N_SEQ, SSM_CHUNK, SSM_NCHUNK, SSM_WIDTH), F32),
        scratch_shapes=[pltpu.VMEM((ng, SSM_COLS, SSM_NCHUNK), BF16),
                        pltpu.VMEM((ng, SSM_COLS, SSM_NCHUNK), F32),
                        pltpu.VMEM((ng, SSM_SCAN_STEPS, SSM_STATE_ROWS, SSM_NCHUNK), F32),
                        pltpu.VMEM((ng, SSM_COLS, SSM_COLS), BF16)],
        compiler_params=_params(("arbitrary", "arbitrary")),
        name="s5",
    )(u_ph, wst, tap_strip, wot, a_chunk)


def _s5_direction(a_re, a_im, log_dt, b_re, b_im, c_re, c_im):
    dt = jnp.exp(log_dt)[:, None]
    k = jnp.arange(SSM_CHUNK + 1, dtype=F32)[None, :, None]
    mag = jnp.exp(k * (a_re * dt)[:, None, :])
    ang = k * (a_im * dt)[:, None, :]
    pw_re = mag * jnp.cos(ang)
    pw_im = mag * jnp.sin(ang)
    ab_re, ab_im = pw_re[:, 1], pw_im[:, 1]
    den = a_re * a_re + a_im * a_im
    q_re = ((ab_re - 1.0) * a_re + ab_im * a_im) / den
    q_im = (ab_im * a_re - (ab_re - 1.0) * a_im) / den
    bb_re = q_re[:, :, None] * b_re - q_im[:, :, None] * b_im
    bb_im = q_re[:, :, None] * b_im + q_im[:, :, None] * b_re
    cp_re = c_re[:, None] * pw_re[:, :, None, :] - c_im[:, None] * pw_im[:, :, None, :]
    cp_im = c_re[:, None] * pw_im[:, :, None, :] + c_im[:, None] * pw_re[:, :, None, :]
    contract = lambda c, b: jnp.einsum('gkqn,gnp->gkqp', c[:, :SSM_CHUNK], b, precision=lax.Precision.HIGHEST)
    taps = contract(cp_re, bb_re) - contract(cp_im, bb_im)
    return pw_re, pw_im, bb_re, bb_im, cp_re, cp_im, taps


def _s5_operators(fwd, bwd, ssm_d):
    g, q, p, n = SSM_GROUPS, SSM_CHUNK, SSM_P, SSM_N
    pf_re, pf_im, bf_re, bf_im, cf_re, cf_im, taps_f = _s5_direction(*fwd)
    pb_re, pb_im, bb_re, bb_im, cb_re, cb_im, taps_b = _s5_direction(*bwd)
    descending, ascending = slice(q - 1, None, -1), slice(0, q)
    from_one, down_to_one = slice(1, q + 1), slice(q, 0, -1)

    centre = taps_f[:, 0] + taps_b[:, 0] + jnp.eye(p, dtype=F32)[None] * ssm_d.reshape(g, 1, p)
    by_lag = jnp.concatenate([taps_b[:, :0:-1], centre[:, None], taps_f[:, 1:]], axis=1)
    strip = by_lag[:, ::-1].transpose(0, 2, 1, 3).reshape(g, p, (2 * q - 1) * p)
    strip = jnp.pad(strip, ((0, 0), (0, 0), (0, SSM_TAP_LANES - (2 * q - 1) * p)))

    def state_in(pw_re, pw_im, b_re, b_im, powers):
        a_re = pw_re[:, powers].transpose(0, 2, 1)[:, :, :, None]
        a_im = pw_im[:, powers].transpose(0, 2, 1)[:, :, :, None]
        b_re, b_im = b_re[:, :, None, :], b_im[:, :, None, :]
        return (a_re * b_re - a_im * b_im).reshape(g, n, q * p), (a_re * b_im + a_im * b_re).reshape(g, n, q * p)

    wsf_re, wsf_im = state_in(pf_re[:, :q], pf_im[:, :q], bf_re, bf_im, descending)
    wsb_re, wsb_im = state_in(pb_re, pb_im, bb_re, bb_im, ascending)
    wst = jnp.concatenate([wsf_re, wsf_im, wsb_re, wsb_im], axis=1)

    def state_out(cp_re, cp_im, powers):
        o_re = cp_re[:, powers].reshape(g, q * p, n)
        o_im = -cp_im[:, powers].reshape(g, q * p, n)
        return o_re, o_im

    of_re, of_im = state_out(cf_re, cf_im, from_one)
    ob_re, ob_im = state_out(cb_re, cb_im, down_to_one)
    wot = jnp.concatenate([of_re, of_im, ob_re, ob_im], axis=2)
    a_chunk = jnp.concatenate([pf_re[:, q], pf_im[:, q], pb_re[:, q], pb_im[:, q]], axis=1)
    a_chunk = jnp.broadcast_to(a_chunk[:, :, None], (g, SSM_STATE_ROWS, SSM_NCHUNK))
    return wst.astype(BF16), strip, wot.astype(BF16), a_chunk


def _gelu_tanh(x):
    return 0.5 * x * (1.0 + jnp.tanh(math.sqrt(2.0 / math.pi) * (x + 0.044715 * (x * x * x))))


def _route_tile(logits, run_ref, route_ref, gate_ref, counts_ref):
    tile = logits.shape[0]
    neg = -jnp.inf
    lt = logits.T
    row8 = lax.broadcasted_iota(jnp.int32, (EXPERTS_PER_GROUP, tile), 0)
    gl = jnp.where(row8 < N_GROUPS, lt[N_EXPERTS:N_EXPERTS + 8], neg)
    gmax = jnp.max(gl, axis=0, keepdims=True)
    g_sel = jnp.min(jnp.where(gl == gmax, row8, 8), axis=0, keepdims=True)
    g_p = 1.0 / jnp.sum(jnp.exp(gl - gmax), axis=0, keepdims=True)
    in_grp = lt[0:EXPERTS_PER_GROUP]
    for g in range(1, N_GROUPS):
        in_grp = jnp.where(g_sel == g, lt[g * EXPERTS_PER_GROUP:(g + 1) * EXPERTS_PER_GROUP], in_grp)
    m1 = jnp.max(in_grp, axis=0, keepdims=True)
    i1 = jnp.min(jnp.where(in_grp == m1, row8, 8), axis=0, keepdims=True)
    rest = jnp.where(row8 == i1, neg, in_grp)
    m2 = jnp.max(rest, axis=0, keepdims=True)
    i2 = jnp.min(jnp.where(rest == m2, row8, 8), axis=0, keepdims=True)
    e21 = jnp.exp(m2 - m1)
    p1 = 1.0 / (1.0 + e21)
    eid1 = g_sel * EXPERTS_PER_GROUP + i1
    eid2 = g_sel * EXPERTS_PER_GROUP + i2

    row_e = lax.broadcasted_iota(jnp.int32, (N_EXPERTS, tile), 0)
    oh1 = (row_e == eid1).astype(F32)
    oh2 = (row_e == eid2).astype(F32)
    earlier = (lax.broadcasted_iota(jnp.int32, (tile, tile), 0)
               < lax.broadcasted_iota(jnp.int32, (tile, tile), 1)).astype(BF16)
    before1 = jnp.dot(oh1.astype(BF16), earlier, preferred_element_type=F32)
    before2 = jnp.dot(oh2.astype(BF16), earlier, preferred_element_type=F32)
    tot1 = jnp.sum(oh1, axis=1, keepdims=True)
    tot2 = jnp.sum(oh2, axis=1, keepdims=True)
    run = run_ref[:, 0:1]
    rank1 = jnp.sum(oh1 * (before1 + run), axis=0, keepdims=True)
    rank2 = jnp.sum(oh2 * (before2 + (run + tot1)), axis=0, keepdims=True)
    new_run = jnp.broadcast_to(run + tot1 + tot2, run_ref.shape)
    run_ref[...] = new_run
    counts_ref[...] = new_run
    zi = jnp.zeros((ROUTE_ROWS - 4, tile), jnp.int32)
    route_ref[0] = jnp.concatenate([eid1, eid2, rank1.astype(jnp.int32), rank2.astype(jnp.int32), zi], axis=0)
    zf = jnp.zeros((ROUTE_ROWS - 2, tile), F32)
    gate_ref[0] = jnp.concatenate([g_p * p1, g_p * (e21 * p1), zf], axis=0)


def _mix_kernel(xp_ref, xs_ref, mod_ref, g1_ref, g2_ref, ua_ref, ua_prev_ref, ua_next_ref, ys_ref,
                wg_ref, wpool_ref, pscale_ref, wpa_ref, wglu_ref, bglu_ref, wpb_ref, wout_ref,
                wr_ref, br_ref,
                x1_ref, h2_ref, route_ref, gate_ref, counts_ref, ext_ref, diff_ref, merged_ref, run_ref):
    tile = TILE_MIX
    i = pl.program_id(1)

    @pl.when((pl.program_id(0) == 0) & (i == 0))
    def _():
        run_ref[...] = jnp.zeros_like(run_ref)

    x = _load_x(xp_ref, xs_ref)
    h = _ada_norm(x, g1_ref[...], mod_ref[0, 1:2, :], mod_ref[0, 0:1, :]).astype(BF16)

    first = i == 0
    last = i == pl.num_programs(1) - 1
    ext_ref[pl.ds(0, POOL_HALO), :] = jnp.where(first, 0.0, ua_prev_ref[0].astype(F32))
    ext_ref[pl.ds(POOL_HALO, tile), :] = ua_ref[0].astype(F32)
    ext_ref[pl.ds(POOL_HALO + tile, POOL_HALO), :] = jnp.where(last, 0.0, ua_next_ref[0].astype(F32))
    pos = i * tile + lax.broadcasted_iota(jnp.int32, (tile, 1), 0)
    for k, w in enumerate(POOL_WINDOWS):
        cols = pl.ds(k * POOL_GROUP, POOL_GROUP)
        lo = jnp.maximum(pos - w // 2, 0)
        hi = jnp.minimum(pos + (w - 1 - w // 2), SEQ - 1)
        inv_cnt = 1.0 / (hi - lo + 1).astype(F32)
        acc = ext_ref[pl.ds(POOL_HALO - w // 2, tile), cols]
        for j in range(1, w):
            acc = acc + ext_ref[pl.ds(POOL_HALO - w // 2 + j, tile), cols]
        diff = acc * inv_cnt - ext_ref[pl.ds(POOL_HALO, tile), cols]
        mixed = jnp.dot(diff.astype(BF16), wpool_ref[k], preferred_element_type=F32)
        diff_ref[:, cols] = (mixed * pscale_ref[:, cols]).astype(BF16)

    z = _gelu_tanh(ys_ref[...].reshape(tile, SSM_WIDTH))
    zg = z * jax.nn.sigmoid(jnp.dot(z.astype(BF16), wglu_ref[...], preferred_element_type=F32) + bglu_ref[...])
    zg = jnp.dot(_regroup_matrix(SSM_CHUNK, tile // SSM_CHUNK), zg.astype(BF16),
                 preferred_element_type=F32).astype(BF16)
    pa = diff_ref[...]

    chunk = 1024
    for j in range(D_MODEL // chunk):
        c0 = j * chunk
        g_a = jnp.dot(h, wg_ref[:, pl.ds(c0, chunk)], preferred_element_type=F32)
        y_a = jnp.dot(pa, wpa_ref[:, pl.ds(c0, chunk)], preferred_element_type=F32)
        m = jax.nn.sigmoid(g_a) * y_a
        g_b = jnp.dot(h, wg_ref[:, pl.ds(D_MODEL + c0, chunk)], preferred_element_type=F32)
        y_b = jnp.dot(zg, wpb_ref[:, pl.ds(c0, chunk)], preferred_element_type=F32)
        m = m + jax.nn.sigmoid(g_b) * y_b
        merged_ref[:, pl.ds(c0, chunk)] = m.astype(BF16)

    x1 = x + mod_ref[0, 2:3, :] * jnp.dot(merged_ref[...], wout_ref[...], preferred_element_type=F32)
    x1_ref[0] = x1
    h2 = _ada_norm(x1, g2_ref[...], mod_ref[0, 4:5, :], mod_ref[0, 3:4, :])
    h2_ref[0] = _pack_bf16_pairs(h2)
    h2_hi = h2.astype(BF16)
    h2_lo = (h2 - h2_hi.astype(F32)).astype(BF16)
    logits = (jnp.dot(h2_hi, wr_ref[0], preferred_element_type=F32)
              + (jnp.dot(h2_lo, wr_ref[0], preferred_element_type=F32)
                 + jnp.dot(h2_hi, wr_ref[1], preferred_element_type=F32))) + br_ref[...]
    _route_tile(logits, run_ref, route_ref, gate_ref, counts_ref)


def _mix_call(x_prompt, x_sample, mod, norm1_g, norm2_g, u_a, y_s5, w_g, w_pool, pool_scale, w_pa,
              w_glu, b_glu, w_pb, w_out, w_r, b_r):
    tile = TILE_MIX
    xp, xs = _x_specs(tile)
    halo_per_tile = tile // POOL_HALO
    n_halo = SEQ // POOL_HALO
    seq_tile = lambda width: pl.BlockSpec((1, tile, width), lambda b, i: (b, i, 0))
    return pl.pallas_call(
        _mix_kernel,
        grid=(N_SEQ, SEQ // tile),
        in_specs=[xp, xs,
                  pl.BlockSpec((1, N_MOD, D_MODEL), lambda b, i: (b, 0, 0)),
                  _resident((1, D_MODEL)), _resident((1, D_MODEL)),
                  seq_tile(POOL_WIDTH),
                  pl.BlockSpec((1, POOL_HALO, POOL_WIDTH),
                               lambda b, i: (b, jnp.maximum(i * halo_per_tile - 1, 0), 0)),
                  pl.BlockSpec((1, POOL_HALO, POOL_WIDTH),
                               lambda b, i: (b, jnp.minimum((i + 1) * halo_per_tile, n_halo - 1), 0)),
                  pl.BlockSpec((None, SSM_CHUNK, tile // SSM_CHUNK, SSM_WIDTH), lambda b, i: (b, 0, i, 0)),
                  _resident(w_g.shape), _resident(w_pool.shape), _resident(pool_scale.shape),
                  _resident(w_pa.shape), _resident(w_glu.shape), _resident(b_glu.shape),
                  _resident(w_pb.shape), _resident(w_out.shape), _resident(w_r.shape), _resident(b_r.shape)],
        out_specs=[seq_tile(D_MODEL), seq_tile(D_PACKED),
                   pl.BlockSpec((1, ROUTE_ROWS, tile), lambda b, i: (b * TILES_PER_SEQ + i, 0, 0)),
                   pl.BlockSpec((1, ROUTE_ROWS, tile), lambda b, i: (b * TILES_PER_SEQ + i, 0, 0)),
                   pl.BlockSpec((N_EXPERTS, 128), lambda b, i: (0, 0))],
        out_shape=[jax.ShapeDtypeStruct((N_SEQ, SEQ, D_MODEL), F32),
                   jax.ShapeDtypeStruct((N_SEQ, SEQ, D_PACKED), jnp.uint32),
                   jax.ShapeDtypeStruct((N_TILES, ROUTE_ROWS, tile), jnp.int32),
                   jax.ShapeDtypeStruct((N_TILES, ROUTE_ROWS, tile), F32),
                   jax.ShapeDtypeStruct((N_EXPERTS, 128), F32)],
        scratch_shapes=[pltpu.VMEM((tile + 2 * POOL_HALO, POOL_WIDTH), F32),
                        pltpu.VMEM((tile, POOL_WIDTH), BF16),
                        pltpu.VMEM((tile, D_MODEL), BF16),
                        pltpu.VMEM((N_EXPERTS, 128), F32)],
        compiler_params=_params(("arbitrary", "arbitrary")),
        name="mix",
    )(x_prompt, x_sample, mod, norm1_g, norm2_g, u_a, u_a, u_a, y_s5, w_g, w_pool, pool_scale, w_pa,
      w_glu, b_glu, w_pb, w_out, w_r, b_r)


def _plan_kernel(cnt_ref, route_ref, counts_ref, dest_ref, vblock_ref, vexpert_ref, vlo_ref, vnext_ref):
    below = (lax.broadcasted_iota(jnp.int32, (N_EXPERTS, N_EXPERTS), 1)
             < lax.broadcasted_iota(jnp.int32, (N_EXPERTS, N_EXPERTS), 0)).astype(F32)
    starts = jnp.dot(below, counts_ref[...], preferred_element_type=F32, precision=lax.Precision.HIGHEST)
    starts_b = jnp.broadcast_to(starts[:, 0:1], (N_EXPERTS, TILE_MIX))
    row_e = lax.broadcasted_iota(jnp.int32, (N_EXPERTS, TILE_MIX), 0)
    zi = jnp.zeros((ROUTE_ROWS - 2, TILE_MIX), jnp.int32)

    def per_tile(t, carry):
        r = route_ref[t]
        s1 = jnp.sum(jnp.where(row_e == r[0:1], starts_b, 0.0), axis=0, keepdims=True)
        s2 = jnp.sum(jnp.where(row_e == r[1:2], starts_b, 0.0), axis=0, keepdims=True)
        dest_ref[t] = jnp.concatenate([s1.astype(jnp.int32) + r[2:3], s2.astype(jnp.int32) + r[3:4], zi], axis=0)
        return carry

    lax.fori_loop(0, N_TILES, per_tile, 0)

    def per_expert(e, carry):
        v, start, last_e = carry
        cnt = cnt_ref[e]
        end = start + cnt
        first = lax.shift_right_logical(start, MOE_SHIFT)
        n_blk = jnp.where(cnt > 0, lax.shift_right_logical(end - 1, MOE_SHIFT) - first + 1, 0)

        def per_block(k, v):
            blk = first + k
            vblock_ref[v] = blk
            vexpert_ref[v] = e
            vlo_ref[v] = jnp.maximum(start - blk * MOE_ROWS, 0)
            return v + 1

        v = lax.fori_loop(0, n_blk, per_block, v)
        return v, end, jnp.where(cnt > 0, e, last_e)

    v, _, last_e = lax.fori_loop(0, N_EXPERTS, per_expert, (0, 0, 0))

    def idle(k, carry):
        vblock_ref[k] = MOE_BLOCKS - 1
        vexpert_ref[k] = last_e
        vlo_ref[k] = MOE_ROWS
        return carry

    lax.fori_loop(v, N_VISITS, idle, 0)

    def following(i, carry):
        nxt, later = carry
        k = N_VISITS - 1 - i
        e = vexpert_ref[k]
        nxt = jnp.where(e != later, later, nxt)
        vnext_ref[k] = nxt
        return nxt, e

    lax.fori_loop(0, N_VISITS, following, (-1, -1))


def _plan_call(cnt, route, counts):
    smem = pl.BlockSpec(memory_space=pltpu.SMEM)
    vmem = pl.BlockSpec(memory_space=pltpu.VMEM)
    visits = jax.ShapeDtypeStruct((N_VISITS,), jnp.int32)
    return pl.pallas_call(
        _plan_kernel,
        in_specs=[smem, vmem, vmem],
        out_specs=[vmem, smem, smem, smem, smem],
        out_shape=[jax.ShapeDtypeStruct((N_TILES, ROUTE_ROWS, TILE_MIX), jnp.int32), visits, visits, visits, visits],
        name="plan",
    )(cnt, route, counts)


def _dest_spec(index_map):
    return pl.BlockSpec((1, 1, TOPK * TILE_MIX), index_map, memory_space=pltpu.SMEM)


def _dispatch_kernel(dest_ref, h_ref, xs_ref, sem):
    tile = TILE_MIX

    def row(r, carry):
        for k in range(TOPK):
            d = dest_ref[0, 0, k * tile + r]
            pltpu.make_async_copy(h_ref.at[pl.ds(r, 1)], xs_ref.at[pl.ds(d, 1)], sem).start()
        return carry

    lax.fori_loop(0, tile, row, 0, unroll=8)
    for k in range(TOPK):
        pltpu.make_async_copy(h_ref, xs_ref.at[pl.ds(0, tile)], sem).wait()


def _dispatch_call(dest, h2):
    return pl.pallas_call(
        _dispatch_kernel,
        grid=(N_TILES,),
        in_specs=[_dest_spec(lambda t: (t, 0, 0)),
                  pl.BlockSpec((TILE_MIX, D_PACKED), lambda t: (t, 0))],
        out_specs=pl.BlockSpec(memory_space=pl.ANY),
        out_shape=jax.ShapeDtypeStruct((N_ASSIGN, D_PACKED), jnp.uint32),
        scratch_shapes=[pltpu.SemaphoreType.DMA],
        compiler_params=_params(("arbitrary",)),
        name="dispatch",
    )(dest, h2)


def _expert_kernel(vblock_ref, vexpert_ref, vlo_ref, vnext_ref, x_ref, wg_ref, wu_ref, wd_ref, o_ref,
                   wgu_s, wd_s, gbuf, ubuf, dbuf, run_ref, sem):
    v = pl.program_id(0)
    e = vexpert_ref[v]

    def weight_copies(expert, slot):
        return [pltpu.make_async_copy(src.at[expert], dst.at[slot], sem.at[slot])
                for src, dst in ((wg_ref, gbuf), (wu_ref, ubuf), (wd_ref, dbuf))]

    @pl.when(v == 0)
    def _():
        run_ref[0] = 0
        for c in weight_copies(e, 0):
            c.start()

    @pl.when((v == 0) | (e != vexpert_ref[jnp.maximum(v - 1, 0)]))
    def _():
        slot = run_ref[0] & 1
        run_ref[0] = run_ref[0] + 1
        for c in weight_copies(e, slot):
            c.wait()
        nxt = vnext_ref[v]

        @pl.when(nxt >= 0)
        def _():
            for c in weight_copies(nxt, 1 - slot):
                c.start()

        wgu_s[:, :D_EXPERT] = gbuf[slot].astype(BF16)
        wgu_s[:, D_EXPERT:] = ubuf[slot].astype(BF16)
        wd_s[...] = dbuf[slot].astype(BF16)

    lo = vlo_ref[v]

    @pl.when(lo < MOE_ROWS)
    def _():
        x_lo, x_hi = _unpack_bf16_pairs(x_ref[...])
        gu = (jnp.dot(x_lo.astype(BF16), wgu_s[pl.ds(0, D_PACKED), :], preferred_element_type=F32)
              + jnp.dot(x_hi.astype(BF16), wgu_s[pl.ds(D_PACKED, D_PACKED), :], preferred_element_type=F32))
        g = gu[:, :D_EXPERT]
        act = (g * jax.nn.sigmoid(g)) * gu[:, D_EXPERT:]
        res = _pack_bf16_pairs(jnp.dot(act.astype(BF16), wd_s[...], preferred_element_type=F32))

        @pl.when(lo == 0)
        def _():
            o_ref[...] = res

        @pl.when(lo > 0)
        def _():
            rows = lax.broadcasted_iota(jnp.int32, (MOE_ROWS, 1), 0)
            o_ref[...] = jnp.where(rows >= lo, res, o_ref[...])


def _expert_call(vblock, vexpert, vlo, vnext, x_slots, w_gate, w_up, w_down):
    hbm = pl.BlockSpec(memory_space=pl.ANY)
    grid_spec = pltpu.PrefetchScalarGridSpec(
        num_scalar_prefetch=4,
        grid=(N_VISITS,),
        in_specs=[pl.BlockSpec((MOE_ROWS, D_PACKED), lambda v, vb, ve, vl, vn: (vb[v], 0)), hbm, hbm, hbm],
        out_specs=pl.BlockSpec((MOE_ROWS, D_PACKED), lambda v, vb, ve, vl, vn: (vb[v], 0)),
        scratch_shapes=[pltpu.VMEM((D_MODEL, 2 * D_EXPERT), BF16),
                        pltpu.VMEM((D_EXPERT, D_MODEL), BF16),
                        pltpu.VMEM((2, D_MODEL, D_EXPERT), F32),
                        pltpu.VMEM((2, D_MODEL, D_EXPERT), F32),
                        pltpu.VMEM((2, D_EXPERT, D_MODEL), F32),
                        pltpu.SMEM((1,), jnp.int32),
                        pltpu.SemaphoreType.DMA((2,))],
    )
    return pl.pallas_call(
        _expert_kernel,
        grid_spec=grid_spec,
        out_shape=jax.ShapeDtypeStruct((N_ASSIGN, D_PACKED), jnp.uint32),
        compiler_params=_params(("arbitrary",)),
        name="experts",
    )(vblock, vexpert, vlo, vnext, x_slots, w_gate, w_up, w_down)


def _final_kernel(dest_ref, dest_next_ref, x1_ref, mod_ref, gate_ref, g_ref, y_ref, o_ref, rows_ref, sem):
    tile = TILE_MIX
    step = pl.program_id(0) * pl.num_programs(1) + pl.program_id(1)
    n_steps = pl.num_programs(0) * pl.num_programs(1)
    slot = lax.rem(step, 2)

    def fetch(dst_ref, into):
        def row(r, carry):
            for k in range(TOPK):
                d = dst_ref[0, 0, k * tile + r]
                pltpu.make_async_copy(y_ref.at[pl.ds(d, 1)], rows_ref.at[into, k, pl.ds(r, 1)], sem.at[into]).start()
            return carry

        lax.fori_loop(0, tile, row, 0, unroll=8)

    @pl.when(step == 0)
    def _():
        fetch(dest_ref, 0)

    @pl.when(step + 1 < n_steps)
    def _():
        fetch(dest_next_ref, 1 - slot)

    for k in range(TOPK):
        pltpu.make_async_copy(y_ref.at[pl.ds(0, tile)], rows_ref.at[slot, k], sem.at[slot]).wait()
    gate = gate_ref[0].T
    lo0, hi0 = _unpack_bf16_pairs(rows_ref[slot, 0])
    lo1, hi1 = _unpack_bf16_pairs(rows_ref[slot, 1])
    moe = jnp.concatenate([gate[:, 0:1] * lo0 + gate[:, 1:2] * lo1, gate[:, 0:1] * hi0 + gate[:, 1:2] * hi1], axis=1)
    x2 = x1_ref[0] + mod_ref[0, 5:6, :] * moe
    r = lax.rsqrt(jnp.mean(x2 * x2, axis=-1, keepdims=True) + EPS)
    o_ref[0] = x2 * r * g_ref[...]


def _final_call(x1, mod, y_slots, dest, gates, final_g, first_seq, n_seq):
    tile = TILE_MIX
    tile_of = lambda b, i: (b + first_seq) * TILES_PER_SEQ + i
    last_tile = (first_seq + n_seq) * TILES_PER_SEQ - 1
    return pl.pallas_call(
        _final_kernel,
        grid=(n_seq, TILES_PER_SEQ),
        in_specs=[_dest_spec(lambda b, i: (tile_of(b, i), 0, 0)),
                  _dest_spec(lambda b, i: (jnp.minimum(tile_of(b, i) + 1, last_tile), 0, 0)),
                  pl.BlockSpec((1, tile, D_MODEL), lambda b, i: (b + first_seq, i, 0)),
                  pl.BlockSpec((1, N_MOD, D_MODEL), lambda b, i: (b + first_seq, 0, 0)),
                  pl.BlockSpec((1, ROUTE_ROWS, tile), lambda b, i: (tile_of(b, i), 0, 0)),
                  _resident((1, D_MODEL)),
                  pl.BlockSpec(memory_space=pl.ANY)],
        out_specs=pl.BlockSpec((1, tile, D_MODEL), lambda b, i: (b, i, 0)),
        out_shape=jax.ShapeDtypeStruct((n_seq, SEQ, D_MODEL), F32),
        scratch_shapes=[pltpu.VMEM((2, TOPK, tile, D_PACKED), jnp.uint32), pltpu.SemaphoreType.DMA((2,))],
        compiler_params=_params(("arbitrary", "arbitrary")),
        name="final",
    )(dest, dest, x1, mod, gates, final_g, y_slots)


def kernel(x_prompt, x_sample, c_prompt, c_sample, w_ada, b_ada, norm1_g, w_in, w_pool, pool_scale,
           ssm_a_re_f, ssm_a_im_f, ssm_log_dt_f, ssm_b_re_f, ssm_b_im_f, ssm_c_re_f, ssm_c_im_f,
           ssm_a_re_b, ssm_a_im_b, ssm_log_dt_b, ssm_b_re_b, ssm_b_im_b, ssm_c_re_b, ssm_c_im_b,
           ssm_d, w_glu, b_glu, w_proj_a, w_proj_b, w_out, norm2_g,
           w_grp, b_grp, w_router, b_router, w_exp_gate, w_exp_up, w_exp_down, final_g):
    n_u = POOL_WIDTH + SSM_WIDTH
    c_pad = jnp.concatenate([c_prompt, c_sample, jnp.zeros((16 - N_SEQ, D_MODEL), F32)], axis=0)
    mod = _mod_call(c_pad, w_ada[0], b_ada).reshape(16, N_MOD, D_MODEL)

    w_in_bf = w_in[0].astype(BF16)
    u_a, u_b = _inproj_call(x_prompt, x_sample, mod, norm1_g, w_in_bf[:, :n_u])

    fwd = (ssm_a_re_f[0], ssm_a_im_f[0], ssm_log_dt_f[0], ssm_b_re_f[0], ssm_b_im_f[0], ssm_c_re_f[0], ssm_c_im_f[0])
    bwd = (ssm_a_re_b[0], ssm_a_im_b[0], ssm_log_dt_b[0], ssm_b_re_b[0], ssm_b_im_b[0], ssm_c_re_b[0], ssm_c_im_b[0])
    y_s5 = _s5_call(u_b, *_s5_operators(fwd, bwd, ssm_d[0]))

    w_r = jnp.concatenate([w_router[0], w_grp[0],
                           jnp.zeros((D_MODEL, ROUTER_COLS - N_GROUPS - N_EXPERTS), F32)], axis=1)
    b_r = jnp.concatenate([b_router[0], b_grp[0],
                           jnp.zeros((ROUTER_COLS - N_GROUPS - N_EXPERTS,), F32)])[None, :]
    w_r_hi = w_r.astype(BF16)
    w_r = jnp.stack([w_r_hi, (w_r - w_r_hi.astype(F32)).astype(BF16)])
    x1, h2, route, gates, counts = _mix_call(
        x_prompt, x_sample, mod, norm1_g, norm2_g, u_a, y_s5, w_in_bf[:, n_u:], w_pool[0].astype(BF16),
        pool_scale, w_proj_a[0].astype(BF16), w_glu[0].astype(BF16), b_glu, w_proj_b[0].astype(BF16),
        w_out[0].astype(BF16), w_r, b_r)

    dest8, vblock, vexpert, vlo, vnext = _plan_call(counts[:, 0].astype(jnp.int32), route, counts)
    dest = dest8[:, :TOPK, :].reshape(N_TILES, 1, TOPK * TILE_MIX)
    x_slots = _dispatch_call(dest, h2.reshape(N_TOK, D_PACKED))
    y_slots = _expert_call(vblock, vexpert, vlo, vnext, x_slots, w_exp_gate[0], w_exp_up[0], w_exp_down[0])

    final_g2 = final_g[None, :]
    y_prompt = _final_call(x1, mod, y_slots, dest, gates, final_g2, 0, N_PROMPT)
    y_sample = _final_call(x1, mod, y_slots, dest, gates, final_g2, N_PROMPT, N_SAMPLE)
    return (y_prompt, y_sample)
```

```python
import math

import jax
import jax.numpy as jnp
from jax import lax
from jax.experimental import pallas as pl
from jax.experimental.pallas import tpu as pltpu

F32 = jnp.float32
BF16 = jnp.bfloat16

D_MODEL = 2048
SEQ = 4096
N_PROMPT = 2
N_SAMPLE = 8
N_SEQ = N_PROMPT + N_SAMPLE
N_TOK = N_SEQ * SEQ
EPS = 1e-6
N_MOD = 6

POOL_WINDOWS = (2, 4, 8, 16)
POOL_GROUP = 256
POOL_WIDTH = 1024
POOL_HALO = 16

SSM_GROUPS = 32
SSM_P = 16
SSM_N = 64
SSM_WIDTH = 512
SSM_CHUNK = 32
SSM_NCHUNK = SEQ // SSM_CHUNK
SSM_COLS = SSM_CHUNK * SSM_P
SSM_STATE_ROWS = 4 * SSM_N
SSM_GROUPS_PER_STEP = 8
SSM_SCAN_STEPS = 7
SSM_TAP_LANES = 2 * SSM_COLS

N_GROUPS = 4
EXPERTS_PER_GROUP = 8
N_EXPERTS = 32
TOPK = 2
D_EXPERT = 512
N_ASSIGN = N_TOK * TOPK
ROUTER_COLS = 128

TILE_INPROJ = 512
TILE_MIX = 256
N_TILES = N_TOK // TILE_MIX
TILES_PER_SEQ = SEQ // TILE_MIX
ROUTE_ROWS = 8
D_PACKED = D_MODEL // 2
MOE_ROWS = 256
MOE_SHIFT = 8
MOE_BLOCKS = N_ASSIGN // MOE_ROWS
N_VISITS = MOE_BLOCKS + N_EXPERTS - 1

VMEM_LIMIT = 60 * 1024 * 1024


def _params(sem, vmem=VMEM_LIMIT):
    return pltpu.CompilerParams(dimension_semantics=sem, vmem_limit_bytes=vmem)


def _resident(shape):
    zeros = (0,) * len(shape)
    return pl.BlockSpec(shape, lambda *_: zeros, pipeline_mode=pl.Buffered(1))


def _ada_norm(x, gain, scale, shift):
    r = lax.rsqrt(jnp.mean(x * x, axis=-1, keepdims=True) + EPS)
    return (x * r * gain) * (1.0 + scale) + shift


def _regroup_matrix(n_outer, n_inner):
    n = n_outer * n_inner
    dst = lax.broadcasted_iota(jnp.int32, (n, n), 0)
    src = lax.broadcasted_iota(jnp.int32, (n, n), 1)
    shift = n_outer.bit_length() - 1
    return (src == (dst & (n_outer - 1)) * n_inner + lax.shift_right_logical(dst, shift)).astype(BF16)


def _pack_bf16_pairs(x):
    w = x.shape[1] // 2
    lo = lax.bitcast_convert_type(x[:, :w].astype(BF16).astype(F32), jnp.uint32)
    hi = lax.bitcast_convert_type(x[:, w:].astype(BF16).astype(F32), jnp.uint32)
    return hi | (lo >> 16)


def _unpack_bf16_pairs(p):
    lo = lax.bitcast_convert_type(p << 16, F32)
    hi = lax.bitcast_convert_type(p & jnp.uint32(0xFFFF0000), F32)
    return lo, hi


def _mod_kernel(c_ref, w_ref, b_ref, o_ref):
    c = c_ref[...]
    s = c * jax.nn.sigmoid(c)
    o_ref[...] = jnp.dot(s.astype(BF16), w_ref[...].astype(BF16), preferred_element_type=F32) + b_ref[...]


def _mod_call(c_pad, w_ada, b_ada):
    n = w_ada.shape[1]
    tn = 1024
    return pl.pallas_call(
        _mod_kernel,
        grid=(n // tn,),
        in_specs=[pl.BlockSpec(c_pad.shape, lambda j: (0, 0)),
                  pl.BlockSpec((D_MODEL, tn), lambda j: (0, j)),
                  pl.BlockSpec((1, tn), lambda j: (0, j))],
        out_specs=pl.BlockSpec((c_pad.shape[0], tn), lambda j: (0, j)),
        out_shape=jax.ShapeDtypeStruct((c_pad.shape[0], n), F32),
        compiler_params=_params(("arbitrary",)),
        name="mod",
    )(c_pad, w_ada, b_ada)


def _x_specs(tile):
    last = SEQ // tile - 1
    xp = pl.BlockSpec((1, tile, D_MODEL),
                      lambda b, i: (jnp.minimum(b, N_PROMPT - 1), jnp.where(b < N_PROMPT, i, last), 0))
    xs = pl.BlockSpec((1, tile, D_MODEL),
                      lambda b, i: (jnp.maximum(b - N_PROMPT, 0), jnp.where(b < N_PROMPT, 0, i), 0))
    return xp, xs


def _load_x(xp_ref, xs_ref):
    return jnp.where(pl.program_id(0) < N_PROMPT, xp_ref[0], xs_ref[0])


def _inproj_kernel(xp_ref, xs_ref, mod_ref, g_ref, w_ref, ua_ref, ub_ref):
    x = _load_x(xp_ref, xs_ref)
    h = _ada_norm(x, g_ref[...], mod_ref[0, 1:2, :], mod_ref[0, 0:1, :])
    p = jnp.dot(h.astype(BF16), w_ref[...], preferred_element_type=F32)
    ua_ref[0] = p[:, :POOL_WIDTH].astype(BF16)
    n_chunk = TILE_INPROJ // SSM_CHUNK
    ub = jnp.dot(_regroup_matrix(n_chunk, SSM_CHUNK), p[:, POOL_WIDTH:].astype(BF16), preferred_element_type=F32)
    ub_ref[...] = ub.reshape(SSM_CHUNK, n_chunk, SSM_WIDTH).astype(BF16)


def _inproj_call(x_prompt, x_sample, mod, norm_g, w_u):
    tile = TILE_INPROJ
    xp, xs = _x_specs(tile)
    return pl.pallas_call(
        _inproj_kernel,
        grid=(N_SEQ, SEQ // tile),
        in_specs=[xp, xs,
                  pl.BlockSpec((1, N_MOD, D_MODEL), lambda b, i: (b, 0, 0)),
                  _resident((1, D_MODEL)),
                  _resident(w_u.shape)],
        out_specs=[pl.BlockSpec((1, tile, POOL_WIDTH), lambda b, i: (b, i, 0)),
                   pl.BlockSpec((None, SSM_CHUNK, tile // SSM_CHUNK, SSM_WIDTH), lambda b, i: (b, 0, i, 0))],
        out_shape=[jax.ShapeDtypeStruct((N_SEQ, SEQ, POOL_WIDTH), BF16),
                   jax.ShapeDtypeStruct((N_SEQ, SSM_CHUNK, SSM_NCHUNK, SSM_WIDTH), BF16)],
        compiler_params=_params(("arbitrary", "arbitrary")),
        name="inproj",
    )(x_prompt, x_sample, mod, norm_g, w_u)


def _s5_kernel(u_ref, wst_ref, taps_ref, wot_ref, a_ref, y_ref, ut_ref, yt_ref, apow_ref, tt_ref):
    n, nc, ng = SSM_N, SSM_NCHUNK, SSM_GROUPS_PER_STEP

    @pl.when(pl.program_id(1) == 0)
    def _():
        for g in range(ng):
            a = a_ref[g]
            for k in range(SSM_SCAN_STEPS):
                apow_ref[g, k] = a
                f_re, f_im, b_re, b_im = a[0:n], a[n:2 * n], a[2 * n:3 * n], a[3 * n:4 * n]
                a = jnp.concatenate([f_re * f_re - f_im * f_im, 2.0 * (f_re * f_im),
                                     b_re * b_re - b_im * b_im, 2.0 * (b_re * b_im)], axis=0)
            strip = taps_ref[g]
            for t in range(SSM_CHUNK):
                start = (SSM_CHUNK - 1 - t) * SSM_P
                window = pltpu.roll(strip, SSM_TAP_LANES - start, 1) if start else strip
                tt_ref[g, pl.ds(t * SSM_P, SSM_P), :] = window[:, :SSM_COLS].astype(BF16)

    for s in range(SSM_CHUNK):
        blk = u_ref[s].astype(F32).T
        ut_ref[:, pl.ds(s * SSM_P, SSM_P), :] = blk.reshape(ng, SSM_P, nc).astype(BF16)

    lane = lax.broadcasted_iota(jnp.int32, (n, nc), 1)

    def shifted(x, d, forward):
        if forward:
            return jnp.where(lane >= d, pltpu.roll(x, d, 1), 0.0)
        return jnp.where(lane < nc - d, pltpu.roll(x, nc - d, 1), 0.0)

    def scan(x_re, x_im, g, re_row, im_row, forward):
        for k in range(SSM_SCAN_STEPS):
            a_re = apow_ref[g, k, pl.ds(re_row, n), :]
            a_im = apow_ref[g, k, pl.ds(im_row, n), :]
            s_re = shifted(x_re, 2 ** k, forward)
            s_im = shifted(x_im, 2 ** k, forward)
            x_re, x_im = x_re + (a_re * s_re - a_im * s_im), x_im + (a_re * s_im + a_im * s_re)
        return x_re, x_im

    def per_group(g, carry):
        ut = ut_ref[g]
        st = jnp.dot(wst_ref[g], ut, preferred_element_type=F32)
        f_re, f_im = scan(st[0:n], st[n:2 * n], g, 0, n, True)
        b_re, b_im = scan(st[2 * n:3 * n], st[3 * n:4 * n], g, 2 * n, 3 * n, False)
        carried = jnp.concatenate([shifted(f_re, 1, True), shifted(f_im, 1, True),
                                   shifted(b_re, 1, False), shifted(b_im, 1, False)], axis=0)
        yt_ref[g] = (jnp.dot(tt_ref[g], ut, preferred_element_type=F32)
                     + jnp.dot(wot_ref[g], carried.astype(BF16), preferred_element_type=F32))
        return carry

    lax.fori_loop(0, ng, per_group, 0, unroll=2)
    for t in range(SSM_CHUNK):
        y_ref[t] = yt_ref[:, pl.ds(t * SSM_P, SSM_P), :].reshape(ng * SSM_P, nc).T


def _s5_call(u_ph, wst, tap_strip, wot, a_chunk):
    ng = SSM_GROUPS_PER_STEP
    seq = pl.BlockSpec((None, SSM_CHUNK, SSM_NCHUNK, ng * SSM_P), lambda q, b: (b, 0, 0, q))
    mat = lambda rows, cols: pl.BlockSpec((ng, rows, cols), lambda q, b: (q, 0, 0))
    return pl.pallas_call(
        _s5_kernel,
        grid=(SSM_GROUPS // ng, N_SEQ),
        in_specs=[seq, mat(SSM_STATE_ROWS, SSM_COLS), mat(SSM_P, SSM_TAP_LANES), mat(SSM_COLS, SSM_STATE_ROWS),
                  mat(SSM_STATE_ROWS, SSM_NCHUNK)],
        out_specs=seq,
        out_shape=jax.ShapeDtypeStruct((N_SEQ, SSM_CHUNK, SSM_NCHUNK, SSM_WIDTH), F32),
        scratch_shapes=[pltpu.VMEM((ng, SSM_COLS, SSM_NCHUNK), BF16),
                        pltpu.VMEM((ng, SSM_COLS, SSM_NCHUNK), F32),
                        pltpu.VMEM((ng, SSM_SCAN_STEPS, SSM_STATE_ROWS, SSM_NCHUNK), F32),
                        pltpu.VMEM((ng, SSM_COLS, SSM_COLS), BF16)],
        compiler_params=_params(("arbitrary", "arbitrary")),
        name="s5",
    )(u_ph, wst, tap_strip, wot, a_chunk)


def _s5_direction(a_re, a_im, log_dt, b_re, b_im, c_re, c_im):
    dt = jnp.exp(log_dt)[:, None]
    k = jnp.arange(SSM_CHUNK + 1, dtype=F32)[None, :, None]
    mag = jnp.exp(k * (a_re * dt)[:, None, :])
    ang = k * (a_im * dt)[:, None, :]
    pw_re = mag * jnp.cos(ang)
    pw_im = mag * jnp.sin(ang)
    ab_re, ab_im = pw_re[:, 1], pw_im[:, 1]
    den = a_re * a_re + a_im * a_im
    q_re = ((ab_re - 1.0) * a_re + ab_im * a_im) / den
    q_im = (ab_im * a_re - (ab_re - 1.0) * a_im) / den
    bb_re = q_re[:, :, None] * b_re - q_im[:, :, None] * b_im
    bb_im = q_re[:, :, None] * b_im + q_im[:, :, None] * b_re
    cp_re = c_re[:, None] * pw_re[:, :, None, :] - c_im[:, None] * pw_im[:, :, None, :]
    cp_im = c_re[:, None] * pw_im[:, :, None, :] + c_im[:, None] * pw_re[:, :, None, :]
    contract = lambda c, b: jnp.einsum('gkqn,gnp->gkqp', c[:, :SSM_CHUNK], b, precision=lax.Precision.HIGHEST)
    taps = contract(cp_re, bb_re) - contract(cp_im, bb_im)
    return pw_re, pw_im, bb_re, bb_im, cp_re, cp_im, taps


def _s5_operators(fwd, bwd, ssm_d):
    g, q, p, n = SSM_GROUPS, SSM_CHUNK, SSM_P, SSM_N
    pf_re, pf_im, bf_re, bf_im, cf_re, cf_im, taps_f = _s5_direction(*fwd)
    pb_re, pb_im, bb_re, bb_im, cb_re, cb_im, taps_b = _s5_direction(*bwd)
    descending, ascending = slice(q - 1, None, -1), slice(0, q)
    from_one, down_to_one = slice(1, q + 1), slice(q, 0, -1)

    centre = taps_f[:, 0] + taps_b[:, 0] + jnp.eye(p, dtype=F32)[None] * ssm_d.reshape(g, 1, p)
    by_lag = jnp.concatenate([taps_b[:, :0:-1], centre[:, None], taps_f[:, 1:]], axis=1)
    strip = by_lag[:, ::-1].transpose(0, 2, 1, 3).reshape(g, p, (2 * q - 1) * p)
    strip = jnp.pad(strip, ((0, 0), (0, 0), (0, SSM_TAP_LANES - (2 * q - 1) * p)))

    col = jnp.arange(q * p)
    rep_s = (col[None, :] // p == jnp.arange(q)[:, None]).astype(F32)
    tile_p = (col[None, :] % p == jnp.arange(p)[:, None]).astype(F32)
    expand = lambda x, m: jnp.einsum('gnk,kl->gnl', x, m, precision=lax.Precision.HIGHEST)

    def state_in(pw_re, pw_im, b_re, b_im, powers):
        a_re = expand(pw_re[:, powers].transpose(0, 2, 1), rep_s)
        a_im = expand(pw_im[:, powers].transpose(0, 2, 1), rep_s)
        b_re, b_im = expand(b_re, tile_p), expand(b_im, tile_p)
        return a_re * b_re - a_im * b_im, a_re * b_im + a_im * b_re

    wsf_re, wsf_im = state_in(pf_re[:, :q], pf_im[:, :q], bf_re, bf_im, descending)
    wsb_re, wsb_im = state_in(pb_re, pb_im, bb_re, bb_im, ascending)
    wst = jnp.concatenate([wsf_re, wsf_im, wsb_re, wsb_im], axis=1)

    def state_out(cp_re, cp_im, powers):
        o_re = cp_re[:, powers].reshape(g, q * p, n)
        o_im = -cp_im[:, powers].reshape(g, q * p, n)
        return o_re, o_im

    of_re, of_im = state_out(cf_re, cf_im, from_one)
    ob_re, ob_im = state_out(cb_re, cb_im, down_to_one)
    wot = jnp.concatenate([of_re, of_im, ob_re, ob_im], axis=2)
    a_chunk = jnp.concatenate([pf_re[:, q], pf_im[:, q], pb_re[:, q], pb_im[:, q]], axis=1)
    a_chunk = jnp.broadcast_to(a_chunk[:, :, None], (g, SSM_STATE_ROWS, SSM_NCHUNK))
    return wst.astype(BF16), strip, wot.astype(BF16), a_chunk


def _gelu_tanh(x):
    return 0.5 * x * (1.0 + jnp.tanh(math.sqrt(2.0 / math.pi) * (x + 0.044715 * (x * x * x))))


def _route_tile(logits, run_ref, route_ref, gate_ref, counts_ref):
    tile = logits.shape[0]
    neg = -jnp.inf
    lt = logits.T
    row8 = lax.broadcasted_iota(jnp.int32, (EXPERTS_PER_GROUP, tile), 0)
    gl = jnp.where(row8 < N_GROUPS, lt[N_EXPERTS:N_EXPERTS + 8], neg)
    gmax = jnp.max(gl, axis=0, keepdims=True)
    g_sel = jnp.min(jnp.where(gl == gmax, row8, 8), axis=0, keepdims=True)
    g_p = 1.0 / jnp.sum(jnp.exp(gl - gmax), axis=0, keepdims=True)
    in_grp = lt[0:EXPERTS_PER_GROUP]
    for g in range(1, N_GROUPS):
        in_grp = jnp.where(g_sel == g, lt[g * EXPERTS_PER_GROUP:(g + 1) * EXPERTS_PER_GROUP], in_grp)
    m1 = jnp.max(in_grp, axis=0, keepdims=True)
    i1 = jnp.min(jnp.where(in_grp == m1, row8, 8), axis=0, keepdims=True)
    rest = jnp.where(row8 == i1, neg, in_grp)
    m2 = jnp.max(rest, axis=0, keepdims=True)
    i2 = jnp.min(jnp.where(rest == m2, row8, 8), axis=0, keepdims=True)
    e21 = jnp.exp(m2 - m1)
    p1 = 1.0 / (1.0 + e21)
    eid1 = g_sel * EXPERTS_PER_GROUP + i1
    eid2 = g_sel * EXPERTS_PER_GROUP + i2

    row_e = lax.broadcasted_iota(jnp.int32, (N_EXPERTS, tile), 0)
    oh1 = (row_e == eid1).astype(F32)
    oh2 = (row_e == eid2).astype(F32)
    earlier = (lax.broadcasted_iota(jnp.int32, (tile, tile), 0)
               < lax.broadcasted_iota(jnp.int32, (tile, tile), 1)).astype(BF16)
    before1 = jnp.dot(oh1.astype(BF16), earlier, preferred_element_type=F32)
    before2 = jnp.dot(oh2.astype(BF16), earlier, preferred_element_type=F32)
    tot1 = jnp.sum(oh1, axis=1, keepdims=True)
    tot2 = jnp.sum(oh2, axis=1, keepdims=True)
    run = run_ref[:, 0:1]
    rank1 = jnp.sum(oh1 * (before1 + run), axis=0, keepdims=True)
    rank2 = jnp.sum(oh2 * (before2 + (run + tot1)), axis=0, keepdims=True)
    new_run = jnp.broadcast_to(run + tot1 + tot2, run_ref.shape)
    run_ref[...] = new_run
    counts_ref[...] = new_run
    zi = jnp.zeros((ROUTE_ROWS - 4, tile), jnp.int32)
    route_ref[0] = jnp.concatenate([eid1, eid2, rank1.astype(jnp.int32), rank2.astype(jnp.int32), zi], axis=0)
    zf = jnp.zeros((ROUTE_ROWS - 2, tile), F32)
    gate_ref[0] = jnp.concatenate([g_p * p1, g_p * (e21 * p1), zf], axis=0)


def _mix_kernel(xp_ref, xs_ref, mod_ref, g1_ref, g2_ref, ua_ref, ua_prev_ref, ua_next_ref, ys_ref,
                wg_ref, wpool_ref, pscale_ref, wpa_ref, wglu_ref, bglu_ref, wpb_ref, wout_ref,
                wr_ref, br_ref,
                x1_ref, h2_ref, route_ref, gate_ref, counts_ref, ext_ref, diff_ref, merged_ref, run_ref):
    tile = TILE_MIX
    i = pl.program_id(1)

    @pl.when((pl.program_id(0) == 0) & (i == 0))
    def _():
        run_ref[...] = jnp.zeros_like(run_ref)

    x = _load_x(xp_ref, xs_ref)
    h = _ada_norm(x, g1_ref[...], mod_ref[0, 1:2, :], mod_ref[0, 0:1, :]).astype(BF16)

    first = i == 0
    last = i == pl.num_programs(1) - 1
    ext_ref[pl.ds(0, POOL_HALO), :] = jnp.where(first, 0.0, ua_prev_ref[0].astype(F32))
    ext_ref[pl.ds(POOL_HALO, tile), :] = ua_ref[0].astype(F32)
    ext_ref[pl.ds(POOL_HALO + tile, POOL_HALO), :] = jnp.where(last, 0.0, ua_next_ref[0].astype(F32))
    pos = i * tile + lax.broadcasted_iota(jnp.int32, (tile, 1), 0)
    for k, w in enumerate(POOL_WINDOWS):
        cols = pl.ds(k * POOL_GROUP, POOL_GROUP)
        lo = jnp.maximum(pos - w // 2, 0)
        hi = jnp.minimum(pos + (w - 1 - w // 2), SEQ - 1)
        inv_cnt = 1.0 / (hi - lo + 1).astype(F32)
        acc = ext_ref[pl.ds(POOL_HALO - w // 2, tile), cols]
        for j in range(1, w):
            acc = acc + ext_ref[pl.ds(POOL_HALO - w // 2 + j, tile), cols]
        diff = acc * inv_cnt - ext_ref[pl.ds(POOL_HALO, tile), cols]
        mixed = jnp.dot(diff.astype(BF16), wpool_ref[k], preferred_element_type=F32)
        diff_ref[:, cols] = (mixed * pscale_ref[:, cols]).astype(BF16)

    z = _gelu_tanh(ys_ref[...].reshape(tile, SSM_WIDTH))
    zg = z * jax.nn.sigmoid(jnp.dot(z.astype(BF16), wglu_ref[...], preferred_element_type=F32) + bglu_ref[...])
    zg = jnp.dot(_regroup_matrix(SSM_CHUNK, tile // SSM_CHUNK), zg.astype(BF16),
                 preferred_element_type=F32).astype(BF16)
    pa = diff_ref[...]

    chunk = 1024
    for j in range(D_MODEL // chunk):
        c0 = j * chunk
        g_a = jnp.dot(h, wg_ref[:, pl.ds(c0, chunk)], preferred_element_type=F32)
        y_a = jnp.dot(pa, wpa_ref[:, pl.ds(c0, chunk)], preferred_element_type=F32)
        m = jax.nn.sigmoid(g_a) * y_a
        g_b = jnp.dot(h, wg_ref[:, pl.ds(D_MODEL + c0, chunk)], preferred_element_type=F32)
        y_b = jnp.dot(zg, wpb_ref[:, pl.ds(c0, chunk)], preferred_element_type=F32)
        m = m + jax.nn.sigmoid(g_b) * y_b
        merged_ref[:, pl.ds(c0, chunk)] = m.astype(BF16)

    x1 = x + mod_ref[0, 2:3, :] * jnp.dot(merged_ref[...], wout_ref[...], preferred_element_type=F32)
    x1_ref[0] = x1
    h2 = _ada_norm(x1, g2_ref[...], mod_ref[0, 4:5, :], mod_ref[0, 3:4, :])
    h2_ref[0] = _pack_bf16_pairs(h2)
    h2_hi = h2.astype(BF16)
    h2_lo = (h2 - h2_hi.astype(F32)).astype(BF16)
    prod = jnp.dot(jnp.concatenate([h2_hi, h2_lo], axis=0), wr_ref[...], preferred_element_type=F32)
    logits = (prod[:tile, :ROUTER_COLS]
              + (prod[tile:, :ROUTER_COLS] + prod[:tile, ROUTER_COLS:])) + br_ref[...]
    _route_tile(logits, run_ref, route_ref, gate_ref, counts_ref)


def _mix_call(x_prompt, x_sample, mod, norm1_g, norm2_g, u_a, y_s5, w_g, w_pool, pool_scale, w_pa,
              w_glu, b_glu, w_pb, w_out, w_r, b_r):
    tile = TILE_MIX
    xp, xs = _x_specs(tile)
    halo_per_tile = tile // POOL_HALO
    n_halo = SEQ // POOL_HALO
    seq_tile = lambda width: pl.BlockSpec((1, tile, width), lambda b, i: (b, i, 0))
    return pl.pallas_call(
        _mix_kernel,
        grid=(N_SEQ, SEQ // tile),
        in_specs=[xp, xs,
                  pl.BlockSpec((1, N_MOD, D_MODEL), lambda b, i: (b, 0, 0)),
                  _resident((1, D_MODEL)), _resident((1, D_MODEL)),
                  seq_tile(POOL_WIDTH),
                  pl.BlockSpec((1, POOL_HALO, POOL_WIDTH),
                               lambda b, i: (b, jnp.maximum(i * halo_per_tile - 1, 0), 0)),
                  pl.BlockSpec((1, POOL_HALO, POOL_WIDTH),
                               lambda b, i: (b, jnp.minimum((i + 1) * halo_per_tile, n_halo - 1), 0)),
                  pl.BlockSpec((None, SSM_CHUNK, tile // SSM_CHUNK, SSM_WIDTH), lambda b, i: (b, 0, i, 0)),
                  _resident(w_g.shape), _resident(w_pool.shape), _resident(pool_scale.shape),
                  _resident(w_pa.shape), _resident(w_glu.shape), _resident(b_glu.shape),
                  _resident(w_pb.shape), _resident(w_out.shape), _resident(w_r.shape), _resident(b_r.shape)],
        out_specs=[seq_tile(D_MODEL), seq_tile(D_PACKED),
                   pl.BlockSpec((1, ROUTE_ROWS, tile), lambda b, i: (b * TILES_PER_SEQ + i, 0, 0)),
                   pl.BlockSpec((1, ROUTE_ROWS, tile), lambda b, i: (b * TILES_PER_SEQ + i, 0, 0)),
                   pl.BlockSpec((N_EXPERTS, 128), lambda b, i: (0, 0))],
        out_shape=[jax.ShapeDtypeStruct((N_SEQ, SEQ, D_MODEL), F32),
                   jax.ShapeDtypeStruct((N_SEQ, SEQ, D_PACKED), jnp.uint32),
                   jax.ShapeDtypeStruct((N_TILES, ROUTE_ROWS, tile), jnp.int32),
                   jax.ShapeDtypeStruct((N_TILES, ROUTE_ROWS, tile), F32),
                   jax.ShapeDtypeStruct((N_EXPERTS, 128), F32)],
        scratch_shapes=[pltpu.VMEM((tile + 2 * POOL_HALO, POOL_WIDTH), F32),
                        pltpu.VMEM((tile, POOL_WIDTH), BF16),
                        pltpu.VMEM((tile, D_MODEL), BF16),
                        pltpu.VMEM((N_EXPERTS, 128), F32)],
        compiler_params=_params(("arbitrary", "arbitrary")),
        name="mix",
    )(x_prompt, x_sample, mod, norm1_g, norm2_g, u_a, u_a, u_a, y_s5, w_g, w_pool, pool_scale, w_pa,
      w_glu, b_glu, w_pb, w_out, w_r, b_r)


def _plan_kernel(cnt_ref, route_ref, counts_ref, dest_ref, vblock_ref, vexpert_ref, vlo_ref, vnext_ref):
    below = (lax.broadcasted_iota(jnp.int32, (N_EXPERTS, N_EXPERTS), 1)
             < lax.broadcasted_iota(jnp.int32, (N_EXPERTS, N_EXPERTS), 0)).astype(F32)
    starts = jnp.dot(below, counts_ref[...], preferred_element_type=F32, precision=lax.Precision.HIGHEST)
    starts_b = jnp.broadcast_to(starts[:, 0:1], (N_EXPERTS, TILE_MIX))
    row_e = lax.broadcasted_iota(jnp.int32, (N_EXPERTS, TILE_MIX), 0)
    zi = jnp.zeros((ROUTE_ROWS - 2, TILE_MIX), jnp.int32)

    def per_tile(t, carry):
        r = route_ref[t]
        s1 = jnp.sum(jnp.where(row_e == r[0:1], starts_b, 0.0), axis=0, keepdims=True)
        s2 = jnp.sum(jnp.where(row_e == r[1:2], starts_b, 0.0), axis=0, keepdims=True)
        dest_ref[t] = jnp.concatenate([s1.astype(jnp.int32) + r[2:3], s2.astype(jnp.int32) + r[3:4], zi], axis=0)
        return carry

    lax.fori_loop(0, N_TILES, per_tile, 0)

    def per_expert(e, carry):
        v, start, last_e = carry
        cnt = cnt_ref[e]
        end = start + cnt
        first = lax.shift_right_logical(start, MOE_SHIFT)
        n_blk = jnp.where(cnt > 0, lax.shift_right_logical(end - 1, MOE_SHIFT) - first + 1, 0)

        def per_block(k, v):
            blk = first + k
            vblock_ref[v] = blk
            vexpert_ref[v] = e
            vlo_ref[v] = jnp.maximum(start - blk * MOE_ROWS, 0)
            return v + 1

        v = lax.fori_loop(0, n_blk, per_block, v)
        return v, end, jnp.where(cnt > 0, e, last_e)

    v, _, last_e = lax.fori_loop(0, N_EXPERTS, per_expert, (0, 0, 0))

    def idle(k, carry):
        vblock_ref[k] = MOE_BLOCKS - 1
        vexpert_ref[k] = last_e
        vlo_ref[k] = MOE_ROWS
        return carry

    lax.fori_loop(v, N_VISITS, idle, 0)

    def following(i, carry):
        nxt, later = carry
        k = N_VISITS - 1 - i
        e = vexpert_ref[k]
        nxt = jnp.where(e != later, later, nxt)
        vnext_ref[k] = nxt
        return nxt, e

    lax.fori_loop(0, N_VISITS, following, (-1, -1))


def _plan_call(cnt, route, counts):
    smem = pl.BlockSpec(memory_space=pltpu.SMEM)
    vmem = pl.BlockSpec(memory_space=pltpu.VMEM)
    visits = jax.ShapeDtypeStruct((N_VISITS,), jnp.int32)
    return pl.pallas_call(
        _plan_kernel,
        in_specs=[smem, vmem, vmem],
        out_specs=[vmem, smem, smem, smem, smem],
        out_shape=[jax.ShapeDtypeStruct((N_TILES, ROUTE_ROWS, TILE_MIX), jnp.int32), visits, visits, visits, visits],
        name="plan",
    )(cnt, route, counts)


def _dest_spec(index_map):
    return pl.BlockSpec((1, 1, TOPK * TILE_MIX), index_map, memory_space=pltpu.SMEM)


DISPATCH_SLOTS = 3


def _dispatch_kernel(dest_ref, h_ref, xs_ref, buf, in_sem, out_sem):
    tile = TILE_MIX
    t = pl.program_id(0)
    n = pl.num_programs(0)
    slot = lax.rem(t, DISPATCH_SLOTS)
    nxt = lax.rem(t + 1, DISPATCH_SLOTS)

    def load(i, s):
        return pltpu.make_async_copy(h_ref.at[pl.ds(i * tile, tile)], buf.at[s], in_sem.at[s])

    def drain(s):
        for k in range(TOPK):
            pltpu.make_async_copy(buf.at[s], xs_ref.at[pl.ds(0, tile)], out_sem.at[s]).wait()

    @pl.when(t == 0)
    def _():
        load(0, 0).start()

    @pl.when(t >= DISPATCH_SLOTS - 1)
    def _():
        drain(nxt)

    @pl.when(t + 1 < n)
    def _():
        load(t + 1, nxt).start()

    load(t, slot).wait()

    def row(r, carry):
        for k in range(TOPK):
            d = dest_ref[0, 0, k * tile + r]
            pltpu.make_async_copy(buf.at[slot, pl.ds(r, 1)], xs_ref.at[pl.ds(d, 1)], out_sem.at[slot]).start()
        return carry

    lax.fori_loop(0, tile, row, 0, unroll=8)

    @pl.when(t == n - 1)
    def _():
        drain(lax.rem(t + DISPATCH_SLOTS - 1, DISPATCH_SLOTS))
        drain(slot)


def _dispatch_call(dest, h2):
    return pl.pallas_call(
        _dispatch_kernel,
        grid=(N_TILES,),
        in_specs=[_dest_spec(lambda t: (t, 0, 0)), pl.BlockSpec(memory_space=pl.ANY)],
        out_specs=pl.BlockSpec(memory_space=pl.ANY),
        out_shape=jax.ShapeDtypeStruct((N_ASSIGN, D_PACKED), jnp.uint32),
        scratch_shapes=[pltpu.VMEM((DISPATCH_SLOTS, TILE_MIX, D_PACKED), jnp.uint32),
                        pltpu.SemaphoreType.DMA((DISPATCH_SLOTS,)),
                        pltpu.SemaphoreType.DMA((DISPATCH_SLOTS,))],
        compiler_params=_params(("arbitrary",)),
        name="dispatch",
    )(dest, h2)


def _expert_kernel(vblock_ref, vexpert_ref, vlo_ref, vnext_ref, x_ref, wg_ref, wu_ref, wd_ref, o_ref,
                   wgu_s, wd_s, gbuf, ubuf, dbuf, run_ref, sem):
    v = pl.program_id(0)
    e = vexpert_ref[v]

    def weight_copies(expert, slot):
        return [pltpu.make_async_copy(src.at[expert], dst.at[slot], sem.at[slot])
                for src, dst in ((wg_ref, gbuf), (wu_ref, ubuf), (wd_ref, dbuf))]

    @pl.when(v == 0)
    def _():
        run_ref[0] = 0
        for c in weight_copies(e, 0):
            c.start()

    @pl.when((v == 0) | (e != vexpert_ref[jnp.maximum(v - 1, 0)]))
    def _():
        slot = run_ref[0] & 1
        run_ref[0] = run_ref[0] + 1
        for c in weight_copies(e, slot):
            c.wait()
        nxt = vnext_ref[v]

        @pl.when(nxt >= 0)
        def _():
            for c in weight_copies(nxt, 1 - slot):
                c.start()

        wgu_s[:, :D_EXPERT] = gbuf[slot].astype(BF16)
        wgu_s[:, D_EXPERT:] = ubuf[slot].astype(BF16)
        wd_s[...] = dbuf[slot].astype(BF16)

    lo = vlo_ref[v]

    @pl.when(lo < MOE_ROWS)
    def _():
        x_lo, x_hi = _unpack_bf16_pairs(x_ref[...])
        gu = (jnp.dot(x_lo.astype(BF16), wgu_s[pl.ds(0, D_PACKED), :], preferred_element_type=F32)
              + jnp.dot(x_hi.astype(BF16), wgu_s[pl.ds(D_PACKED, D_PACKED), :], preferred_element_type=F32))
        g = gu[:, :D_EXPERT]
        act = (g * jax.nn.sigmoid(g)) * gu[:, D_EXPERT:]
        res = _pack_bf16_pairs(jnp.dot(act.astype(BF16), wd_s[...], preferred_element_type=F32))

        @pl.when(lo == 0)
        def _():
            o_ref[...] = res

        @pl.when(lo > 0)
        def _():
            rows = lax.broadcasted_iota(jnp.int32, (MOE_ROWS, 1), 0)
            o_ref[...] = jnp.where(rows >= lo, res, o_ref[...])


def _expert_call(vblock, vexpert, vlo, vnext, x_slots, w_gate, w_up, w_down):
    hbm = pl.BlockSpec(memory_space=pl.ANY)
    grid_spec = pltpu.PrefetchScalarGridSpec(
        num_scalar_prefetch=4,
        grid=(N_VISITS,),
        in_specs=[pl.BlockSpec((MOE_ROWS, D_PACKED), lambda v, vb, ve, vl, vn: (vb[v], 0)), hbm, hbm, hbm],
        out_specs=pl.BlockSpec((MOE_ROWS, D_PACKED), lambda v, vb, ve, vl, vn: (vb[v], 0)),
        scratch_shapes=[pltpu.VMEM((D_MODEL, 2 * D_EXPERT), BF16),
                        pltpu.VMEM((D_EXPERT, D_MODEL), BF16),
                        pltpu.VMEM((2, D_MODEL, D_EXPERT), F32),
                        pltpu.VMEM((2, D_MODEL, D_EXPERT), F32),
                        pltpu.VMEM((2, D_EXPERT, D_MODEL), F32),
                        pltpu.SMEM((1,), jnp.int32),
                        pltpu.SemaphoreType.DMA((2,))],
    )
    return pl.pallas_call(
        _expert_kernel,
        grid_spec=grid_spec,
        out_shape=jax.ShapeDtypeStruct((N_ASSIGN, D_PACKED), jnp.uint32),
        compiler_params=_params(("arbitrary",)),
        name="experts",
    )(vblock, vexpert, vlo, vnext, x_slots, w_gate, w_up, w_down)


def _final_kernel(dest_ref, dest_next_ref, x1_ref, mod_ref, gate_ref, g_ref, y_ref, o_ref, rows_ref, sem):
    tile = TILE_MIX
    step = pl.program_id(0) * pl.num_programs(1) + pl.program_id(1)
    n_steps = pl.num_programs(0) * pl.num_programs(1)
    slot = lax.rem(step, 2)

    def fetch(dst_ref, into):
        def row(r, carry):
            for k in range(TOPK):
                d = dst_ref[0, 0, k * tile + r]
                pltpu.make_async_copy(y_ref.at[pl.ds(d, 1)], rows_ref.at[into, k, pl.ds(r, 1)], sem.at[into]).start()
            return carry

        lax.fori_loop(0, tile, row, 0, unroll=8)

    @pl.when(step == 0)
    def _():
        fetch(dest_ref, 0)

    @pl.when(step + 1 < n_steps)
    def _():
        fetch(dest_next_ref, 1 - slot)

    for k in range(TOPK):
        pltpu.make_async_copy(y_ref.at[pl.ds(0, tile)], rows_ref.at[slot, k], sem.at[slot]).wait()
    gate = gate_ref[0].T
    lo0, hi0 = _unpack_bf16_pairs(rows_ref[slot, 0])
    lo1, hi1 = _unpack_bf16_pairs(rows_ref[slot, 1])
    moe = jnp.concatenate([gate[:, 0:1] * lo0 + gate[:, 1:2] * lo1, gate[:, 0:1] * hi0 + gate[:, 1:2] * hi1], axis=1)
    x2 = x1_ref[0] + mod_ref[0, 5:6, :] * moe
    r = lax.rsqrt(jnp.mean(x2 * x2, axis=-1, keepdims=True) + EPS)
    o_ref[0] = x2 * r * g_ref[...]


def _final_call(x1, mod, y_slots, dest, gates, final_g, first_seq, n_seq):
    tile = TILE_MIX
    tile_of = lambda b, i: (b + first_seq) * TILES_PER_SEQ + i
    last_tile = (first_seq + n_seq) * TILES_PER_SEQ - 1
    return pl.pallas_call(
        _final_kernel,
        grid=(n_seq, TILES_PER_SEQ),
        in_specs=[_dest_spec(lambda b, i: (tile_of(b, i), 0, 0)),
                  _dest_spec(lambda b, i: (jnp.minimum(tile_of(b, i) + 1, last_tile), 0, 0)),
                  pl.BlockSpec((1, tile, D_MODEL), lambda b, i: (b + first_seq, i, 0)),
                  pl.BlockSpec((1, N_MOD, D_MODEL), lambda b, i: (b + first_seq, 0, 0)),
                  pl.BlockSpec((1, ROUTE_ROWS, tile), lambda b, i: (tile_of(b, i), 0, 0)),
                  _resident((1, D_MODEL)),
                  pl.BlockSpec(memory_space=pl.ANY)],
        out_specs=pl.BlockSpec((1, tile, D_MODEL), lambda b, i: (b, i, 0)),
        out_shape=jax.ShapeDtypeStruct((n_seq, SEQ, D_MODEL), F32),
        scratch_shapes=[pltpu.VMEM((2, TOPK, tile, D_PACKED), jnp.uint32), pltpu.SemaphoreType.DMA((2,))],
        compiler_params=_params(("arbitrary", "arbitrary")),
        name="final",
    )(dest, dest, x1, mod, gates, final_g, y_slots)


def kernel(x_prompt, x_sample, c_prompt, c_sample, w_ada, b_ada, norm1_g, w_in, w_pool, pool_scale,
           ssm_a_re_f, ssm_a_im_f, ssm_log_dt_f, ssm_b_re_f, ssm_b_im_f, ssm_c_re_f, ssm_c_im_f,
           ssm_a_re_b, ssm_a_im_b, ssm_log_dt_b, ssm_b_re_b, ssm_b_im_b, ssm_c_re_b, ssm_c_im_b,
           ssm_d, w_glu, b_glu, w_proj_a, w_proj_b, w_out, norm2_g,
           w_grp, b_grp, w_router, b_router, w_exp_gate, w_exp_up, w_exp_down, final_g):
    n_u = POOL_WIDTH + SSM_WIDTH
    c_pad = jnp.concatenate([c_prompt, c_sample, jnp.zeros((16 - N_SEQ, D_MODEL), F32)], axis=0)
    mod = _mod_call(c_pad, w_ada[0], b_ada).reshape(16, N_MOD, D_MODEL)

    w_in_bf = w_in[0].astype(BF16)
    u_a, u_b = _inproj_call(x_prompt, x_sample, mod, norm1_g, w_in_bf[:, :n_u])

    fwd = (ssm_a_re_f[0], ssm_a_im_f[0], ssm_log_dt_f[0], ssm_b_re_f[0], ssm_b_im_f[0], ssm_c_re_f[0], ssm_c_im_f[0])
    bwd = (ssm_a_re_b[0], ssm_a_im_b[0], ssm_log_dt_b[0], ssm_b_re_b[0], ssm_b_im_b[0], ssm_c_re_b[0], ssm_c_im_b[0])
    y_s5 = _s5_call(u_b, *_s5_operators(fwd, bwd, ssm_d[0]))

    w_r = jnp.concatenate([w_router[0], w_grp[0],
                           jnp.zeros((D_MODEL, ROUTER_COLS - N_GROUPS - N_EXPERTS), F32)], axis=1)
    b_r = jnp.concatenate([b_router[0], b_grp[0],
                           jnp.zeros((ROUTER_COLS - N_GROUPS - N_EXPERTS,), F32)])[None, :]
    w_r_hi = w_r.astype(BF16)
    w_r = jnp.concatenate([w_r_hi, (w_r - w_r_hi.astype(F32)).astype(BF16)], axis=1)
    x1, h2, route, gates, counts = _mix_call(
        x_prompt, x_sample, mod, norm1_g, norm2_g, u_a, y_s5, w_in_bf[:, n_u:], w_pool[0].astype(BF16),
        pool_scale, w_proj_a[0].astype(BF16), w_glu[0].astype(BF16), b_glu, w_proj_b[0].astype(BF16),
        w_out[0].astype(BF16), w_r, b_r)

    dest8, vblock, vexpert, vlo, vnext = _plan_call(counts[:, 0].astype(jnp.int32), route, counts)
    dest = dest8[:, :TOPK, :].reshape(N_TILES, 1, TOPK * TILE_MIX)
    x_slots = _dispatch_call(dest, h2.reshape(N_TOK, D_PACKED))
    y_slots = _expert_call(vblock, vexpert, vlo, vnext, x_slots, w_exp_gate[0], w_exp_up[0], w_exp_down[0])

    final_g2 = final_g[None, :]
    y_prompt = _final_call(x1, mod, y_slots, dest, gates, final_g2, 0, N_PROMPT)
    y_sample = _final_call(x1, mod, y_slots, dest, gates, final_g2, N_PROMPT, N_SAMPLE)
    return (y_prompt, y_sample)
```

```python
import math

import jax
import jax.numpy as jnp
from jax import lax
from jax.experimental import pallas as pl
from jax.experimental.pallas import tpu as pltpu

F32 = jnp.float32
BF16 = jnp.bfloat16

D_MODEL = 2048
SEQ = 4096
N_PROMPT = 2
N_SAMPLE = 8
N_SEQ = N_PROMPT + N_SAMPLE
N_TOK = N_SEQ * SEQ
EPS = 1e-6
N_MOD = 6

POOL_WINDOWS = (2, 4, 8, 16)
POOL_GROUP = 256
POOL_WIDTH = 1024
POOL_HALO = 16

SSM_GROUPS = 32
SSM_P = 16
SSM_N = 64
SSM_WIDTH = 512
SSM_CHUNK = 32
SSM_NCHUNK = SEQ // SSM_CHUNK
SSM_COLS = SSM_CHUNK * SSM_P
SSM_STATE_ROWS = 4 * SSM_N
SSM_GROUPS_PER_STEP = 8
SSM_SCAN_STEPS = 7
SSM_TAP_LANES = 2 * SSM_COLS

N_GROUPS = 4
EXPERTS_PER_GROUP = 8
N_EXPERTS = 32
TOPK = 2
D_EXPERT = 512
N_ASSIGN = N_TOK * TOPK
ROUTER_COLS = 128

TILE_INPROJ = 512
TILE_MIX = 256
N_TILES = N_TOK // TILE_MIX
TILES_PER_SEQ = SEQ // TILE_MIX
ROUTE_ROWS = 8
D_PACKED = D_MODEL // 2
MOE_ROWS = 256
MOE_SHIFT = 8
MOE_BLOCKS = N_ASSIGN // MOE_ROWS
N_VISITS = MOE_BLOCKS + N_EXPERTS - 1

VMEM_LIMIT = 60 * 1024 * 1024


def _params(sem, vmem=VMEM_LIMIT):
    return pltpu.CompilerParams(dimension_semantics=sem, vmem_limit_bytes=vmem)


def _resident(shape):
    zeros = (0,) * len(shape)
    return pl.BlockSpec(shape, lambda *_: zeros, pipeline_mode=pl.Buffered(1))


def _ada_norm(x, gain, scale, shift):
    r = lax.rsqrt(jnp.mean(x * x, axis=-1, keepdims=True) + EPS)
    return (x * r * gain) * (1.0 + scale) + shift


def _regroup_matrix(n_outer, n_inner):
    n = n_outer * n_inner
    dst = lax.broadcasted_iota(jnp.int32, (n, n), 0)
    src = lax.broadcasted_iota(jnp.int32, (n, n), 1)
    shift = n_outer.bit_length() - 1
    return (src == (dst & (n_outer - 1)) * n_inner + lax.shift_right_logical(dst, shift)).astype(BF16)


def _pack_bf16_pairs(x):
    w = x.shape[1] // 2
    lo = lax.bitcast_convert_type(x[:, :w].astype(BF16).astype(F32), jnp.uint32)
    hi = lax.bitcast_convert_type(x[:, w:].astype(BF16).astype(F32), jnp.uint32)
    return hi | (lo >> 16)


def _unpack_bf16_pairs(p):
    lo = lax.bitcast_convert_type(p << 16, F32)
    hi = lax.bitcast_convert_type(p & jnp.uint32(0xFFFF0000), F32)
    return lo, hi


def _mod_kernel(c_ref, w_ref, b_ref, o_ref):
    c = c_ref[...]
    s = c * jax.nn.sigmoid(c)
    o_ref[...] = jnp.dot(s.astype(BF16), w_ref[...].astype(BF16), preferred_element_type=F32) + b_ref[...]


def _mod_call(c_pad, w_ada, b_ada):
    n = w_ada.shape[1]
    tn = 1024
    return pl.pallas_call(
        _mod_kernel,
        grid=(n // tn,),
        in_specs=[pl.BlockSpec(c_pad.shape, lambda j: (0, 0)),
                  pl.BlockSpec((D_MODEL, tn), lambda j: (0, j)),
                  pl.BlockSpec((1, tn), lambda j: (0, j))],
        out_specs=pl.BlockSpec((c_pad.shape[0], tn), lambda j: (0, j)),
        out_shape=jax.ShapeDtypeStruct((c_pad.shape[0], n), F32),
        compiler_params=_params(("arbitrary",)),
        name="mod",
    )(c_pad, w_ada, b_ada)


def _x_specs(tile):
    last = SEQ // tile - 1
    xp = pl.BlockSpec((1, tile, D_MODEL),
                      lambda b, i: (jnp.minimum(b, N_PROMPT - 1), jnp.where(b < N_PROMPT, i, last), 0))
    xs = pl.BlockSpec((1, tile, D_MODEL),
                      lambda b, i: (jnp.maximum(b - N_PROMPT, 0), jnp.where(b < N_PROMPT, 0, i), 0))
    return xp, xs


def _load_x(xp_ref, xs_ref):
    return jnp.where(pl.program_id(0) < N_PROMPT, xp_ref[0], xs_ref[0])


def _inproj_kernel(xp_ref, xs_ref, mod_ref, g_ref, w_ref, ua_ref, ub_ref):
    x = _load_x(xp_ref, xs_ref)
    h = _ada_norm(x, g_ref[...], mod_ref[0, 1:2, :], mod_ref[0, 0:1, :])
    p = jnp.dot(h.astype(BF16), w_ref[...], preferred_element_type=F32)
    ua_ref[0] = p[:, :POOL_WIDTH].astype(BF16)
    n_chunk = TILE_INPROJ // SSM_CHUNK
    ub = jnp.dot(_regroup_matrix(n_chunk, SSM_CHUNK), p[:, POOL_WIDTH:].astype(BF16), preferred_element_type=F32)
    ub_ref[...] = ub.reshape(SSM_CHUNK, n_chunk, SSM_WIDTH).astype(BF16)


def _inproj_call(x_prompt, x_sample, mod, norm_g, w_u):
    tile = TILE_INPROJ
    xp, xs = _x_specs(tile)
    return pl.pallas_call(
        _inproj_kernel,
        grid=(N_SEQ, SEQ // tile),
        in_specs=[xp, xs,
                  pl.BlockSpec((1, N_MOD, D_MODEL), lambda b, i: (b, 0, 0)),
                  _resident((1, D_MODEL)),
                  _resident(w_u.shape)],
        out_specs=[pl.BlockSpec((1, tile, POOL_WIDTH), lambda b, i: (b, i, 0)),
                   pl.BlockSpec((None, SSM_CHUNK, tile // SSM_CHUNK, SSM_WIDTH), lambda b, i: (b, 0, i, 0))],
        out_shape=[jax.ShapeDtypeStruct((N_SEQ, SEQ, POOL_WIDTH), BF16),
                   jax.ShapeDtypeStruct((N_SEQ, SSM_CHUNK, SSM_NCHUNK, SSM_WIDTH), BF16)],
        compiler_params=_params(("arbitrary", "arbitrary")),
        name="inproj",
    )(x_prompt, x_sample, mod, norm_g, w_u)


def _s5_kernel(u_ref, wst_ref, taps_ref, wot_ref, a_ref, y_ref, ut_ref, yt_ref, apow_ref, tt_ref):
    n, nc, ng = SSM_N, SSM_NCHUNK, SSM_GROUPS_PER_STEP

    @pl.when(pl.program_id(1) == 0)
    def _():
        for g in range(ng):
            a = a_ref[g]
            for k in range(SSM_SCAN_STEPS):
                apow_ref[g, k] = a
                f_re, f_im, b_re, b_im = a[0:n], a[n:2 * n], a[2 * n:3 * n], a[3 * n:4 * n]
                a = jnp.concatenate([f_re * f_re - f_im * f_im, 2.0 * (f_re * f_im),
                                     b_re * b_re - b_im * b_im, 2.0 * (b_re * b_im)], axis=0)
            strip = taps_ref[g]
            for t in range(SSM_CHUNK):
                start = (SSM_CHUNK - 1 - t) * SSM_P
                window = pltpu.roll(strip, SSM_TAP_LANES - start, 1) if start else strip
                tt_ref[g, pl.ds(t * SSM_P, SSM_P), :] = window[:, :SSM_COLS].astype(BF16)

    for s in range(SSM_CHUNK):
        blk = u_ref[s].astype(F32).T
        ut_ref[:, pl.ds(s * SSM_P, SSM_P), :] = blk.reshape(ng, SSM_P, nc).astype(BF16)

    lane = lax.broadcasted_iota(jnp.int32, (n, nc), 1)

    def shifted(x, d, forward):
        if forward:
            return jnp.where(lane >= d, pltpu.roll(x, d, 1), 0.0)
        return jnp.where(lane < nc - d, pltpu.roll(x, nc - d, 1), 0.0)

    def scan(x_re, x_im, g, re_row, im_row, forward):
        for k in range(SSM_SCAN_STEPS):
            a_re = apow_ref[g, k, pl.ds(re_row, n), :]
            a_im = apow_ref[g, k, pl.ds(im_row, n), :]
            s_re = shifted(x_re, 2 ** k, forward)
            s_im = shifted(x_im, 2 ** k, forward)
            x_re, x_im = x_re + (a_re * s_re - a_im * s_im), x_im + (a_re * s_im + a_im * s_re)
        return x_re, x_im

    def per_group(g, carry):
        ut = ut_ref[g]
        st = jnp.dot(wst_ref[g], ut, preferred_element_type=F32)
        f_re, f_im = scan(st[0:n], st[n:2 * n], g, 0, n, True)
        b_re, b_im = scan(st[2 * n:3 * n], st[3 * n:4 * n], g, 2 * n, 3 * n, False)
        carried = jnp.concatenate([shifted(f_re, 1, True), shifted(f_im, 1, True),
                                   shifted(b_re, 1, False), shifted(b_im, 1, False)], axis=0)
        yt_ref[g] = (jnp.dot(tt_ref[g], ut, preferred_element_type=F32)
                     + jnp.dot(wot_ref[g], carried.astype(BF16), preferred_element_type=F32))
        return carry

    lax.fori_loop(0, ng, per_group, 0, unroll=2)
    for t in range(SSM_CHUNK):
        y_ref[t] = yt_ref[:, pl.ds(t * SSM_P, SSM_P), :].reshape(ng * SSM_P, nc).T


def _s5_call(u_ph, wst, tap_strip, wot, a_chunk):
    ng = SSM_GROUPS_PER_STEP
    seq = pl.BlockSpec((None, SSM_CHUNK, SSM_NCHUNK, ng * SSM_P), lambda q, b: (b, 0, 0, q))
    mat = lambda rows, cols: pl.BlockSpec((ng, rows, cols), lambda q, b: (q, 0, 0))
    return pl.pallas_call(
        _s5_kernel,
        grid=(SSM_GROUPS // ng, N_SEQ),
        in_specs=[seq, mat(SSM_STATE_ROWS, SSM_COLS), mat(SSM_P, SSM_TAP_LANES), mat(SSM_COLS, SSM_STATE_ROWS),
                  mat(SSM_STATE_ROWS, SSM_NCHUNK)],
        out_specs=seq,
        out_shape=jax.ShapeDtypeStruct((N_SEQ, SSM_CHUNK, SSM_NCHUNK, SSM_WIDTH), F32),
        scratch_shapes=[pltpu.VMEM((ng, SSM_COLS, SSM_NCHUNK), BF16),
                        pltpu.VMEM((ng, SSM_COLS, SSM_NCHUNK), F32),
                        pltpu.VMEM((ng, SSM_SCAN_STEPS, SSM_STATE_ROWS, SSM_NCHUNK), F32),
                        pltpu.VMEM((ng, SSM_COLS, SSM_COLS), BF16)],
        compiler_params=_params(("arbitrary", "arbitrary")),
        name="s5",
    )(u_ph, wst, tap_strip, wot, a_chunk)


def _s5_direction(a_re, a_im, log_dt, b_re, b_im, c_re, c_im):
    dt = jnp.exp(log_dt)[:, None]
    k = jnp.arange(SSM_CHUNK + 1, dtype=F32)[None, :, None]
    mag = jnp.exp(k * (a_re * dt)[:, None, :])
    ang = k * (a_im * dt)[:, None, :]
    pw_re = mag * jnp.cos(ang)
    pw_im = mag * jnp.sin(ang)
    ab_re, ab_im = pw_re[:, 1], pw_im[:, 1]
    den = a_re * a_re + a_im * a_im
    q_re = ((ab_re - 1.0) * a_re + ab_im * a_im) / den
    q_im = (ab_im * a_re - (ab_re - 1.0) * a_im) / den
    bb_re = q_re[:, :, None] * b_re - q_im[:, :, None] * b_im
    bb_im = q_re[:, :, None] * b_im + q_im[:, :, None] * b_re
    cp_re = c_re[:, None] * pw_re[:, :, None, :] - c_im[:, None] * pw_im[:, :, None, :]
    cp_im = c_re[:, None] * pw_im[:, :, None, :] + c_im[:, None] * pw_re[:, :, None, :]
    taps = jnp.einsum('gkqn,gnp->gkqp', jnp.concatenate([cp_re[:, :SSM_CHUNK], -cp_im[:, :SSM_CHUNK]], axis=-1),
                      jnp.concatenate([bb_re, bb_im], axis=1), precision=lax.Precision.HIGH)
    return pw_re, pw_im, bb_re, bb_im, cp_re, cp_im, taps


def _s5_operators(fwd, bwd, ssm_d):
    g, q, p, n = SSM_GROUPS, SSM_CHUNK, SSM_P, SSM_N
    pf_re, pf_im, bf_re, bf_im, cf_re, cf_im, taps_f = _s5_direction(*fwd)
    pb_re, pb_im, bb_re, bb_im, cb_re, cb_im, taps_b = _s5_direction(*bwd)
    descending, ascending = slice(q - 1, None, -1), slice(0, q)
    from_one, down_to_one = slice(1, q + 1), slice(q, 0, -1)

    centre = taps_f[:, 0] + taps_b[:, 0] + jnp.eye(p, dtype=F32)[None] * ssm_d.reshape(g, 1, p)
    by_lag = jnp.concatenate([taps_b[:, :0:-1], centre[:, None], taps_f[:, 1:]], axis=1)
    strip = by_lag[:, ::-1].transpose(0, 2, 1, 3).reshape(g, p, (2 * q - 1) * p)
    strip = jnp.pad(strip, ((0, 0), (0, 0), (0, SSM_TAP_LANES - (2 * q - 1) * p)))

    col = jnp.arange(q * p)
    rep_s = (col[None, :] // p == jnp.arange(q)[:, None]).astype(F32)
    tile_p = (col[None, :] % p == jnp.arange(p)[:, None]).astype(F32)
    expand = lambda x, m: jnp.einsum('gnk,kl->gnl', x, m, precision=lax.Precision.HIGH)

    def state_in(pw_re, pw_im, b_re, b_im, powers):
        a_re = expand(pw_re[:, powers].transpose(0, 2, 1), rep_s)
        a_im = expand(pw_im[:, powers].transpose(0, 2, 1), rep_s)
        b_re, b_im = expand(b_re, tile_p), expand(b_im, tile_p)
        return a_re * b_re - a_im * b_im, a_re * b_im + a_im * b_re

    wsf_re, wsf_im = state_in(pf_re[:, :q], pf_im[:, :q], bf_re, bf_im, descending)
    wsb_re, wsb_im = state_in(pb_re, pb_im, bb_re, bb_im, ascending)
    wst = jnp.concatenate([wsf_re, wsf_im, wsb_re, wsb_im], axis=1)

    def state_out(cp_re, cp_im, powers):
        o_re = cp_re[:, powers].reshape(g, q * p, n)
        o_im = -cp_im[:, powers].reshape(g, q * p, n)
        return o_re, o_im

    of_re, of_im = state_out(cf_re, cf_im, from_one)
    ob_re, ob_im = state_out(cb_re, cb_im, down_to_one)
    wot = jnp.concatenate([of_re, of_im, ob_re, ob_im], axis=2)
    a_chunk = jnp.concatenate([pf_re[:, q], pf_im[:, q], pb_re[:, q], pb_im[:, q]], axis=1)
    a_chunk = jnp.broadcast_to(a_chunk[:, :, None], (g, SSM_STATE_ROWS, SSM_NCHUNK))
    return wst.astype(BF16), strip, wot.astype(BF16), a_chunk


def _gelu_tanh(x):
    return 0.5 * x * (1.0 + jnp.tanh(math.sqrt(2.0 / math.pi) * (x + 0.044715 * (x * x * x))))


def _route_tile(logits, run_ref, route_ref, gate_ref, counts_ref):
    tile = logits.shape[0]
    neg = -jnp.inf
    lt = logits.T
    row8 = lax.broadcasted_iota(jnp.int32, (EXPERTS_PER_GROUP, tile), 0)
    gl = jnp.where(row8 < N_GROUPS, lt[N_EXPERTS:N_EXPERTS + 8], neg)
    gmax = jnp.max(gl, axis=0, keepdims=True)
    g_sel = jnp.min(jnp.where(gl == gmax, row8, 8), axis=0, keepdims=True)
    g_p = 1.0 / jnp.sum(jnp.exp(gl - gmax), axis=0, keepdims=True)
    in_grp = lt[0:EXPERTS_PER_GROUP]
    for g in range(1, N_GROUPS):
        in_grp = jnp.where(g_sel == g, lt[g * EXPERTS_PER_GROUP:(g + 1) * EXPERTS_PER_GROUP], in_grp)
    m1 = jnp.max(in_grp, axis=0, keepdims=True)
    i1 = jnp.min(jnp.where(in_grp == m1, row8, 8), axis=0, keepdims=True)
    rest = jnp.where(row8 == i1, neg, in_grp)
    m2 = jnp.max(rest, axis=0, keepdims=True)
    i2 = jnp.min(jnp.where(rest == m2, row8, 8), axis=0, keepdims=True)
    e21 = jnp.exp(m2 - m1)
    p1 = 1.0 / (1.0 + e21)
    eid1 = g_sel * EXPERTS_PER_GROUP + i1
    eid2 = g_sel * EXPERTS_PER_GROUP + i2

    row_e = lax.broadcasted_iota(jnp.int32, (N_EXPERTS, tile), 0)
    oh1 = (row_e == eid1).astype(F32)
    oh2 = (row_e == eid2).astype(F32)
    earlier = (lax.broadcasted_iota(jnp.int32, (tile, tile), 0)
               < lax.broadcasted_iota(jnp.int32, (tile, tile), 1)).astype(BF16)
    before1 = jnp.dot(oh1.astype(BF16), earlier, preferred_element_type=F32)
    before2 = jnp.dot(oh2.astype(BF16), earlier, preferred_element_type=F32)
    tot1 = jnp.sum(oh1, axis=1, keepdims=True)
    tot2 = jnp.sum(oh2, axis=1, keepdims=True)
    run = run_ref[:, 0:1]
    rank1 = jnp.sum(oh1 * (before1 + run), axis=0, keepdims=True)
    rank2 = jnp.sum(oh2 * (before2 + (run + tot1)), axis=0, keepdims=True)
    new_run = jnp.broadcast_to(run + tot1 + tot2, run_ref.shape)
    run_ref[...] = new_run
    counts_ref[...] = new_run
    zi = jnp.zeros((ROUTE_ROWS - 4, tile), jnp.int32)
    route_ref[0] = jnp.concatenate([eid1, eid2, rank1.astype(jnp.int32), rank2.astype(jnp.int32), zi], axis=0)
    zf = jnp.zeros((ROUTE_ROWS - 2, tile), F32)
    gate_ref[0] = jnp.concatenate([g_p * p1, g_p * (e21 * p1), zf], axis=0)


def _mix_kernel(xp_ref, xs_ref, mod_ref, g1_ref, g2_ref, ua_ref, ua_prev_ref, ua_next_ref, ys_ref,
                wg_ref, wpool_ref, pscale_ref, wpa_ref, wglu_ref, bglu_ref, wpb_ref, wout_ref,
                wr_ref, br_ref,
                x1_ref, h2_ref, route_ref, gate_ref, counts_ref, ext_ref, diff_ref, merged_ref, run_ref):
    tile = TILE_MIX
    i = pl.program_id(1)

    @pl.when((pl.program_id(0) == 0) & (i == 0))
    def _():
        run_ref[...] = jnp.zeros_like(run_ref)

    x = _load_x(xp_ref, xs_ref)
    h = _ada_norm(x, g1_ref[...], mod_ref[0, 1:2, :], mod_ref[0, 0:1, :]).astype(BF16)

    first = i == 0
    last = i == pl.num_programs(1) - 1
    ext_ref[pl.ds(0, POOL_HALO), :] = jnp.where(first, 0.0, ua_prev_ref[0].astype(F32))
    ext_ref[pl.ds(POOL_HALO, tile), :] = ua_ref[0].astype(F32)
    ext_ref[pl.ds(POOL_HALO + tile, POOL_HALO), :] = jnp.where(last, 0.0, ua_next_ref[0].astype(F32))
    pos = i * tile + lax.broadcasted_iota(jnp.int32, (tile, 1), 0)
    for k, w in enumerate(POOL_WINDOWS):
        cols = pl.ds(k * POOL_GROUP, POOL_GROUP)
        lo = jnp.maximum(pos - w // 2, 0)
        hi = jnp.minimum(pos + (w - 1 - w // 2), SEQ - 1)
        inv_cnt = 1.0 / (hi - lo + 1).astype(F32)
        acc = ext_ref[pl.ds(POOL_HALO - w // 2, tile), cols]
        for j in range(1, w):
            acc = acc + ext_ref[pl.ds(POOL_HALO - w // 2 + j, tile), cols]
        diff = acc * inv_cnt - ext_ref[pl.ds(POOL_HALO, tile), cols]
        mixed = jnp.dot(diff.astype(BF16), wpool_ref[k], preferred_element_type=F32)
        diff_ref[:, cols] = (mixed * pscale_ref[:, cols]).astype(BF16)

    z = _gelu_tanh(ys_ref[...].reshape(tile, SSM_WIDTH))
    zg = z * jax.nn.sigmoid(jnp.dot(z.astype(BF16), wglu_ref[...], preferred_element_type=F32) + bglu_ref[...])
    zg = jnp.dot(_regroup_matrix(SSM_CHUNK, tile // SSM_CHUNK), zg.astype(BF16),
                 preferred_element_type=F32).astype(BF16)
    pa = diff_ref[...]

    chunk = 1024
    for j in range(D_MODEL // chunk):
        c0 = j * chunk
        g_a = jnp.dot(h, wg_ref[:, pl.ds(c0, chunk)], preferred_element_type=F32)
        y_a = jnp.dot(pa, wpa_ref[:, pl.ds(c0, chunk)], preferred_element_type=F32)
        m = jax.nn.sigmoid(g_a) * y_a
        g_b = jnp.dot(h, wg_ref[:, pl.ds(D_MODEL + c0, chunk)], preferred_element_type=F32)
        y_b = jnp.dot(zg, wpb_ref[:, pl.ds(c0, chunk)], preferred_element_type=F32)
        m = m + jax.nn.sigmoid(g_b) * y_b
        merged_ref[:, pl.ds(c0, chunk)] = m.astype(BF16)

    x1 = x + mod_ref[0, 2:3, :] * jnp.dot(merged_ref[...], wout_ref[...], preferred_element_type=F32)
    x1_ref[0] = x1
    h2 = _ada_norm(x1, g2_ref[...], mod_ref[0, 4:5, :], mod_ref[0, 3:4, :])
    h2_ref[0] = _pack_bf16_pairs(h2)
    h2_hi = h2.astype(BF16)
    h2_lo = (h2 - h2_hi.astype(F32)).astype(BF16)
    prod = jnp.dot(jnp.concatenate([h2_hi, h2_lo], axis=0), wr_ref[...], preferred_element_type=F32)
    logits = (prod[:tile, :ROUTER_COLS]
              + (prod[tile:, :ROUTER_COLS] + prod[:tile, ROUTER_COLS:])) + br_ref[...]
    _route_tile(logits, run_ref, route_ref, gate_ref, counts_ref)


def _mix_call(x_prompt, x_sample, mod, norm1_g, norm2_g, u_a, y_s5, w_g, w_pool, pool_scale, w_pa,
              w_glu, b_glu, w_pb, w_out, w_r, b_r):
    tile = TILE_MIX
    xp, xs = _x_specs(tile)
    halo_per_tile = tile // POOL_HALO
    n_halo = SEQ // POOL_HALO
    seq_tile = lambda width: pl.BlockSpec((1, tile, width), lambda b, i: (b, i, 0))
    return pl.pallas_call(
        _mix_kernel,
        grid=(N_SEQ, SEQ // tile),
        in_specs=[xp, xs,
                  pl.BlockSpec((1, N_MOD, D_MODEL), lambda b, i: (b, 0, 0)),
                  _resident((1, D_MODEL)), _resident((1, D_MODEL)),
                  seq_tile(POOL_WIDTH),
                  pl.BlockSpec((1, POOL_HALO, POOL_WIDTH),
                               lambda b, i: (b, jnp.maximum(i * halo_per_tile - 1, 0), 0)),
                  pl.BlockSpec((1, POOL_HALO, POOL_WIDTH),
                               lambda b, i: (b, jnp.minimum((i + 1) * halo_per_tile, n_halo - 1), 0)),
                  pl.BlockSpec((None, SSM_CHUNK, tile // SSM_CHUNK, SSM_WIDTH), lambda b, i: (b, 0, i, 0)),
                  _resident(w_g.shape), _resident(w_pool.shape), _resident(pool_scale.shape),
                  _resident(w_pa.shape), _resident(w_glu.shape), _resident(b_glu.shape),
                  _resident(w_pb.shape), _resident(w_out.shape), _resident(w_r.shape), _resident(b_r.shape)],
        out_specs=[seq_tile(D_MODEL), seq_tile(D_PACKED),
                   pl.BlockSpec((1, ROUTE_ROWS, tile), lambda b, i: (b * TILES_PER_SEQ + i, 0, 0)),
                   pl.BlockSpec((1, ROUTE_ROWS, tile), lambda b, i: (b * TILES_PER_SEQ + i, 0, 0)),
                   pl.BlockSpec((N_EXPERTS, 128), lambda b, i: (0, 0))],
        out_shape=[jax.ShapeDtypeStruct((N_SEQ, SEQ, D_MODEL), F32),
                   jax.ShapeDtypeStruct((N_SEQ, SEQ, D_PACKED), jnp.uint32),
                   jax.ShapeDtypeStruct((N_TILES, ROUTE_ROWS, tile), jnp.int32),
                   jax.ShapeDtypeStruct((N_TILES, ROUTE_ROWS, tile), F32),
                   jax.ShapeDtypeStruct((N_EXPERTS, 128), F32)],
        scratch_shapes=[pltpu.VMEM((tile + 2 * POOL_HALO, POOL_WIDTH), F32),
                        pltpu.VMEM((tile, POOL_WIDTH), BF16),
                        pltpu.VMEM((tile, D_MODEL), BF16),
                        pltpu.VMEM((N_EXPERTS, 128), F32)],
        compiler_params=_params(("arbitrary", "arbitrary")),
        name="mix",
    )(x_prompt, x_sample, mod, norm1_g, norm2_g, u_a, u_a, u_a, y_s5, w_g, w_pool, pool_scale, w_pa,
      w_glu, b_glu, w_pb, w_out, w_r, b_r)


def _plan_kernel(cnt_ref, route_ref, counts_ref, dest_ref, vblock_ref, vexpert_ref, vlo_ref, vnext_ref):
    below = (lax.broadcasted_iota(jnp.int32, (N_EXPERTS, N_EXPERTS), 1)
             < lax.broadcasted_iota(jnp.int32, (N_EXPERTS, N_EXPERTS), 0)).astype(F32)
    starts = jnp.dot(below, counts_ref[...], preferred_element_type=F32, precision=lax.Precision.HIGHEST)
    starts_b = jnp.broadcast_to(starts[:, 0:1], (N_EXPERTS, TILE_MIX))
    row_e = lax.broadcasted_iota(jnp.int32, (N_EXPERTS, TILE_MIX), 0)
    zi = jnp.zeros((ROUTE_ROWS - 2, TILE_MIX), jnp.int32)

    def per_tile(t, carry):
        r = route_ref[t]
        s1 = jnp.sum(jnp.where(row_e == r[0:1], starts_b, 0.0), axis=0, keepdims=True)
        s2 = jnp.sum(jnp.where(row_e == r[1:2], starts_b, 0.0), axis=0, keepdims=True)
        dest_ref[t] = jnp.concatenate([s1.astype(jnp.int32) + r[2:3], s2.astype(jnp.int32) + r[3:4], zi], axis=0)
        return carry

    lax.fori_loop(0, N_TILES, per_tile, 0)

    def per_expert(e, carry):
        v, start, last_e = carry
        cnt = cnt_ref[e]
        end = start + cnt
        first = lax.shift_right_logical(start, MOE_SHIFT)
        n_blk = jnp.where(cnt > 0, lax.shift_right_logical(end - 1, MOE_SHIFT) - first + 1, 0)

        def per_block(k, v):
            blk = first + k
            vblock_ref[v] = blk
            vexpert_ref[v] = e
            vlo_ref[v] = jnp.maximum(start - blk * MOE_ROWS, 0)
            return v + 1

        v = lax.fori_loop(0, n_blk, per_block, v)
        return v, end, jnp.where(cnt > 0, e, last_e)

    v, _, last_e = lax.fori_loop(0, N_EXPERTS, per_expert, (0, 0, 0))

    def idle(k, carry):
        vblock_ref[k] = MOE_BLOCKS - 1
        vexpert_ref[k] = last_e
        vlo_ref[k] = MOE_ROWS
        return carry

    lax.fori_loop(v, N_VISITS, idle, 0)

    def following(i, carry):
        nxt, later = carry
        k = N_VISITS - 1 - i
        e = vexpert_ref[k]
        nxt = jnp.where(e != later, later, nxt)
        vnext_ref[k] = nxt
        return nxt, e

    lax.fori_loop(0, N_VISITS, following, (-1, -1))


def _plan_call(cnt, route, counts):
    smem = pl.BlockSpec(memory_space=pltpu.SMEM)
    vmem = pl.BlockSpec(memory_space=pltpu.VMEM)
    visits = jax.ShapeDtypeStruct((N_VISITS,), jnp.int32)
    return pl.pallas_call(
        _plan_kernel,
        in_specs=[smem, vmem, vmem],
        out_specs=[vmem, smem, smem, smem, smem],
        out_shape=[jax.ShapeDtypeStruct((N_TILES, ROUTE_ROWS, TILE_MIX), jnp.int32), visits, visits, visits, visits],
        name="plan",
    )(cnt, route, counts)


def _dest_spec(index_map):
    return pl.BlockSpec((1, 1, TOPK * TILE_MIX), index_map, memory_space=pltpu.SMEM)


DISPATCH_SLOTS = 3


def _dispatch_kernel(dest_ref, h_ref, xs_ref, buf, in_sem, out_sem):
    tile = TILE_MIX
    t = pl.program_id(0)
    n = pl.num_programs(0)
    slot = lax.rem(t, DISPATCH_SLOTS)
    nxt = lax.rem(t + 1, DISPATCH_SLOTS)

    def load(i, s):
        return pltpu.make_async_copy(h_ref.at[pl.ds(i * tile, tile)], buf.at[s], in_sem.at[s])

    def drain(s):
        for k in range(TOPK):
            pltpu.make_async_copy(buf.at[s], xs_ref.at[pl.ds(0, tile)], out_sem.at[s]).wait()

    @pl.when(t == 0)
    def _():
        load(0, 0).start()

    @pl.when(t >= DISPATCH_SLOTS - 1)
    def _():
        drain(nxt)

    @pl.when(t + 1 < n)
    def _():
        load(t + 1, nxt).start()

    load(t, slot).wait()

    def row(r, carry):
        for k in range(TOPK):
            d = dest_ref[0, 0, k * tile + r]
            pltpu.make_async_copy(buf.at[slot, pl.ds(r, 1)], xs_ref.at[pl.ds(d, 1)], out_sem.at[slot]).start()
        return carry

    lax.fori_loop(0, tile, row, 0, unroll=8)

    @pl.when(t == n - 1)
    def _():
        drain(lax.rem(t + DISPATCH_SLOTS - 1, DISPATCH_SLOTS))
        drain(slot)


def _dispatch_call(dest, h2):
    return pl.pallas_call(
        _dispatch_kernel,
        grid=(N_TILES,),
        in_specs=[_dest_spec(lambda t: (t, 0, 0)), pl.BlockSpec(memory_space=pl.ANY)],
        out_specs=pl.BlockSpec(memory_space=pl.ANY),
        out_shape=jax.ShapeDtypeStruct((N_ASSIGN, D_PACKED), jnp.uint32),
        scratch_shapes=[pltpu.VMEM((DISPATCH_SLOTS, TILE_MIX, D_PACKED), jnp.uint32),
                        pltpu.SemaphoreType.DMA((DISPATCH_SLOTS,)),
                        pltpu.SemaphoreType.DMA((DISPATCH_SLOTS,))],
        compiler_params=_params(("arbitrary",)),
        name="dispatch",
    )(dest, h2)


def _expert_kernel(vblock_ref, vexpert_ref, vlo_ref, vnext_ref, x_ref, wg_ref, wu_ref, wd_ref, o_ref,
                   wgu_s, wd_s, gbuf, ubuf, dbuf, run_ref, sem):
    v = pl.program_id(0)
    e = vexpert_ref[v]

    def weight_copies(expert, slot):
        return [pltpu.make_async_copy(src.at[expert], dst.at[slot], sem.at[slot])
                for src, dst in ((wg_ref, gbuf), (wu_ref, ubuf), (wd_ref, dbuf))]

    @pl.when(v == 0)
    def _():
        run_ref[0] = 0
        for c in weight_copies(e, 0):
            c.start()

    @pl.when((v == 0) | (e != vexpert_ref[jnp.maximum(v - 1, 0)]))
    def _():
        slot = run_ref[0] & 1
        run_ref[0] = run_ref[0] + 1
        for c in weight_copies(e, slot):
            c.wait()
        nxt = vnext_ref[v]

        @pl.when(nxt >= 0)
        def _():
            for c in weight_copies(nxt, 1 - slot):
                c.start()

        wgu_s[:, :D_EXPERT] = gbuf[slot].astype(BF16)
        wgu_s[:, D_EXPERT:] = ubuf[slot].astype(BF16)
        wd_s[...] = dbuf[slot].astype(BF16)

    lo = vlo_ref[v]

    @pl.when(lo < MOE_ROWS)
    def _():
        x_lo, x_hi = _unpack_bf16_pairs(x_ref[...])
        gu = (jnp.dot(x_lo.astype(BF16), wgu_s[pl.ds(0, D_PACKED), :], preferred_element_type=F32)
              + jnp.dot(x_hi.astype(BF16), wgu_s[pl.ds(D_PACKED, D_PACKED), :], preferred_element_type=F32))
        g = gu[:, :D_EXPERT]
        act = (g * jax.nn.sigmoid(g)) * gu[:, D_EXPERT:]
        res = _pack_bf16_pairs(jnp.dot(act.astype(BF16), wd_s[...], preferred_element_type=F32))

        @pl.when(lo == 0)
        def _():
            o_ref[...] = res

        @pl.when(lo > 0)
        def _():
            rows = lax.broadcasted_iota(jnp.int32, (MOE_ROWS, 1), 0)
            o_ref[...] = jnp.where(rows >= lo, res, o_ref[...])


def _expert_call(vblock, vexpert, vlo, vnext, x_slots, w_gate, w_up, w_down):
    hbm = pl.BlockSpec(memory_space=pl.ANY)
    grid_spec = pltpu.PrefetchScalarGridSpec(
        num_scalar_prefetch=4,
        grid=(N_VISITS,),
        in_specs=[pl.BlockSpec((MOE_ROWS, D_PACKED), lambda v, vb, ve, vl, vn: (vb[v], 0)), hbm, hbm, hbm],
        out_specs=pl.BlockSpec((MOE_ROWS, D_PACKED), lambda v, vb, ve, vl, vn: (vb[v], 0)),
        scratch_shapes=[pltpu.VMEM((D_MODEL, 2 * D_EXPERT), BF16),
                        pltpu.VMEM((D_EXPERT, D_MODEL), BF16),
                        pltpu.VMEM((2, D_MODEL, D_EXPERT), F32),
                        pltpu.VMEM((2, D_MODEL, D_EXPERT), F32),
                        pltpu.VMEM((2, D_EXPERT, D_MODEL), F32),
                        pltpu.SMEM((1,), jnp.int32),
                        pltpu.SemaphoreType.DMA((2,))],
    )
    return pl.pallas_call(
        _expert_kernel,
        grid_spec=grid_spec,
        out_shape=jax.ShapeDtypeStruct((N_ASSIGN, D_PACKED), jnp.uint32),
        compiler_params=_params(("arbitrary",)),
        name="experts",
    )(vblock, vexpert, vlo, vnext, x_slots, w_gate, w_up, w_down)


def _final_kernel(dest_ref, dest_next_ref, x1_ref, mod_ref, gate_ref, g_ref, y_ref, o_ref, rows_ref, sem):
    tile = TILE_MIX
    step = pl.program_id(0) * pl.num_programs(1) + pl.program_id(1)
    n_steps = pl.num_programs(0) * pl.num_programs(1)
    slot = lax.rem(step, 2)

    def fetch(dst_ref, into):
        def row(r, carry):
            for k in range(TOPK):
                d = dst_ref[0, 0, k * tile + r]
                pltpu.make_async_copy(y_ref.at[pl.ds(d, 1)], rows_ref.at[into, k, pl.ds(r, 1)], sem.at[into]).start()
            return carry

        lax.fori_loop(0, tile, row, 0, unroll=8)

    @pl.when(step == 0)
    def _():
        fetch(dest_ref, 0)

    @pl.when(step + 1 < n_steps)
    def _():
        fetch(dest_next_ref, 1 - slot)

    for k in range(TOPK):
        pltpu.make_async_copy(y_ref.at[pl.ds(0, tile)], rows_ref.at[slot, k], sem.at[slot]).wait()
    gate = gate_ref[0].T
    lo0, hi0 = _unpack_bf16_pairs(rows_ref[slot, 0])
    lo1, hi1 = _unpack_bf16_pairs(rows_ref[slot, 1])
    moe = jnp.concatenate([gate[:, 0:1] * lo0 + gate[:, 1:2] * lo1, gate[:, 0:1] * hi0 + gate[:, 1:2] * hi1], axis=1)
    x2 = x1_ref[0] + mod_ref[0, 5:6, :] * moe
    r = lax.rsqrt(jnp.mean(x2 * x2, axis=-1, keepdims=True) + EPS)
    o_ref[0] = x2 * r * g_ref[...]


def _final_call(x1, mod, y_slots, dest, gates, final_g, first_seq, n_seq):
    tile = TILE_MIX
    tile_of = lambda b, i: (b + first_seq) * TILES_PER_SEQ + i
    last_tile = (first_seq + n_seq) * TILES_PER_SEQ - 1
    return pl.pallas_call(
        _final_kernel,
        grid=(n_seq, TILES_PER_SEQ),
        in_specs=[_dest_spec(lambda b, i: (tile_of(b, i), 0, 0)),
                  _dest_spec(lambda b, i: (jnp.minimum(tile_of(b, i) + 1, last_tile), 0, 0)),
                  pl.BlockSpec((1, tile, D_MODEL), lambda b, i: (b + first_seq, i, 0)),
                  pl.BlockSpec((1, N_MOD, D_MODEL), lambda b, i: (b + first_seq, 0, 0)),
                  pl.BlockSpec((1, ROUTE_ROWS, tile), lambda b, i: (tile_of(b, i), 0, 0)),
                  _resident((1, D_MODEL)),
                  pl.BlockSpec(memory_space=pl.ANY)],
        out_specs=pl.BlockSpec((1, tile, D_MODEL), lambda b, i: (b, i, 0)),
        out_shape=jax.ShapeDtypeStruct((n_seq, SEQ, D_MODEL), F32),
        scratch_shapes=[pltpu.VMEM((2, TOPK, tile, D_PACKED), jnp.uint32), pltpu.SemaphoreType.DMA((2,))],
        compiler_params=_params(("arbitrary", "arbitrary")),
        name="final",
    )(dest, dest, x1, mod, gates, final_g, y_slots)


def kernel(x_prompt, x_sample, c_prompt, c_sample, w_ada, b_ada, norm1_g, w_in, w_pool, pool_scale,
           ssm_a_re_f, ssm_a_im_f, ssm_log_dt_f, ssm_b_re_f, ssm_b_im_f, ssm_c_re_f, ssm_c_im_f,
           ssm_a_re_b, ssm_a_im_b, ssm_log_dt_b, ssm_b_re_b, ssm_b_im_b, ssm_c_re_b, ssm_c_im_b,
           ssm_d, w_glu, b_glu, w_proj_a, w_proj_b, w_out, norm2_g,
           w_grp, b_grp, w_router, b_router, w_exp_gate, w_exp_up, w_exp_down, final_g):
    n_u = POOL_WIDTH + SSM_WIDTH
    c_pad = jnp.concatenate([c_prompt, c_sample, jnp.zeros((16 - N_SEQ, D_MODEL), F32)], axis=0)
    mod = _mod_call(c_pad, w_ada[0], b_ada).reshape(16, N_MOD, D_MODEL)

    w_in_bf = w_in[0].astype(BF16)
    u_a, u_b = _inproj_call(x_prompt, x_sample, mod, norm1_g, w_in_bf[:, :n_u])

    fwd = (ssm_a_re_f[0], ssm_a_im_f[0], ssm_log_dt_f[0], ssm_b_re_f[0], ssm_b_im_f[0], ssm_c_re_f[0], ssm_c_im_f[0])
    bwd = (ssm_a_re_b[0], ssm_a_im_b[0], ssm_log_dt_b[0], ssm_b_re_b[0], ssm_b_im_b[0], ssm_c_re_b[0], ssm_c_im_b[0])
    y_s5 = _s5_call(u_b, *_s5_operators(fwd, bwd, ssm_d[0]))

    w_r = jnp.concatenate([w_router[0], w_grp[0],
                           jnp.zeros((D_MODEL, ROUTER_COLS - N_GROUPS - N_EXPERTS), F32)], axis=1)
    b_r = jnp.concatenate([b_router[0], b_grp[0],
                           jnp.zeros((ROUTER_COLS - N_GROUPS - N_EXPERTS,), F32)])[None, :]
    w_r_hi = w_r.astype(BF16)
    w_r = jnp.concatenate([w_r_hi, (w_r - w_r_hi.astype(F32)).astype(BF16)], axis=1)
    x1, h2, route, gates, counts = _mix_call(
        x_prompt, x_sample, mod, norm1_g, norm2_g, u_a, y_s5, w_in_bf[:, n_u:], w_pool[0].astype(BF16),
        pool_scale, w_proj_a[0].astype(BF16), w_glu[0].astype(BF16), b_glu, w_proj_b[0].astype(BF16),
        w_out[0].astype(BF16), w_r, b_r)

    dest8, vblock, vexpert, vlo, vnext = _plan_call(counts[:, 0].astype(jnp.int32), route, counts)
    dest = dest8[:, :TOPK, :].reshape(N_TILES, 1, TOPK * TILE_MIX)
    x_slots = _dispatch_call(dest, h2.reshape(N_TOK, D_PACKED))
    y_slots = _expert_call(vblock, vexpert, vlo, vnext, x_slots, w_exp_gate[0], w_exp_up[0], w_exp_down[0])

    final_g2 = final_g[None, :]
    y_prompt = _final_call(x1, mod, y_slots, dest, gates, final_g2, 0, N_PROMPT)
    y_sample = _final_call(x1, mod, y_slots, dest, gates, final_g2, N_PROMPT, N_SAMPLE)
    return (y_prompt, y_sample)
```

```python
import math

import jax
import jax.numpy as jnp
from jax import lax
from jax.experimental import pallas as pl
from jax.experimental.pallas import tpu as pltpu

F32 = jnp.float32
BF16 = jnp.bfloat16

D_MODEL = 2048
SEQ = 4096
N_PROMPT = 2
N_SAMPLE = 8
N_SEQ = N_PROMPT + N_SAMPLE
N_TOK = N_SEQ * SEQ
EPS = 1e-6
N_MOD = 6

POOL_WINDOWS = (2, 4, 8, 16)
POOL_GROUP = 256
POOL_WIDTH = 1024
POOL_HALO = 16

SSM_GROUPS = 32
SSM_P = 16
SSM_N = 64
SSM_WIDTH = 512
SSM_CHUNK = 32
SSM_NCHUNK = SEQ // SSM_CHUNK
SSM_COLS = SSM_CHUNK * SSM_P
SSM_STATE_ROWS = 4 * SSM_N
SSM_GROUPS_PER_STEP = 8
SSM_SCAN_STEPS = 7
SSM_TAP_LANES = 2 * SSM_COLS

N_GROUPS = 4
EXPERTS_PER_GROUP = 8
N_EXPERTS = 32
TOPK = 2
D_EXPERT = 512
N_ASSIGN = N_TOK * TOPK
ROUTER_COLS = 128

TILE_INPROJ = 512
TILE_MIX = 256
N_TILES = N_TOK // TILE_MIX
TILES_PER_SEQ = SEQ // TILE_MIX
ROUTE_ROWS = 8
D_PACKED = D_MODEL // 2
MOE_ROWS = 256
MOE_SHIFT = 8
MOE_BLOCKS = N_ASSIGN // MOE_ROWS
N_VISITS = MOE_BLOCKS + N_EXPERTS - 1

VMEM_LIMIT = 60 * 1024 * 1024


def _params(sem, vmem=VMEM_LIMIT):
    return pltpu.CompilerParams(dimension_semantics=sem, vmem_limit_bytes=vmem)


def _resident(shape):
    zeros = (0,) * len(shape)
    return pl.BlockSpec(shape, lambda *_: zeros, pipeline_mode=pl.Buffered(1))


def _ada_norm(x, gain, scale, shift):
    r = lax.rsqrt(jnp.mean(x * x, axis=-1, keepdims=True) + EPS)
    return (x * r * gain) * (1.0 + scale) + shift


def _regroup_matrix(n_outer, n_inner):
    n = n_outer * n_inner
    dst = lax.broadcasted_iota(jnp.int32, (n, n), 0)
    src = lax.broadcasted_iota(jnp.int32, (n, n), 1)
    shift = n_outer.bit_length() - 1
    return (src == (dst & (n_outer - 1)) * n_inner + lax.shift_right_logical(dst, shift)).astype(BF16)


def _pack_bf16_pairs(x):
    w = x.shape[1] // 2
    lo = lax.bitcast_convert_type(x[:, :w].astype(BF16).astype(F32), jnp.uint32)
    hi = lax.bitcast_convert_type(x[:, w:].astype(BF16).astype(F32), jnp.uint32)
    return hi | (lo >> 16)


def _unpack_bf16_pairs(p):
    lo = lax.bitcast_convert_type(p << 16, F32)
    hi = lax.bitcast_convert_type(p & jnp.uint32(0xFFFF0000), F32)
    return lo, hi


def _mod_kernel(c_ref, w_ref, b_ref, o_ref):
    c = c_ref[...]
    s = c * jax.nn.sigmoid(c)
    o_ref[...] = jnp.dot(s.astype(BF16), w_ref[...].astype(BF16), preferred_element_type=F32) + b_ref[...]


def _mod_call(c_pad, w_ada, b_ada):
    n = w_ada.shape[1]
    tn = 1024
    return pl.pallas_call(
        _mod_kernel,
        grid=(n // tn,),
        in_specs=[pl.BlockSpec(c_pad.shape, lambda j: (0, 0)),
                  pl.BlockSpec((D_MODEL, tn), lambda j: (0, j)),
                  pl.BlockSpec((1, tn), lambda j: (0, j))],
        out_specs=pl.BlockSpec((c_pad.shape[0], tn), lambda j: (0, j)),
        out_shape=jax.ShapeDtypeStruct((c_pad.shape[0], n), F32),
        compiler_params=_params(("arbitrary",)),
        name="mod",
    )(c_pad, w_ada, b_ada)


def _x_specs(tile):
    last = SEQ // tile - 1
    xp = pl.BlockSpec((1, tile, D_MODEL),
                      lambda b, i: (jnp.minimum(b, N_PROMPT - 1), jnp.where(b < N_PROMPT, i, last), 0))
    xs = pl.BlockSpec((1, tile, D_MODEL),
                      lambda b, i: (jnp.maximum(b - N_PROMPT, 0), jnp.where(b < N_PROMPT, 0, i), 0))
    return xp, xs


def _load_x(xp_ref, xs_ref):
    return jnp.where(pl.program_id(0) < N_PROMPT, xp_ref[0], xs_ref[0])


def _inproj_kernel(xp_ref, xs_ref, mod_ref, g_ref, w_ref, ua_ref, ub_ref):
    x = _load_x(xp_ref, xs_ref)
    h = _ada_norm(x, g_ref[...], mod_ref[0, 1:2, :], mod_ref[0, 0:1, :])
    p = jnp.dot(h.astype(BF16), w_ref[...], preferred_element_type=F32)
    ua_ref[0] = p[:, :POOL_WIDTH].astype(BF16)
    n_chunk = TILE_INPROJ // SSM_CHUNK
    ub = jnp.dot(_regroup_matrix(n_chunk, SSM_CHUNK), p[:, POOL_WIDTH:].astype(BF16), preferred_element_type=F32)
    ub_ref[...] = ub.reshape(SSM_CHUNK, n_chunk, SSM_WIDTH).astype(BF16)


def _inproj_call(x_prompt, x_sample, mod, norm_g, w_u):
    tile = TILE_INPROJ
    xp, xs = _x_specs(tile)
    return pl.pallas_call(
        _inproj_kernel,
        grid=(N_SEQ, SEQ // tile),
        in_specs=[xp, xs,
                  pl.BlockSpec((1, N_MOD, D_MODEL), lambda b, i: (b, 0, 0)),
                  _resident((1, D_MODEL)),
                  _resident(w_u.shape)],
        out_specs=[pl.BlockSpec((1, tile, POOL_WIDTH), lambda b, i: (b, i, 0)),
                   pl.BlockSpec((None, SSM_CHUNK, tile // SSM_CHUNK, SSM_WIDTH), lambda b, i: (b, 0, i, 0))],
        out_shape=[jax.ShapeDtypeStruct((N_SEQ, SEQ, POOL_WIDTH), BF16),
                   jax.ShapeDtypeStruct((N_SEQ, SSM_CHUNK, SSM_NCHUNK, SSM_WIDTH), BF16)],
        compiler_params=_params(("arbitrary", "arbitrary")),
        name="inproj",
    )(x_prompt, x_sample, mod, norm_g, w_u)


def _s5_kernel(u_ref, wst_ref, taps_ref, wot_ref, a_ref, y_ref, ut_ref, yt_ref, apow_ref, tt_ref):
    n, nc, ng = SSM_N, SSM_NCHUNK, SSM_GROUPS_PER_STEP

    @pl.when(pl.program_id(1) == 0)
    def _():
        for g in range(ng):
            a = a_ref[g]
            for k in range(SSM_SCAN_STEPS):
                apow_ref[g, k] = a
                f_re, f_im, b_re, b_im = a[0:n], a[n:2 * n], a[2 * n:3 * n], a[3 * n:4 * n]
                a = jnp.concatenate([f_re * f_re - f_im * f_im, 2.0 * (f_re * f_im),
                                     b_re * b_re - b_im * b_im, 2.0 * (b_re * b_im)], axis=0)
            strip = taps_ref[g]
            for t in range(SSM_CHUNK):
                start = (SSM_CHUNK - 1 - t) * SSM_P
                window = pltpu.roll(strip, SSM_TAP_LANES - start, 1) if start else strip
                tt_ref[g, pl.ds(t * SSM_P, SSM_P), :] = window[:, :SSM_COLS].astype(BF16)

    for s in range(SSM_CHUNK):
        blk = u_ref[s].astype(F32).T
        ut_ref[:, pl.ds(s * SSM_P, SSM_P), :] = blk.reshape(ng, SSM_P, nc).astype(BF16)

    lane = lax.broadcasted_iota(jnp.int32, (n, nc), 1)

    def shifted(x, d, forward):
        if forward:
            return jnp.where(lane >= d, pltpu.roll(x, d, 1), 0.0)
        return jnp.where(lane < nc - d, pltpu.roll(x, nc - d, 1), 0.0)

    def scan(x_re, x_im, g, re_row, im_row, forward):
        for k in range(SSM_SCAN_STEPS):
            a_re = apow_ref[g, k, pl.ds(re_row, n), :]
            a_im = apow_ref[g, k, pl.ds(im_row, n), :]
            s_re = shifted(x_re, 2 ** k, forward)
            s_im = shifted(x_im, 2 ** k, forward)
            x_re, x_im = x_re + (a_re * s_re - a_im * s_im), x_im + (a_re * s_im + a_im * s_re)
        return x_re, x_im

    def per_group(g, carry):
        ut = ut_ref[g]
        st = jnp.dot(wst_ref[g], ut, preferred_element_type=F32)
        f_re, f_im = scan(st[0:n], st[n:2 * n], g, 0, n, True)
        b_re, b_im = scan(st[2 * n:3 * n], st[3 * n:4 * n], g, 2 * n, 3 * n, False)
        carried = jnp.concatenate([shifted(f_re, 1, True), shifted(f_im, 1, True),
                                   shifted(b_re, 1, False), shifted(b_im, 1, False)], axis=0)
        yt_ref[g] = (jnp.dot(tt_ref[g], ut, preferred_element_type=F32)
                     + jnp.dot(wot_ref[g], carried.astype(BF16), preferred_element_type=F32))
        return carry

    lax.fori_loop(0, ng, per_group, 0, unroll=2)
    for t in range(SSM_CHUNK):
        y_ref[t] = yt_ref[:, pl.ds(t * SSM_P, SSM_P), :].reshape(ng * SSM_P, nc).T


def _s5_call(u_ph, wst, tap_strip, wot, a_chunk):
    ng = SSM_GROUPS_PER_STEP
    seq = pl.BlockSpec((None, SSM_CHUNK, SSM_NCHUNK, ng * SSM_P), lambda q, b: (b, 0, 0, q))
    mat = lambda rows, cols: pl.BlockSpec((ng, rows, cols), lambda q, b: (q, 0, 0))
    return pl.pallas_call(
        _s5_kernel,
        grid=(SSM_GROUPS // ng, N_SEQ),
        in_specs=[seq, mat(SSM_STATE_ROWS, SSM_COLS), mat(SSM_P, SSM_TAP_LANES), mat(SSM_COLS, SSM_STATE_ROWS),
                  mat(SSM_STATE_ROWS, SSM_NCHUNK)],
        out_specs=seq,
        out_shape=jax.ShapeDtypeStruct((N_SEQ, SSM_CHUNK, SSM_NCHUNK, SSM_WIDTH), F32),
        scratch_shapes=[pltpu.VMEM((ng, SSM_COLS, SSM_NCHUNK), BF16),
                        pltpu.VMEM((ng, SSM_COLS, SSM_NCHUNK), F32),
                        pltpu.VMEM((ng, SSM_SCAN_STEPS, SSM_STATE_ROWS, SSM_NCHUNK), F32),
                        pltpu.VMEM((ng, SSM_COLS, SSM_COLS), BF16)],
        compiler_params=_params(("arbitrary", "arbitrary")),
        name="s5",
    )(u_ph, wst, tap_strip, wot, a_chunk)


def _s5_direction(a_re, a_im, log_dt, b_re, b_im, c_re, c_im):
    dt = jnp.exp(log_dt)[:, None]
    k = jnp.arange(SSM_CHUNK + 1, dtype=F32)[None, :, None]
    mag = jnp.exp(k * (a_re * dt)[:, None, :])
    ang = k * (a_im * dt)[:, None, :]
    pw_re = mag * jnp.cos(ang)
    pw_im = mag * jnp.sin(ang)
    ab_re, ab_im = pw_re[:, 1], pw_im[:, 1]
    den = a_re * a_re + a_im * a_im
    q_re = ((ab_re - 1.0) * a_re + ab_im * a_im) / den
    q_im = (ab_im * a_re - (ab_re - 1.0) * a_im) / den
    bb_re = q_re[:, :, None] * b_re - q_im[:, :, None] * b_im
    bb_im = q_re[:, :, None] * b_im + q_im[:, :, None] * b_re
    cp_re = c_re[:, None] * pw_re[:, :, None, :] - c_im[:, None] * pw_im[:, :, None, :]
    cp_im = c_re[:, None] * pw_im[:, :, None, :] + c_im[:, None] * pw_re[:, :, None, :]
    taps = jnp.einsum('gkqn,gnp->gkqp', jnp.concatenate([cp_re[:, :SSM_CHUNK], -cp_im[:, :SSM_CHUNK]], axis=-1),
                      jnp.concatenate([bb_re, bb_im], axis=1), precision=lax.Precision.HIGH)
    return pw_re, pw_im, bb_re, bb_im, cp_re, cp_im, taps


def _s5_operators(fwd, bwd, ssm_d):
    g, q, p, n = SSM_GROUPS, SSM_CHUNK, SSM_P, SSM_N
    pf_re, pf_im, bf_re, bf_im, cf_re, cf_im, taps_f = _s5_direction(*fwd)
    pb_re, pb_im, bb_re, bb_im, cb_re, cb_im, taps_b = _s5_direction(*bwd)
    descending, ascending = slice(q - 1, None, -1), slice(0, q)
    from_one, down_to_one = slice(1, q + 1), slice(q, 0, -1)

    centre = taps_f[:, 0] + taps_b[:, 0] + jnp.eye(p, dtype=F32)[None] * ssm_d.reshape(g, 1, p)
    by_lag = jnp.concatenate([taps_b[:, :0:-1], centre[:, None], taps_f[:, 1:]], axis=1)
    strip = by_lag[:, ::-1].transpose(0, 2, 1, 3).reshape(g, p, (2 * q - 1) * p)
    strip = jnp.pad(strip, ((0, 0), (0, 0), (0, SSM_TAP_LANES - (2 * q - 1) * p)))

    col = jnp.arange(q * p)
    rep_s = (col[None, :] // p == jnp.arange(q)[:, None]).astype(F32)
    tile_p = (col[None, :] % p == jnp.arange(p)[:, None]).astype(F32)
    expand = lambda x, m: jnp.einsum('gnk,kl->gnl', x, m, precision=lax.Precision.HIGH)

    def state_in(pw_re, pw_im, b_re, b_im, powers):
        a_re = expand(pw_re[:, powers].transpose(0, 2, 1), rep_s)
        a_im = expand(pw_im[:, powers].transpose(0, 2, 1), rep_s)
        b_re, b_im = expand(b_re, tile_p), expand(b_im, tile_p)
        return a_re * b_re - a_im * b_im, a_re * b_im + a_im * b_re

    wsf_re, wsf_im = state_in(pf_re[:, :q], pf_im[:, :q], bf_re, bf_im, descending)
    wsb_re, wsb_im = state_in(pb_re, pb_im, bb_re, bb_im, ascending)
    wst = jnp.concatenate([wsf_re, wsf_im, wsb_re, wsb_im], axis=1)

    def state_out(cp_re, cp_im, powers):
        o_re = cp_re[:, powers].reshape(g, q * p, n)
        o_im = -cp_im[:, powers].reshape(g, q * p, n)
        return o_re, o_im

    of_re, of_im = state_out(cf_re, cf_im, from_one)
    ob_re, ob_im = state_out(cb_re, cb_im, down_to_one)
    wot = jnp.concatenate([of_re, of_im, ob_re, ob_im], axis=2)
    a_chunk = jnp.concatenate([pf_re[:, q], pf_im[:, q], pb_re[:, q], pb_im[:, q]], axis=1)
    a_chunk = jnp.broadcast_to(a_chunk[:, :, None], (g, SSM_STATE_ROWS, SSM_NCHUNK))
    return wst.astype(BF16), strip, wot.astype(BF16), a_chunk


def _gelu_tanh(x):
    return 0.5 * x * (1.0 + jnp.tanh(math.sqrt(2.0 / math.pi) * (x + 0.044715 * (x * x * x))))


def _route_tile(logits, run_ref, route_ref, gate_ref, counts_ref):
    tile = logits.shape[0]
    neg = -jnp.inf
    lt = logits.T
    row8 = lax.broadcasted_iota(jnp.int32, (EXPERTS_PER_GROUP, tile), 0)
    gl = jnp.where(row8 < N_GROUPS, lt[N_EXPERTS:N_EXPERTS + 8], neg)
    gmax = jnp.max(gl, axis=0, keepdims=True)
    g_sel = jnp.min(jnp.where(gl == gmax, row8, 8), axis=0, keepdims=True)
    g_p = 1.0 / jnp.sum(jnp.exp(gl - gmax), axis=0, keepdims=True)
    in_grp = lt[0:EXPERTS_PER_GROUP]
    for g in range(1, N_GROUPS):
        in_grp = jnp.where(g_sel == g, lt[g * EXPERTS_PER_GROUP:(g + 1) * EXPERTS_PER_GROUP], in_grp)
    m1 = jnp.max(in_grp, axis=0, keepdims=True)
    i1 = jnp.min(jnp.where(in_grp == m1, row8, 8), axis=0, keepdims=True)
    rest = jnp.where(row8 == i1, neg, in_grp)
    m2 = jnp.max(rest, axis=0, keepdims=True)
    i2 = jnp.min(jnp.where(rest == m2, row8, 8), axis=0, keepdims=True)
    e21 = jnp.exp(m2 - m1)
    p1 = 1.0 / (1.0 + e21)
    eid1 = g_sel * EXPERTS_PER_GROUP + i1
    eid2 = g_sel * EXPERTS_PER_GROUP + i2

    row_e = lax.broadcasted_iota(jnp.int32, (N_EXPERTS, tile), 0)
    oh1 = (row_e == eid1).astype(F32)
    oh2 = (row_e == eid2).astype(F32)
    earlier = (lax.broadcasted_iota(jnp.int32, (tile, tile), 0)
               < lax.broadcasted_iota(jnp.int32, (tile, tile), 1)).astype(BF16)
    before1 = jnp.dot(oh1.astype(BF16), earlier, preferred_element_type=F32)
    before2 = jnp.dot(oh2.astype(BF16), earlier, preferred_element_type=F32)
    tot1 = jnp.sum(oh1, axis=1, keepdims=True)
    tot2 = jnp.sum(oh2, axis=1, keepdims=True)
    run = run_ref[:, 0:1]
    rank1 = jnp.sum(oh1 * (before1 + run), axis=0, keepdims=True)
    rank2 = jnp.sum(oh2 * (before2 + (run + tot1)), axis=0, keepdims=True)
    new_run = jnp.broadcast_to(run + tot1 + tot2, run_ref.shape)
    run_ref[...] = new_run
    counts_ref[...] = new_run
    zi = jnp.zeros((ROUTE_ROWS - 4, tile), jnp.int32)
    route_ref[0] = jnp.concatenate([eid1, eid2, rank1.astype(jnp.int32), rank2.astype(jnp.int32), zi], axis=0)
    zf = jnp.zeros((ROUTE_ROWS - 2, tile), F32)
    gate_ref[0] = jnp.concatenate([g_p * p1, g_p * (e21 * p1), zf], axis=0)


def _mix_kernel(xp_ref, xs_ref, mod_ref, g1_ref, g2_ref, ua_ref, ua_prev_ref, ua_next_ref, ys_ref,
                wg_ref, wpool_ref, pscale_ref, wpa_ref, wglu_ref, bglu_ref, wpb_ref, wout_ref,
                wr_ref, br_ref,
                x1_ref, h2_ref, route_ref, gate_ref, counts_ref, ext_ref, diff_ref, merged_ref, run_ref):
    tile = TILE_MIX
    i = pl.program_id(1)

    @pl.when((pl.program_id(0) == 0) & (i == 0))
    def _():
        run_ref[...] = jnp.zeros_like(run_ref)

    x = _load_x(xp_ref, xs_ref)
    h = _ada_norm(x, g1_ref[...], mod_ref[0, 1:2, :], mod_ref[0, 0:1, :]).astype(BF16)

    first = i == 0
    last = i == pl.num_programs(1) - 1
    ext_ref[pl.ds(0, POOL_HALO), :] = jnp.where(first, 0.0, ua_prev_ref[0].astype(F32))
    ext_ref[pl.ds(POOL_HALO, tile), :] = ua_ref[0].astype(F32)
    ext_ref[pl.ds(POOL_HALO + tile, POOL_HALO), :] = jnp.where(last, 0.0, ua_next_ref[0].astype(F32))
    pos = i * tile + lax.broadcasted_iota(jnp.int32, (tile, 1), 0)
    for k, w in enumerate(POOL_WINDOWS):
        cols = pl.ds(k * POOL_GROUP, POOL_GROUP)
        lo = jnp.maximum(pos - w // 2, 0)
        hi = jnp.minimum(pos + (w - 1 - w // 2), SEQ - 1)
        inv_cnt = 1.0 / (hi - lo + 1).astype(F32)
        acc = ext_ref[pl.ds(POOL_HALO - w // 2, tile), cols]
        for j in range(1, w):
            acc = acc + ext_ref[pl.ds(POOL_HALO - w // 2 + j, tile), cols]
        diff = acc * inv_cnt - ext_ref[pl.ds(POOL_HALO, tile), cols]
        mixed = jnp.dot(diff.astype(BF16), wpool_ref[k], preferred_element_type=F32)
        diff_ref[:, cols] = (mixed * pscale_ref[:, cols]).astype(BF16)

    z = _gelu_tanh(ys_ref[...].reshape(tile, SSM_WIDTH))
    zg = z * jax.nn.sigmoid(jnp.dot(z.astype(BF16), wglu_ref[...], preferred_element_type=F32) + bglu_ref[...])
    zg = jnp.dot(_regroup_matrix(SSM_CHUNK, tile // SSM_CHUNK), zg.astype(BF16),
                 preferred_element_type=F32).astype(BF16)
    pa = diff_ref[...]

    chunk = 1024
    for j in range(D_MODEL // chunk):
        c0 = j * chunk
        g_a = jnp.dot(h, wg_ref[:, pl.ds(c0, chunk)], preferred_element_type=F32)
        y_a = jnp.dot(pa, wpa_ref[:, pl.ds(c0, chunk)], preferred_element_type=F32)
        m = jax.nn.sigmoid(g_a) * y_a
        g_b = jnp.dot(h, wg_ref[:, pl.ds(D_MODEL + c0, chunk)], preferred_element_type=F32)
        y_b = jnp.dot(zg, wpb_ref[:, pl.ds(c0, chunk)], preferred_element_type=F32)
        m = m + jax.nn.sigmoid(g_b) * y_b
        merged_ref[:, pl.ds(c0, chunk)] = m.astype(BF16)

    x1 = x + mod_ref[0, 2:3, :] * jnp.dot(merged_ref[...], wout_ref[...], preferred_element_type=F32)
    x1_ref[0] = x1
    h2 = _ada_norm(x1, g2_ref[...], mod_ref[0, 4:5, :], mod_ref[0, 3:4, :])
    h2_ref[0] = _pack_bf16_pairs(h2)
    h2_hi = h2.astype(BF16)
    h2_lo = (h2 - h2_hi.astype(F32)).astype(BF16)
    prod = jnp.dot(jnp.concatenate([h2_hi, h2_lo], axis=0), wr_ref[...], preferred_element_type=F32)
    logits = (prod[:tile, :ROUTER_COLS]
              + (prod[tile:, :ROUTER_COLS] + prod[:tile, ROUTER_COLS:])) + br_ref[...]
    _route_tile(logits, run_ref, route_ref, gate_ref, counts_ref)


def _mix_call(x_prompt, x_sample, mod, norm1_g, norm2_g, u_a, y_s5, w_g, w_pool, pool_scale, w_pa,
              w_glu, b_glu, w_pb, w_out, w_r, b_r):
    tile = TILE_MIX
    xp, xs = _x_specs(tile)
    halo_per_tile = tile // POOL_HALO
    n_halo = SEQ // POOL_HALO
    seq_tile = lambda width: pl.BlockSpec((1, tile, width), lambda b, i: (b, i, 0))
    return pl.pallas_call(
        _mix_kernel,
        grid=(N_SEQ, SEQ // tile),
        in_specs=[xp, xs,
                  pl.BlockSpec((1, N_MOD, D_MODEL), lambda b, i: (b, 0, 0)),
                  _resident((1, D_MODEL)), _resident((1, D_MODEL)),
                  seq_tile(POOL_WIDTH),
                  pl.BlockSpec((1, POOL_HALO, POOL_WIDTH),
                               lambda b, i: (b, jnp.maximum(i * halo_per_tile - 1, 0), 0)),
                  pl.BlockSpec((1, POOL_HALO, POOL_WIDTH),
                               lambda b, i: (b, jnp.minimum((i + 1) * halo_per_tile, n_halo - 1), 0)),
                  pl.BlockSpec((None, SSM_CHUNK, tile // SSM_CHUNK, SSM_WIDTH), lambda b, i: (b, 0, i, 0)),
                  _resident(w_g.shape), _resident(w_pool.shape), _resident(pool_scale.shape),
                  _resident(w_pa.shape), _resident(w_glu.shape), _resident(b_glu.shape),
                  _resident(w_pb.shape), _resident(w_out.shape), _resident(w_r.shape), _resident(b_r.shape)],
        out_specs=[seq_tile(D_MODEL), seq_tile(D_PACKED),
                   pl.BlockSpec((1, ROUTE_ROWS, tile), lambda b, i: (b * TILES_PER_SEQ + i, 0, 0)),
                   pl.BlockSpec((1, ROUTE_ROWS, tile), lambda b, i: (b * TILES_PER_SEQ + i, 0, 0)),
                   pl.BlockSpec((N_EXPERTS, 128), lambda b, i: (0, 0))],
        out_shape=[jax.ShapeDtypeStruct((N_SEQ, SEQ, D_MODEL), F32),
                   jax.ShapeDtypeStruct((N_SEQ, SEQ, D_PACKED), jnp.uint32),
                   jax.ShapeDtypeStruct((N_TILES, ROUTE_ROWS, tile), jnp.int32),
                   jax.ShapeDtypeStruct((N_TILES, ROUTE_ROWS, tile), F32),
                   jax.ShapeDtypeStruct((N_EXPERTS, 128), F32)],
        scratch_shapes=[pltpu.VMEM((tile + 2 * POOL_HALO, POOL_WIDTH), F32),
                        pltpu.VMEM((tile, POOL_WIDTH), BF16),
                        pltpu.VMEM((tile, D_MODEL), BF16),
                        pltpu.VMEM((N_EXPERTS, 128), F32)],
        compiler_params=_params(("arbitrary", "arbitrary")),
        name="mix",
    )(x_prompt, x_sample, mod, norm1_g, norm2_g, u_a, u_a, u_a, y_s5, w_g, w_pool, pool_scale, w_pa,
      w_glu, b_glu, w_pb, w_out, w_r, b_r)


def _plan_kernel(cnt_ref, route_ref, counts_ref, dest_ref, vblock_ref, vexpert_ref, vlo_ref, vnext_ref):
    below = (lax.broadcasted_iota(jnp.int32, (N_EXPERTS, N_EXPERTS), 1)
             < lax.broadcasted_iota(jnp.int32, (N_EXPERTS, N_EXPERTS), 0)).astype(F32)
    starts = jnp.dot(below, counts_ref[...], preferred_element_type=F32, precision=lax.Precision.HIGHEST)
    starts_b = jnp.broadcast_to(starts[:, 0:1], (N_EXPERTS, TILE_MIX))
    row_e = lax.broadcasted_iota(jnp.int32, (N_EXPERTS, TILE_MIX), 0)
    zi = jnp.zeros((ROUTE_ROWS - 2, TILE_MIX), jnp.int32)

    def per_tile(t, carry):
        r = route_ref[t]
        s1 = jnp.sum(jnp.where(row_e == r[0:1], starts_b, 0.0), axis=0, keepdims=True)
        s2 = jnp.sum(jnp.where(row_e == r[1:2], starts_b, 0.0), axis=0, keepdims=True)
        dest_ref[t] = jnp.concatenate([s1.astype(jnp.int32) + r[2:3], s2.astype(jnp.int32) + r[3:4], zi], axis=0)
        return carry

    lax.fori_loop(0, N_TILES, per_tile, 0)

    def per_expert(e, carry):
        v, start, last_e = carry
        cnt = cnt_ref[e]
        end = start + cnt
        first = lax.shift_right_logical(start, MOE_SHIFT)
        n_blk = jnp.where(cnt > 0, lax.shift_right_logical(end - 1, MOE_SHIFT) - first + 1, 0)

        def per_block(k, v):
            blk = first + k
            vblock_ref[v] = blk
            vexpert_ref[v] = e
            vlo_ref[v] = jnp.maximum(start - blk * MOE_ROWS, 0)
            return v + 1

        v = lax.fori_loop(0, n_blk, per_block, v)
        return v, end, jnp.where(cnt > 0, e, last_e)

    v, _, last_e = lax.fori_loop(0, N_EXPERTS, per_expert, (0, 0, 0))

    def idle(k, carry):
        vblock_ref[k] = MOE_BLOCKS - 1
        vexpert_ref[k] = last_e
        vlo_ref[k] = MOE_ROWS
        return carry

    lax.fori_loop(v, N_VISITS, idle, 0)

    def following(i, carry):
        nxt, later = carry
        k = N_VISITS - 1 - i
        e = vexpert_ref[k]
        nxt = jnp.where(e != later, later, nxt)
        vnext_ref[k] = nxt
        return nxt, e

    lax.fori_loop(0, N_VISITS, following, (-1, -1))


def _plan_call(cnt, route, counts):
    smem = pl.BlockSpec(memory_space=pltpu.SMEM)
    vmem = pl.BlockSpec(memory_space=pltpu.VMEM)
    visits = jax.ShapeDtypeStruct((N_VISITS,), jnp.int32)
    return pl.pallas_call(
        _plan_kernel,
        in_specs=[smem, vmem, vmem],
        out_specs=[vmem, smem, smem, smem, smem],
        out_shape=[jax.ShapeDtypeStruct((N_TILES, ROUTE_ROWS, TILE_MIX), jnp.int32), visits, visits, visits, visits],
        name="plan",
    )(cnt, route, counts)


def _dest_spec(index_map):
    return pl.BlockSpec((1, 1, TOPK * TILE_MIX), index_map, memory_space=pltpu.SMEM)


DISPATCH_SLOTS = 3


def _dispatch_kernel(dest_ref, h_ref, xs_ref, buf, in_sem, out_sem):
    tile = TILE_MIX
    t = pl.program_id(0)
    n = pl.num_programs(0)
    slot = lax.rem(t, DISPATCH_SLOTS)
    nxt = lax.rem(t + 1, DISPATCH_SLOTS)

    def load(i, s):
        return pltpu.make_async_copy(h_ref.at[pl.ds(i * tile, tile)], buf.at[s], in_sem.at[s])

    def drain(s):
        for k in range(TOPK):
            pltpu.make_async_copy(buf.at[s], xs_ref.at[pl.ds(0, tile)], out_sem.at[s]).wait()

    @pl.when(t == 0)
    def _():
        load(0, 0).start()

    @pl.when(t >= DISPATCH_SLOTS - 1)
    def _():
        drain(nxt)

    @pl.when(t + 1 < n)
    def _():
        load(t + 1, nxt).start()

    load(t, slot).wait()

    def row(r, carry):
        for k in range(TOPK):
            d = dest_ref[0, 0, k * tile + r]
            pltpu.make_async_copy(buf.at[slot, pl.ds(r, 1)], xs_ref.at[pl.ds(d, 1)], out_sem.at[slot]).start(priority=k)
        return carry

    lax.fori_loop(0, tile, row, 0, unroll=8)

    @pl.when(t == n - 1)
    def _():
        drain(lax.rem(t + DISPATCH_SLOTS - 1, DISPATCH_SLOTS))
        drain(slot)


def _dispatch_call(dest, h2):
    return pl.pallas_call(
        _dispatch_kernel,
        grid=(N_TILES,),
        in_specs=[_dest_spec(lambda t: (t, 0, 0)), pl.BlockSpec(memory_space=pl.ANY)],
        out_specs=pl.BlockSpec(memory_space=pl.ANY),
        out_shape=jax.ShapeDtypeStruct((N_ASSIGN, D_PACKED), jnp.uint32),
        scratch_shapes=[pltpu.VMEM((DISPATCH_SLOTS, TILE_MIX, D_PACKED), jnp.uint32),
                        pltpu.SemaphoreType.DMA((DISPATCH_SLOTS,)),
                        pltpu.SemaphoreType.DMA((DISPATCH_SLOTS,))],
        compiler_params=_params(("arbitrary",)),
        name="dispatch",
    )(dest, h2)


def _expert_kernel(vblock_ref, vexpert_ref, vlo_ref, vnext_ref, x_ref, wg_ref, wu_ref, wd_ref, o_ref,
                   wgu_s, wd_s, gbuf, ubuf, dbuf, run_ref, sem):
    v = pl.program_id(0)
    e = vexpert_ref[v]

    def weight_copies(expert, slot):
        return [pltpu.make_async_copy(src.at[expert], dst.at[slot], sem.at[slot])
                for src, dst in ((wg_ref, gbuf), (wu_ref, ubuf), (wd_ref, dbuf))]

    @pl.when(v == 0)
    def _():
        run_ref[0] = 0
        for c in weight_copies(e, 0):
            c.start()

    @pl.when((v == 0) | (e != vexpert_ref[jnp.maximum(v - 1, 0)]))
    def _():
        slot = run_ref[0] & 1
        run_ref[0] = run_ref[0] + 1
        for c in weight_copies(e, slot):
            c.wait()
        nxt = vnext_ref[v]

        @pl.when(nxt >= 0)
        def _():
            for c in weight_copies(nxt, 1 - slot):
                c.start()

        wgu_s[:, :D_EXPERT] = gbuf[slot].astype(BF16)
        wgu_s[:, D_EXPERT:] = ubuf[slot].astype(BF16)
        wd_s[...] = dbuf[slot].astype(BF16)

    lo = vlo_ref[v]

    @pl.when(lo < MOE_ROWS)
    def _():
        x_lo, x_hi = _unpack_bf16_pairs(x_ref[...])
        gu = (jnp.dot(x_lo.astype(BF16), wgu_s[pl.ds(0, D_PACKED), :], preferred_element_type=F32)
              + jnp.dot(x_hi.astype(BF16), wgu_s[pl.ds(D_PACKED, D_PACKED), :], preferred_element_type=F32))
        g = gu[:, :D_EXPERT]
        act = (g * jax.nn.sigmoid(g)) * gu[:, D_EXPERT:]
        res = _pack_bf16_pairs(jnp.dot(act.astype(BF16), wd_s[...], preferred_element_type=F32))

        @pl.when(lo == 0)
        def _():
            o_ref[...] = res

        @pl.when(lo > 0)
        def _():
            rows = lax.broadcasted_iota(jnp.int32, (MOE_ROWS, 1), 0)
            o_ref[...] = jnp.where(rows >= lo, res, o_ref[...])


def _expert_call(vblock, vexpert, vlo, vnext, x_slots, w_gate, w_up, w_down):
    hbm = pl.BlockSpec(memory_space=pl.ANY)
    grid_spec = pltpu.PrefetchScalarGridSpec(
        num_scalar_prefetch=4,
        grid=(N_VISITS,),
        in_specs=[pl.BlockSpec((MOE_ROWS, D_PACKED), lambda v, vb, ve, vl, vn: (vb[v], 0)), hbm, hbm, hbm],
        out_specs=pl.BlockSpec((MOE_ROWS, D_PACKED), lambda v, vb, ve, vl, vn: (vb[v], 0)),
        scratch_shapes=[pltpu.VMEM((D_MODEL, 2 * D_EXPERT), BF16),
                        pltpu.VMEM((D_EXPERT, D_MODEL), BF16),
                        pltpu.VMEM((2, D_MODEL, D_EXPERT), F32),
                        pltpu.VMEM((2, D_MODEL, D_EXPERT), F32),
                        pltpu.VMEM((2, D_EXPERT, D_MODEL), F32),
                        pltpu.SMEM((1,), jnp.int32),
                        pltpu.SemaphoreType.DMA((2,))],
    )
    return pl.pallas_call(
        _expert_kernel,
        grid_spec=grid_spec,
        out_shape=jax.ShapeDtypeStruct((N_ASSIGN, D_PACKED), jnp.uint32),
        compiler_params=_params(("arbitrary",)),
        name="experts",
    )(vblock, vexpert, vlo, vnext, x_slots, w_gate, w_up, w_down)


def _final_kernel(dest_ref, dest_next_ref, x1_ref, mod_ref, gate_ref, g_ref, y_ref, o_ref, rows_ref, sem):
    tile = TILE_MIX
    step = pl.program_id(0) * pl.num_programs(1) + pl.program_id(1)
    n_steps = pl.num_programs(0) * pl.num_programs(1)
    slot = lax.rem(step, 2)

    def fetch(dst_ref, into):
        def row(r, carry):
            for k in range(TOPK):
                d = dst_ref[0, 0, k * tile + r]
                pltpu.make_async_copy(y_ref.at[pl.ds(d, 1)], rows_ref.at[into, k, pl.ds(r, 1)], sem.at[into]).start(priority=k)
            return carry

        lax.fori_loop(0, tile, row, 0, unroll=8)

    @pl.when(step == 0)
    def _():
        fetch(dest_ref, 0)

    @pl.when(step + 1 < n_steps)
    def _():
        fetch(dest_next_ref, 1 - slot)

    for k in range(TOPK):
        pltpu.make_async_copy(y_ref.at[pl.ds(0, tile)], rows_ref.at[slot, k], sem.at[slot]).wait()
    gate = gate_ref[0].T
    lo0, hi0 = _unpack_bf16_pairs(rows_ref[slot, 0])
    lo1, hi1 = _unpack_bf16_pairs(rows_ref[slot, 1])
    moe = jnp.concatenate([gate[:, 0:1] * lo0 + gate[:, 1:2] * lo1, gate[:, 0:1] * hi0 + gate[:, 1:2] * hi1], axis=1)
    x2 = x1_ref[0] + mod_ref[0, 5:6, :] * moe
    r = lax.rsqrt(jnp.mean(x2 * x2, axis=-1, keepdims=True) + EPS)
    o_ref[0] = x2 * r * g_ref[...]


def _final_call(x1, mod, y_slots, dest, gates, final_g, first_seq, n_seq):
    tile = TILE_MIX
    tile_of = lambda b, i: (b + first_seq) * TILES_PER_SEQ + i
    last_tile = (first_seq + n_seq) * TILES_PER_SEQ - 1
    return pl.pallas_call(
        _final_kernel,
        grid=(n_seq, TILES_PER_SEQ),
        in_specs=[_dest_spec(lambda b, i: (tile_of(b, i), 0, 0)),
                  _dest_spec(lambda b, i: (jnp.minimum(tile_of(b, i) + 1, last_tile), 0, 0)),
                  pl.BlockSpec((1, tile, D_MODEL), lambda b, i: (b + first_seq, i, 0)),
                  pl.BlockSpec((1, N_MOD, D_MODEL), lambda b, i: (b + first_seq, 0, 0)),
                  pl.BlockSpec((1, ROUTE_ROWS, tile), lambda b, i: (tile_of(b, i), 0, 0)),
                  _resident((1, D_MODEL)),
                  pl.BlockSpec(memory_space=pl.ANY)],
        out_specs=pl.BlockSpec((1, tile, D_MODEL), lambda b, i: (b, i, 0)),
        out_shape=jax.ShapeDtypeStruct((n_seq, SEQ, D_MODEL), F32),
        scratch_shapes=[pltpu.VMEM((2, TOPK, tile, D_PACKED), jnp.uint32), pltpu.SemaphoreType.DMA((2,))],
        compiler_params=_params(("arbitrary", "arbitrary")),
        name="final",
    )(dest, dest, x1, mod, gates, final_g, y_slots)


def kernel(x_prompt, x_sample, c_prompt, c_sample, w_ada, b_ada, norm1_g, w_in, w_pool, pool_scale,
           ssm_a_re_f, ssm_a_im_f, ssm_log_dt_f, ssm_b_re_f, ssm_b_im_f, ssm_c_re_f, ssm_c_im_f,
           ssm_a_re_b, ssm_a_im_b, ssm_log_dt_b, ssm_b_re_b, ssm_b_im_b, ssm_c_re_b, ssm_c_im_b,
           ssm_d, w_glu, b_glu, w_proj_a, w_proj_b, w_out, norm2_g,
           w_grp, b_grp, w_router, b_router, w_exp_gate, w_exp_up, w_exp_down, final_g):
    n_u = POOL_WIDTH + SSM_WIDTH
    c_pad = jnp.concatenate([c_prompt, c_sample, jnp.zeros((16 - N_SEQ, D_MODEL), F32)], axis=0)
    mod = _mod_call(c_pad, w_ada[0], b_ada).reshape(16, N_MOD, D_MODEL)

    w_in_bf = w_in[0].astype(BF16)
    u_a, u_b = _inproj_call(x_prompt, x_sample, mod, norm1_g, w_in_bf[:, :n_u])

    fwd = (ssm_a_re_f[0], ssm_a_im_f[0], ssm_log_dt_f[0], ssm_b_re_f[0], ssm_b_im_f[0], ssm_c_re_f[0], ssm_c_im_f[0])
    bwd = (ssm_a_re_b[0], ssm_a_im_b[0], ssm_log_dt_b[0], ssm_b_re_b[0], ssm_b_im_b[0], ssm_c_re_b[0], ssm_c_im_b[0])
    y_s5 = _s5_call(u_b, *_s5_operators(fwd, bwd, ssm_d[0]))

    w_r = jnp.concatenate([w_router[0], w_grp[0],
                           jnp.zeros((D_MODEL, ROUTER_COLS - N_GROUPS - N_EXPERTS), F32)], axis=1)
    b_r = jnp.concatenate([b_router[0], b_grp[0],
                           jnp.zeros((ROUTER_COLS - N_GROUPS - N_EXPERTS,), F32)])[None, :]
    w_r_hi = w_r.astype(BF16)
    w_r = jnp.concatenate([w_r_hi, (w_r - w_r_hi.astype(F32)).astype(BF16)], axis=1)
    x1, h2, route, gates, counts = _mix_call(
        x_prompt, x_sample, mod, norm1_g, norm2_g, u_a, y_s5, w_in_bf[:, n_u:], w_pool[0].astype(BF16),
        pool_scale, w_proj_a[0].astype(BF16), w_glu[0].astype(BF16), b_glu, w_proj_b[0].astype(BF16),
        w_out[0].astype(BF16), w_r, b_r)

    dest8, vblock, vexpert, vlo, vnext = _plan_call(counts[:, 0].astype(jnp.int32), route, counts)
    dest = dest8[:, :TOPK, :].reshape(N_TILES, 1, TOPK * TILE_MIX)
    x_slots = _dispatch_call(dest, h2.reshape(N_TOK, D_PACKED))
    y_slots = _expert_call(vblock, vexpert, vlo, vnext, x_slots, w_exp_gate[0], w_exp_up[0], w_exp_down[0])

    final_g2 = final_g[None, :]
    y_prompt = _final_call(x1, mod, y_slots, dest, gates, final_g2, 0, N_PROMPT)
    y_sample = _final_call(x1, mod, y_slots, dest, gates, final_g2, N_PROMPT, N_SAMPLE)
    return (y_prompt, y_sample)
```
